```python
import jax, jax.numpy as jnp
from jax import lax
import numpy as np

D_MODEL = 1024
BATCH = 8
SEQ = 2048
DEPTH = 4

N_MIXERS = 3
N_A_LAYERS = (DEPTH + 2) // 3
N_B_LAYERS = (DEPTH + 1) // 3
N_C_LAYERS = DEPTH // 3
NORM_EPS = 1e-6
NEG_BIG = -1e30
LB_FLOOR = 1e-30

A_WIDTH = D_MODEL
A_HEAD_DIM = 128
A_HEADS = A_WIDTH // A_HEAD_DIM
A_CHUNK = 64

B_HEAD_DIM = 64
B_HEADS = D_MODEL // B_HEAD_DIM
B_WIDTH = B_HEADS * B_HEAD_DIM
DILATED_GROUPS = ((128, 1), (512, 4), (2048, 16))
B_N_GROUPS = len(DILATED_GROUPS)
ALIBI_MAX_EXP = 8.0

C_WIDTH = D_MODEL
C_CHUNK = 128
C_GROUPS = 8
C_GROUP_DIM = C_WIDTH // C_GROUPS

kernel_name = 'hybrid_hgrn2_dilated_gmlp'


def rms_norm(x, w):
    xf = x.astype(jnp.float32)
    y = xf * lax.rsqrt(jnp.mean(xf * xf, axis=-1, keepdims=True) + NORM_EPS)
    return (y * w.astype(jnp.float32)).astype(x.dtype)


def layer_norm(x, w, b):
    xf = x.astype(jnp.float32)
    mu = jnp.mean(xf, axis=-1, keepdims=True)
    xc = xf - mu
    y = xc * lax.rsqrt(jnp.mean(xc * xc, axis=-1, keepdims=True) + NORM_EPS)
    return (y * w.astype(jnp.float32) + b.astype(jnp.float32)).astype(x.dtype)


def hgrn2_mixer(h, w_in, lower_bound, o_norm_w, w_out):
    bsz, seq, _ = h.shape
    n_chunks = seq // A_CHUNK
    q, f, inp_v, g = jnp.split(h @ w_in, 4, axis=-1)
    f32 = jnp.float32
    q = jax.nn.silu(q.astype(f32))
    f = f.astype(f32)
    lb = lower_bound.astype(f32)
    k = (1.0 - lb) * jax.nn.sigmoid(-f)
    log_f = jnp.logaddexp(jnp.log(jnp.maximum(lb, LB_FLOOR)),
                          jnp.log1p(-lb) + jax.nn.log_sigmoid(f))

    def heads(t):
        return t.reshape(bsz, n_chunks, A_CHUNK, A_HEADS, A_HEAD_DIM).transpose(1, 0, 3, 2, 4)

    xs = (heads(q), heads(k), heads(inp_v.astype(f32)), heads(log_f))
    causal = jnp.tril(jnp.ones((A_CHUNK, A_CHUNK), dtype=bool))

    def chunk_step(state, inp):
        qc, kc, vc, lc = inp
        b = jnp.cumsum(lc, axis=-2)
        diff = b[..., :, None, :] - b[..., None, :, :]
        decay = jnp.exp(jnp.where(causal[:, :, None], diff, NEG_BIG))
        scores = jnp.einsum('bhtk,bhsk,bhtsk->bhts', qc, kc, decay)
        o = scores @ vc + jnp.einsum('bhtk,bhkv->bhtv', qc * jnp.exp(b), state)
        b_last = b[..., -1:, :]
        state = (jnp.exp(b_last[..., 0, :])[..., None] * state
                 + jnp.einsum('bhsk,bhsv->bhkv', kc * jnp.exp(b_last - b), vc))
        return state, o

    state0 = jnp.zeros((bsz, A_HEADS, A_HEAD_DIM, A_HEAD_DIM), f32)
    _, o = lax.scan(chunk_step, state0, xs)
    o = o.transpose(1, 0, 3, 2, 4).reshape(bsz, seq, A_HEADS, A_HEAD_DIM)
    o = rms_norm(o, o_norm_w).reshape(bsz, seq, A_WIDTH)
    y = (o * jax.nn.silu(g.astype(f32))).astype(h.dtype)
    return y @ w_out


def dilated_window_attention(q, k, v, window, dilation, slopes):
    bsz, nh, seq, hd = q.shape
    n_keys = window // dilation
    sub_len = seq // dilation
    n_blk = -(-sub_len // n_keys)
    pad = n_blk * n_keys - sub_len

    def to_blocks(t):
        t = t.reshape(bsz, nh, sub_len, dilation, hd).transpose(0, 1, 3, 2, 4)
        t = jnp.pad(t, ((0, 0), (0, 0), (0, 0), (0, pad), (0, 0)))
        return t.reshape(bsz, nh, dilation, n_blk, n_keys, hd)

    def with_prev(t):
        prev = jnp.pad(t, ((0, 0), (0, 0), (0, 0), (1, 0), (0, 0), (0, 0)))[:, :, :, :-1]
        return jnp.concatenate([prev, t], axis=-2)

    qb = to_blocks(q)
    kb = with_prev(to_blocks(k))
    vb = with_prev(to_blocks(v))
    scores = jnp.einsum('bhrnid,bhrnjd->bhrnij', qb, kb).astype(jnp.float32) * (hd ** -0.5)
    qi = jnp.arange(n_keys)[:, None]
    kj = jnp.arange(2 * n_keys)[None, :]
    dist = n_keys + qi - kj
    key_idx = (jnp.arange(n_blk)[:, None, None] - 1) * n_keys + kj
    valid = (dist >= 0) & (dist <= n_keys) & (key_idx >= 0)
    alibi = -slopes.astype(jnp.float32)[:, None, None, None, None] * (dist * dilation).astype(jnp.float32)
    scores = jnp.where(valid, scores + alibi, NEG_BIG)
    lse = jax.nn.logsumexp(scores, axis=-1)
    p = jnp.exp(scores - lse[..., None])
    o = jnp.einsum('bhrnij,bhrnjd->bhrnid', p.astype(v.dtype), vb)
    o = o.reshape(bsz, nh, dilation, n_blk * n_keys, hd)[:, :, :, :sub_len]
    o = o.transpose(0, 1, 3, 2, 4).reshape(bsz, nh, seq, hd)
    lse = lse.reshape(bsz, nh, dilation, n_blk * n_keys)[..., :sub_len]
    lse = lse.transpose(0, 1, 3, 2).reshape(bsz, nh, seq)
    return o, lse


def dilated_attention_mixer(h, w_in, q_norm_w, k_norm_w, w_out):
    bsz, seq, _ = h.shape
    proj = h @ w_in
    n_qkv = 3 * B_N_GROUPS * B_WIDTH
    qkv = proj[..., :n_qkv].reshape(bsz, seq, B_N_GROUPS, 3, B_HEADS, B_HEAD_DIM)
    g = proj[..., n_qkv:]
    n_total = B_N_GROUPS * B_HEADS
    slopes = jnp.exp2(-ALIBI_MAX_EXP * jnp.arange(1, n_total + 1, dtype=jnp.float32) / n_total)
    slopes = slopes.reshape(B_N_GROUPS, B_HEADS)
    outs, lses = [], []
    for gi, (window, dilation) in enumerate(DILATED_GROUPS):
        q = rms_norm(qkv[:, :, gi, 0], q_norm_w[gi]).transpose(0, 2, 1, 3)
        k = rms_norm(qkv[:, :, gi, 1], k_norm_w[gi]).transpose(0, 2, 1, 3)
        v = qkv[:, :, gi, 2].transpose(0, 2, 1, 3)
        o, lse = dilated_window_attention(q, k, v, window, dilation, slopes[gi])
        outs.append(o)
        lses.append(lse)
    weights = jax.nn.softmax(jnp.stack(lses), axis=0)
    o = jnp.einsum('gbhs,gbhsd->bshd', weights.astype(outs[0].dtype), jnp.stack(outs))
    o = o.reshape(bsz, seq, B_WIDTH)
    y = (o.astype(jnp.float32) * jax.nn.silu(g.astype(jnp.float32))).astype(h.dtype)
    return y @ w_out


def spatial_gating_mixer(h, w_in, v_norm_w, v_norm_b, w_s, b_s, w_out):
    bsz, seq, _ = h.shape
    u, v, g = jnp.split(h @ w_in, 3, axis=-1)
    u = jax.nn.gelu(u)
    v = layer_norm(jax.nn.gelu(v), v_norm_w, v_norm_b)
    v = v.reshape(bsz, seq // C_CHUNK, C_CHUNK, C_GROUPS, C_GROUP_DIM)
    w_causal = w_s * jnp.tril(jnp.ones((C_CHUNK, C_CHUNK), dtype=w_s.dtype))
    s = jnp.einsum('gts,bnsgc->bntgc', w_causal, v) + b_s.T[None, None, :, :, None]
    y = u * s.reshape(bsz, seq, C_WIDTH) * jax.nn.silu(g)
    return y @ w_out


def setup_inputs(seed: int = 0) -> dict:
    key = jax.random.key(seed)
    ks = jax.random.split(key, 16)
    nrm = jax.random.normal
    f32 = jnp.float32
    return {
        'x': nrm(ks[0], (BATCH, SEQ, D_MODEL), f32),
        'norm_w': 1.0 + 0.02 * nrm(ks[1], (DEPTH, D_MODEL), f32),
        'a_w_in': nrm(ks[2], (N_A_LAYERS, D_MODEL, 4 * A_WIDTH), f32) * D_MODEL ** -0.5,
        'a_lower_bounds': 0.1 * nrm(ks[3], (N_A_LAYERS, A_WIDTH), f32),
        'a_o_norm_w': 1.0 + 0.02 * nrm(ks[4], (N_A_LAYERS, A_HEAD_DIM), f32),
        'a_w_out': nrm(ks[5], (N_A_LAYERS, A_WIDTH, D_MODEL), f32) * A_WIDTH ** -0.5,
        'b_w_in': nrm(ks[6], (N_B_LAYERS, D_MODEL, (3 * B_N_GROUPS + 1) * B_WIDTH), f32) * D_MODEL ** -0.5,
        'b_q_norm_w': 1.0 + 0.02 * nrm(ks[7], (N_B_LAYERS, B_N_GROUPS, B_HEAD_DIM), f32),
        'b_k_norm_w': 1.0 + 0.02 * nrm(ks[8], (N_B_LAYERS, B_N_GROUPS, B_HEAD_DIM), f32),
        'b_w_out': nrm(ks[9], (N_B_LAYERS, B_WIDTH, D_MODEL), f32) * B_WIDTH ** -0.5,
        'c_w_in': nrm(ks[10], (N_C_LAYERS, D_MODEL, 3 * C_WIDTH), f32) * D_MODEL ** -0.5,
        'c_v_norm_w': 1.0 + 0.02 * nrm(ks[11], (N_C_LAYERS, C_WIDTH), f32),
        'c_v_norm_b': 0.02 * nrm(ks[12], (N_C_LAYERS, C_WIDTH), f32),
        'c_w_s': nrm(ks[13], (N_C_LAYERS, C_GROUPS, C_CHUNK, C_CHUNK), f32) * C_CHUNK ** -0.5,
        'c_b_s': 1.0 + 0.1 * nrm(ks[14], (N_C_LAYERS, C_GROUPS, C_CHUNK), f32),
        'c_w_out': nrm(ks[15], (N_C_LAYERS, C_WIDTH, D_MODEL), f32) * C_WIDTH ** -0.5,
    }


def reference(x, norm_w, a_w_in, a_lower_bounds, a_o_norm_w, a_w_out,
              b_w_in, b_q_norm_w, b_k_norm_w, b_w_out,
              c_w_in, c_v_norm_w, c_v_norm_b, c_w_s, c_b_s, c_w_out):
    lb_soft = jax.nn.softmax(a_lower_bounds.astype(jnp.float32), axis=0)
    lower_bounds = jnp.cumsum(lb_soft, axis=0) - lb_soft[0]
    for layer in range(DEPTH):
        h = rms_norm(x, norm_w[layer])
        kind, idx = layer % N_MIXERS, layer // N_MIXERS
        if kind == 0:
            y = hgrn2_mixer(h, a_w_in[idx], lower_bounds[idx], a_o_norm_w[idx], a_w_out[idx])
        elif kind == 1:
            y = dilated_attention_mixer(h, b_w_in[idx], b_q_norm_w[idx], b_k_norm_w[idx], b_w_out[idx])
        else:
            y = spatial_gating_mixer(h, c_w_in[idx], c_v_norm_w[idx], c_v_norm_b[idx],
                                     c_w_s[idx], c_b_s[idx], c_w_out[idx])
        x = x + y.astype(x.dtype)
    return x
```

```python
import functools

import numpy as np
import jax
import jax.numpy as jnp
from jax import lax
from jax.experimental import pallas as pl
from jax.experimental.pallas import tpu as pltpu

F32 = jnp.float32
BF16 = jnp.bfloat16

NORM_EPS = 1e-6
NEG_BIG = -1e30
LB_FLOOR = 1e-30

LANES = 128
VMEM_LIMIT_BYTES = 56 * 1024 * 1024

A_HEAD_DIM = 128
A_CHUNK = 128
B_HEAD_DIM = 64
DILATED_GROUPS = ((128, 1), (512, 4), (2048, 16))
B_N_GROUPS = len(DILATED_GROUPS)
B_BLOCK = 128
ALIBI_MAX_EXP = 8.0
C_CHUNK = 128
C_GROUPS = 8


def _sigmoid(x):
    return 1.0 / (1.0 + jnp.exp(-x))


def _silu(x):
    return x * _sigmoid(x)


def _gelu_tanh(x):
    c = np.float32(np.sqrt(2.0 / np.pi))
    return 0.5 * x * (1.0 + jnp.tanh(c * (x + 0.044715 * (x * x * x))))


def _rms_rows(xf, w):
    ms = jnp.mean(xf * xf, axis=-1, keepdims=True)
    return xf * lax.rsqrt(ms + NORM_EPS) * w


def _dot(a, b):
    return jnp.dot(a, b, preferred_element_type=F32)


def _dot_nt(a, b):
    return lax.dot_general(a, b, (((1,), (1,)), ((), ())), preferred_element_type=F32)


def _dot_tn(a, b):
    return lax.dot_general(a, b, (((0,), (0,)), ((), ())), preferred_element_type=F32)


def _params(sem):
    return pltpu.CompilerParams(dimension_semantics=sem, vmem_limit_bytes=VMEM_LIMIT_BYTES)


def _out_proj_kernel(y_ref, w_ref, x_ref, o_ref):
    o_ref[...] = x_ref[...] + _dot(y_ref[...], w_ref[...])


def _out_proj(y2d, w_bf16, x2d, tm=1024):
    m, k = y2d.shape
    n = w_bf16.shape[1]
    return pl.pallas_call(
        _out_proj_kernel,
        grid=(m // tm,),
        in_specs=[pl.BlockSpec((tm, k), lambda i: (i, 0)),
                  pl.BlockSpec((k, n), lambda i: (0, 0)),
                  pl.BlockSpec((tm, n), lambda i: (i, 0))],
        out_specs=pl.BlockSpec((tm, n), lambda i: (i, 0)),
        out_shape=jax.ShapeDtypeStruct((m, n), F32),
        compiler_params=_params(("arbitrary",)),
        name="out_proj",
    )(y2d, w_bf16, x2d)


def _hgrn_levels(chunk):
    return [1 << j for j in range(int(np.log2(chunk)))]


def _hgrn_sum_matrix(chunk):
    t = np.arange(chunk)[:, None]
    j = np.arange(chunk)[None, :]
    blocks = [(j <= t), (j > t)]
    for m in _hgrn_levels(chunk):
        u = t % (2 * m)
        base = t - u
        upper = (u >= m) & (j >= base + m) & (j <= t)
        lower = (u < m) & (j > t) & (j <= base + m - 1)
        blocks.append(upper | lower)
    return np.concatenate(blocks, axis=0).astype(np.float32)


def _hgrn_kernel(x_ref, nw_ref, w_ref, lbraw_ref, onw_ref, dall_ref, y_ref,
                 h_ref, proj_ref, e_ref, st_ref, lvl_ref, *, layer_idx, seq, chunk):
    hd = pl.program_id(1)
    dh = A_HEAD_DIM
    rb = 256

    @pl.when(hd == 0)
    def _():
        for i in range(seq // rb):
            xs = x_ref[0, i * rb:(i + 1) * rb, :]
            h_ref[i * rb:(i + 1) * rb, :] = _rms_rows(xs, nw_ref[...]).astype(BF16)

    pb = 512
    for i in range(seq // pb):
        proj_ref[i * pb:(i + 1) * pb, :] = _dot(h_ref[i * pb:(i + 1) * pb, :], w_ref[...])

    raw = lbraw_ref[...]
    n_layers = raw.shape[0]
    mx = raw[0:1, :]
    for i in range(1, n_layers):
        mx = jnp.maximum(mx, raw[i:i + 1, :])
    es = [jnp.exp(raw[i:i + 1, :] - mx) for i in range(n_layers)]
    z = es[0]
    for i in range(1, n_layers):
        z = z + es[i]
    soft = [e / z for e in es]
    csum = soft[0]
    for i in range(1, layer_idx + 1):
        csum = csum + soft[i]
    lb = csum - soft[0]
    log_lb = jnp.log(jnp.maximum(lb, LB_FLOOR))
    log1m_lb = jnp.log1p(-lb)
    one_m_lb = 1.0 - lb

    ti = lax.broadcasted_iota(jnp.int32, (chunk, chunk), 0)
    si = lax.broadcasted_iota(jnp.int32, (chunk, chunk), 1)
    lvl_ref[...] = jnp.where(si < ti, ti ^ si, jnp.where(si == ti, 0, -1))
    st_ref[...] = jnp.zeros_like(st_ref)
    onw = onw_ref[...]
    levels = _hgrn_levels(chunk)

    def body(c, carry):
        r0 = pl.multiple_of(c * chunk, chunk)
        pq = proj_ref[pl.ds(r0, chunk), 0 * dh:1 * dh]
        pf = proj_ref[pl.ds(r0, chunk), 1 * dh:2 * dh]
        v = proj_ref[pl.ds(r0, chunk), 2 * dh:3 * dh]
        q = _silu(pq)
        k = one_m_lb * _sigmoid(-pf)
        log_sig = jnp.minimum(pf, 0.0) - jnp.log1p(jnp.exp(-jnp.abs(pf)))
        bb = log1m_lb + log_sig
        log_f = jnp.maximum(log_lb, bb) + jnp.log1p(jnp.exp(-jnp.abs(log_lb - bb)))
        lf_hi = log_f.astype(BF16)
        lf_lo = (log_f - lf_hi.astype(F32)).astype(BF16)
        e_ref[...] = _dot(dall_ref[...], lf_hi) + _dot(dall_ref[...], lf_lo)

        v16 = v.astype(BF16)
        st = st_ref[...]
        qd = (q * jnp.exp(e_ref[0:chunk, :])).astype(BF16)
        o = _dot_nt(qd, st.astype(BF16))

        lvl = lvl_ref[...]
        a = jnp.where(lvl == 0, _dot_nt(q.astype(BF16), k.astype(BF16)), 0.0)
        for j, m in enumerate(levels):
            p = jnp.exp(e_ref[(2 + j) * chunk:(3 + j) * chunk, :])
            a_m = _dot_nt((q * p).astype(BF16), (k * p).astype(BF16))
            a = jnp.where(lvl >= m, a_m, a)
        o = o + _dot(a.astype(BF16), v16)

        kd = (k * jnp.exp(e_ref[chunk:2 * chunk, :])).astype(BF16)
        decay = jnp.exp(e_ref[chunk - 1:chunk, :])
        st_ref[...] = st * decay + _dot_tn(v16, kd)

        pg = proj_ref[pl.ds(r0, chunk), 3 * dh:4 * dh]
        y = _rms_rows(o, onw) * _silu(pg)
        y_ref[0, pl.ds(r0, chunk), :] = y.astype(BF16)
        return carry

    lax.fori_loop(0, seq // chunk, body, 0)


def _hgrn_mixer(x, norm_w_row, w_in_heads, lb_raw, o_norm_w_row, layer_idx):
    bsz, seq, d = x.shape
    dh = A_HEAD_DIM
    n_heads = d // dh
    chunk = A_CHUNK
    dall = jnp.asarray(_hgrn_sum_matrix(chunk), BF16)
    n_e = dall.shape[0]
    kern = functools.partial(_hgrn_kernel, layer_idx=layer_idx, seq=seq, chunk=chunk)
    return pl.pallas_call(
        kern,
        grid=(bsz, n_heads),
        in_specs=[pl.BlockSpec((1, seq, d), lambda b, h: (b, 0, 0)),
                  pl.BlockSpec((1, d), lambda b, h: (0, 0)),
                  pl.BlockSpec((d, 4 * dh), lambda b, h: (0, h)),
                  pl.BlockSpec((lb_raw.shape[0], dh), lambda b, h: (0, h)),
                  pl.BlockSpec((1, dh), lambda b, h: (0, 0)),
                  pl.BlockSpec((n_e, chunk), lambda b, h: (0, 0))],
        out_specs=pl.BlockSpec((1, seq, dh), lambda b, h: (b, 0, h)),
        out_shape=jax.ShapeDtypeStruct((bsz, seq, d), BF16),
        scratch_shapes=[pltpu.VMEM((seq, d), BF16),
                        pltpu.VMEM((seq, 4 * dh), F32),
                        pltpu.VMEM((n_e, chunk), F32),
                        pltpu.VMEM((dh, dh), F32),
                        pltpu.VMEM((chunk, chunk), jnp.int32)],
        compiler_params=_params(("arbitrary", "arbitrary")),
        name="hgrn_mixer",
    )(x, norm_w_row, w_in_heads, lb_raw, o_norm_w_row, dall)


def _bproj_kernel(x_ref, nw_ref, w_ref, hw_ref, eseg_ref, esegt_ref, o_ref, h_ref, hn_ref, *, seq, dils):
    j = pl.program_id(1)
    n_qkv_tiles = 3 * len(dils)

    @pl.when(j == 0)
    def _():
        d = x_ref.shape[-1]
        n_planes = d // LANES
        rb = 256
        for i in range(seq // rb):
            hn = _rms_rows(x_ref[0, i * rb:(i + 1) * rb, :], nw_ref[...])
            for c in range(n_planes):
                hn_ref[c, i * rb:(i + 1) * rb, :] = hn[:, c * LANES:(c + 1) * LANES]
        for g, dil in enumerate(dils):
            sub = seq // dil
            for r in range(dil):
                for c0 in range(0, sub, rb):
                    rows = min(rb, sub - c0)
                    if dil == 1:
                        idx = pl.ds(c0, rows)
                    else:
                        idx = pl.ds(r + c0 * dil, rows, stride=dil)
                    for c in range(n_planes):
                        h_ref[g, r * sub + c0:r * sub + c0 + rows, c * LANES:(c + 1) * LANES] = (
                            hn_ref[c, idx, :].astype(BF16))

    grp = jnp.where(j < n_qkv_tiles, j // 3, 0)
    kind = jnp.where(j < n_qkv_tiles, j % 3, 3)
    pb = 512

    def tile(i):
        return _dot(h_ref[grp, i * pb:(i + 1) * pb, :], w_ref[...])

    @pl.when(kind < 2)
    def _():
        hw = hw_ref[0]
        for i in range(seq // pb):
            y = tile(i)
            ssq = _dot((y * y).astype(BF16), eseg_ref[...])
            r = lax.rsqrt(ssq * (1.0 / B_HEAD_DIM) + NORM_EPS)
            r_hi = r.astype(BF16)
            r_lo = (r - r_hi.astype(F32)).astype(BF16)
            rexp = _dot(r_hi, esegt_ref[...]) + _dot(r_lo, esegt_ref[...])
            o_ref[0, i * pb:(i + 1) * pb, :] = (y * rexp * hw).astype(BF16)

    @pl.when(kind == 2)
    def _():
        for i in range(seq // pb):
            o_ref[0, i * pb:(i + 1) * pb, :] = tile(i).astype(BF16)

    @pl.when(kind == 3)
    def _():
        for i in range(seq // pb):
            o_ref[0, i * pb:(i + 1) * pb, :] = _silu(tile(i)).astype(BF16)


def _attn_in_proj(x, norm_w_row, w_bf16, head_w):
    bsz, seq, d = x.shape
    n_out = w_bf16.shape[1]
    tn = d
    dils = tuple(dil for _, dil in DILATED_GROUPS)
    n_heads = d // B_HEAD_DIM
    eseg = np.zeros((d, LANES), np.float32)
    eseg[np.arange(d), np.arange(d) // B_HEAD_DIM] = 1.0
    eseg = jnp.asarray(eseg, BF16)
    kern = functools.partial(_bproj_kernel, seq=seq, dils=dils)
    return pl.pallas_call(
        kern,
        grid=(bsz, n_out // tn),
        in_specs=[pl.BlockSpec((1, seq, d), lambda b, j: (b, 0, 0), pipeline_mode=pl.Buffered(1)),
                  pl.BlockSpec((1, d), lambda b, j: (0, 0)),
                  pl.BlockSpec((d, tn), lambda b, j: (0, j)),
                  pl.BlockSpec((1, 1, tn), lambda b, j: (j, 0, 0)),
                  pl.BlockSpec((d, LANES), lambda b, j: (0, 0)),
                  pl.BlockSpec((LANES, d), lambda b, j: (0, 0))],
        out_specs=pl.BlockSpec((1, seq, tn), lambda b, j: (b, 0, j)),
        out_shape=jax.ShapeDtypeStruct((bsz, seq, n_out), BF16),
        scratch_shapes=[pltpu.VMEM((len(dils), seq, d), BF16),
                        pltpu.VMEM((d // LANES, seq, LANES), F32)],
        compiler_params=_params(("arbitrary", "arbitrary")),
        name="attn_in_proj",
    )(x, norm_w_row, w_bf16, head_w, eseg, eseg.T)


def _attn_kernel(slopes_ref, q0, k0, v0, q1, k1, v1, q2, k2, v2, g_ref, o_ref,
                 num_ref, den_ref, max_ref, *, seq, dils):
    hp = pl.program_id(1)
    blk = B_BLOCK
    hd = B_HEAD_DIM
    n_heads = (LANES // hd) * pl.num_programs(1)
    qkv = ((q0, k0, v0), (q1, k1, v1), (q2, k2, v2))
    lane_q = lax.broadcasted_iota(jnp.int32, (blk, LANES), 1)
    head_a = lane_q < hd

    def window(g, dil, row0, tok0, has_prev):
        q_ref, k_ref, v_ref = qkv[g]
        nk = 2 * blk if has_prev else blk
        k_lo = row0 - blk if has_prev else row0
        q2d = q_ref[0, pl.ds(row0, blk), :]
        k2d = k_ref[0, pl.ds(k_lo, nk), :]
        v2d = v_ref[0, pl.ds(k_lo, nk), :]
        lane_k = lax.broadcasted_iota(jnp.int32, (nk, LANES), 1)
        qi = lax.broadcasted_iota(jnp.int32, (blk, nk), 0)
        kj = lax.broadcasted_iota(jnp.int32, (blk, nk), 1)
        dist = (blk + qi - kj) if has_prev else (qi - kj)
        valid = (dist >= 0) & (dist <= blk)
        dist_f = (dist * dil).astype(F32)
        outs = []
        for a in range(2):
            mine = head_a if a == 0 else jnp.logical_not(head_a)
            slope = slopes_ref[g * n_heads + 2 * hp + a]
            qa = jnp.where(mine, q2d, jnp.zeros_like(q2d))
            s = _dot_nt(qa, k2d)
            s = jnp.where(valid, s - slope * dist_f, NEG_BIG)
            m = jnp.max(s, axis=-1, keepdims=True)
            p = jnp.exp(s - m)
            mine_k = (lane_k < hd) if a == 0 else (lane_k >= hd)
            v_ext = jnp.where(mine_k, v2d, jnp.ones_like(v2d))
            outs.append((_dot(p.astype(BF16), v_ext), m))
        (oa, ma), (ob, mb) = outs
        num = jnp.where(head_a, oa, ob)
        den = jnp.where(head_a, ob, oa)
        mx = jnp.where(head_a, ma, mb)
        if dil == 1:
            idx = pl.ds(tok0, blk)
        else:
            idx = pl.ds(tok0, blk, stride=dil)
        num_ref[g, idx, :] = num
        den_ref[g, idx, :] = den
        max_ref[g, idx, :] = mx

    for g, dil in enumerate(dils):
        sub = seq // dil
        n_blk = sub // blk

        def first(r, carry, g=g, dil=dil, sub=sub):
            window(g, dil, pl.multiple_of(r * sub, blk), r, False)
            return carry

        lax.fori_loop(0, dil, first, 0)
        if n_blk > 1:
            def rest(i, carry, g=g, dil=dil, sub=sub, n_blk=n_blk):
                r = i // (n_blk - 1)
                b = i % (n_blk - 1) + 1
                window(g, dil, pl.multiple_of(r * sub + b * blk, blk), b * blk * dil + r, True)
                return carry

            lax.fori_loop(0, dil * (n_blk - 1), rest, 0)

    rb = 256
    n_g = len(dils)

    def merge(i, carry):
        r0 = pl.multiple_of(i * rb, rb)
        ms = [max_ref[g, pl.ds(r0, rb), :] for g in range(n_g)]
        mx = ms[0]
        for g in range(1, n_g):
            mx = jnp.maximum(mx, ms[g])
        num = jnp.zeros((rb, LANES), F32)
        den = jnp.zeros((rb, LANES), F32)
        for g in range(n_g):
            w = jnp.exp(ms[g] - mx)
            num = num + w * num_ref[g, pl.ds(r0, rb), :]
            w_sw = pltpu.roll(w, hd, axis=1)
            den = den + w_sw * den_ref[g, pl.ds(r0, rb), :]
        den = pltpu.roll(den, hd, axis=1)
        gate = g_ref[0, pl.ds(r0, rb), :].astype(F32)
        o_ref[0, pl.ds(r0, rb), :] = (num / den * gate).astype(BF16)
        return carry

    lax.fori_loop(0, seq // rb, merge, 0)


def _attention(proj, slopes):
    bsz, seq, n_out = proj.shape
    d = n_out // (3 * B_N_GROUPS + 1)
    n_pairs = d // LANES
    dils = tuple(dil for _, dil in DILATED_GROUPS)

    def col_spec(tile):
        return pl.BlockSpec((1, seq, LANES), lambda b, h, s, tile=tile: (b, 0, tile * n_pairs + h))

    kern = functools.partial(_attn_kernel, seq=seq, dils=dils)
    grid_spec = pltpu.PrefetchScalarGridSpec(
        num_scalar_prefetch=1,
        grid=(bsz, n_pairs),
        in_specs=[col_spec(t) for t in range(3 * B_N_GROUPS + 1)],
        out_specs=pl.BlockSpec((1, seq, LANES), lambda b, h, s: (b, 0, h)),
        scratch_shapes=[pltpu.VMEM((B_N_GROUPS, seq, LANES), F32),
                        pltpu.VMEM((B_N_GROUPS, seq, LANES), F32),
                        pltpu.VMEM((B_N_GROUPS, seq, LANES), F32)],
    )
    return pl.pallas_call(
        kern,
        grid_spec=grid_spec,
        out_shape=jax.ShapeDtypeStruct((bsz, seq, d), BF16),
        compiler_params=_params(("arbitrary", "arbitrary")),
        name="dilated_attention",
    )(slopes, *([proj] * (3 * B_N_GROUPS + 1)))


def _gmlp_kernel(x_ref, nw_ref, w_ref, vw_ref, vb_ref, ws_ref, bias_ref, y_ref,
                 h_ref, u_ref, v_ref, g_ref, *, tm):
    d = x_ref.shape[-1]
    cg = d // C_GROUPS
    rb = 256
    for i in range(tm // rb):
        h_ref[i * rb:(i + 1) * rb, :] = _rms_rows(x_ref[0, i * rb:(i + 1) * rb, :], nw_ref[...]).astype(BF16)
    for i in range(tm // rb):
        rows = slice(i * rb, (i + 1) * rb)
        hh = h_ref[rows, :]
        u_ref[rows, :] = _gelu_tanh(_dot(hh, w_ref[:, 0:d]))
        vv = _gelu_tanh(_dot(hh, w_ref[:, d:2 * d]))
        mu = jnp.mean(vv, axis=-1, keepdims=True)
        vc = vv - mu
        var = jnp.mean(vc * vc, axis=-1, keepdims=True)
        v_ref[rows, :] = (vc * lax.rsqrt(var + NORM_EPS) * vw_ref[...] + vb_ref[...]).astype(BF16)
        g_ref[rows, :] = _silu(_dot(hh, w_ref[:, 2 * d:3 * d]))
    ti = lax.broadcasted_iota(jnp.int32, (C_CHUNK, C_CHUNK), 0)
    si = lax.broadcasted_iota(jnp.int32, (C_CHUNK, C_CHUNK), 1)
    n_ch = tm // C_CHUNK
    for grp in range(C_GROUPS):
        cols = slice(grp * cg, (grp + 1) * cg)
        wc = jnp.where(si <= ti, ws_ref[grp], 0.0).astype(BF16)
        rhs = jnp.concatenate([v_ref[n * C_CHUNK:(n + 1) * C_CHUNK, cols] for n in range(n_ch)], axis=1)
        s = _dot(wc, rhs)
        bias = bias_ref[:, cols]
        for n in range(n_ch):
            rows = slice(n * C_CHUNK, (n + 1) * C_CHUNK)
            sn = s[:, n * cg:(n + 1) * cg] + bias
            y_ref[0, rows, cols] = (u_ref[rows, cols] * sn * g_ref[rows, cols]).astype(BF16)


def _gmlp_mixer(x, norm_w_row, w_bf16, vw_row, vb_row, w_s, bias_full, tm=512):
    bsz, seq, d = x.shape
    kern = functools.partial(_gmlp_kernel, tm=tm)
    return pl.pallas_call(
        kern,
        grid=(bsz, seq // tm),
        in_specs=[pl.BlockSpec((1, tm, d), lambda b, i: (b, i, 0)),
                  pl.BlockSpec((1, d), lambda b, i: (0, 0)),
                  pl.BlockSpec((d, 3 * d), lambda b, i: (0, 0)),
                  pl.BlockSpec((1, d), lambda b, i: (0, 0)),
                  pl.BlockSpec((1, d), lambda b, i: (0, 0)),
                  pl.BlockSpec(w_s.shape, lambda b, i: (0, 0, 0)),
                  pl.BlockSpec((C_CHUNK, d), lambda b, i: (0, 0))],
        out_specs=pl.BlockSpec((1, tm, d), lambda b, i: (b, i, 0)),
        out_shape=jax.ShapeDtypeStruct((bsz, seq, d), BF16),
        scratch_shapes=[pltpu.VMEM((tm, d), BF16),
                        pltpu.VMEM((tm, d), F32),
                        pltpu.VMEM((tm, d), BF16),
                        pltpu.VMEM((tm, d), F32)],
        compiler_params=_params(("arbitrary", "arbitrary")),
        name="gmlp_mixer",
    )(x, norm_w_row, w_bf16, vw_row, vb_row, w_s, bias_full)


def kernel(x, norm_w, a_w_in, a_lower_bounds, a_o_norm_w, a_w_out, b_w_in, b_q_norm_w, b_k_norm_w, b_w_out,
           c_w_in, c_v_norm_w, c_v_norm_b, c_w_s, c_b_s, c_w_out):
    bsz, seq, d = x.shape
    depth = norm_w.shape[0]
    n_mixers = 3
    assert seq == DILATED_GROUPS[-1][0] and seq % A_CHUNK == 0 and d % LANES == 0

    def residual(y, w_out, xin):
        out = _out_proj(y.reshape(bsz * seq, d), w_out.astype(BF16), xin.reshape(bsz * seq, d))
        return out.reshape(bsz, seq, d)

    for layer in range(depth):
        kind, idx = layer % n_mixers, layer // n_mixers
        nw = norm_w[layer][None, :]
        if kind == 0:
            n_heads = d // A_HEAD_DIM
            w = a_w_in[idx].reshape(d, 4, n_heads, A_HEAD_DIM).transpose(0, 2, 1, 3)
            w = w.reshape(d, 4 * d).astype(BF16)
            y = _hgrn_mixer(x, nw, w, a_lower_bounds, a_o_norm_w[idx][None, :], idx)
            x = residual(y, a_w_out[idx], x)
        elif kind == 1:
            n_heads = d // B_HEAD_DIM
            n_total = B_N_GROUPS * n_heads
            slopes = jnp.exp2(-ALIBI_MAX_EXP * jnp.arange(1, n_total + 1, dtype=F32) / n_total)
            rows = []
            for g in range(B_N_GROUPS):
                rows.append(jnp.tile(b_q_norm_w[idx, g], n_heads) * (B_HEAD_DIM ** -0.5))
                rows.append(jnp.tile(b_k_norm_w[idx, g], n_heads))
                rows.append(jnp.ones((d,), F32))
            rows.append(jnp.ones((d,), F32))
            head_w = jnp.stack(rows)[:, None, :]
            proj = _attn_in_proj(x, nw, b_w_in[idx].astype(BF16), head_w)
            y = _attention(proj, slopes)
            x = residual(y, b_w_out[idx], x)
        else:
            bias_full = jnp.repeat(c_b_s[idx].T, d // C_GROUPS, axis=1)
            y = _gmlp_mixer(x, nw, c_w_in[idx].astype(BF16), c_v_norm_w[idx][None, :],
                            c_v_norm_b[idx][None, :], c_w_s[idx], bias_full)
            x = residual(y, c_w_out[idx], x)
    return x
```

```python
import functools

import numpy as np
import jax
import jax.numpy as jnp
from jax import lax
from jax.experimental import pallas as pl
from jax.experimental.pallas import tpu as pltpu

F32 = jnp.float32
BF16 = jnp.bfloat16

NORM_EPS = 1e-6
NEG_BIG = -1e30
LB_FLOOR = 1e-30
LOG2E = float(np.log2(np.e))

LANES = 128
SUBLANES = 8
VMEM_LIMIT_BYTES = 56 * 1024 * 1024

A_HEAD_DIM = 128
A_CHUNK = 128
A_HEADS_PER_STEP = 2
B_HEAD_DIM = 64
DILATED_GROUPS = ((128, 1), (512, 4), (2048, 16))
B_N_GROUPS = len(DILATED_GROUPS)
B_BLOCK = 128
ALIBI_MAX_EXP = 8.0
C_CHUNK = 128
C_GROUPS = 8


def _sigmoid(x):
    return 1.0 / (1.0 + jnp.exp(-x))


def _silu(x):
    return x * _sigmoid(x)


def _gelu_tanh(x):
    c = np.float32(np.sqrt(2.0 / np.pi))
    return 0.5 * x * (1.0 + jnp.tanh(c * (x + 0.044715 * (x * x * x))))


def _rms_rows(xf, w):
    ms = jnp.mean(xf * xf, axis=-1, keepdims=True)
    return xf * lax.rsqrt(ms + NORM_EPS) * w


def _dot(a, b):
    return jnp.dot(a, b, preferred_element_type=F32)


def _dot_nt(a, b):
    return lax.dot_general(a, b, (((1,), (1,)), ((), ())), preferred_element_type=F32)


def _dot_tn(a, b):
    return lax.dot_general(a, b, (((0,), (0,)), ((), ())), preferred_element_type=F32)


def _params(sem):
    return pltpu.CompilerParams(dimension_semantics=sem, vmem_limit_bytes=VMEM_LIMIT_BYTES)


def _out_proj_kernel(y_ref, w_ref, x_ref, o_ref):
    o_ref[...] = x_ref[...] + _dot(y_ref[...], w_ref[...])


def _out_proj(y2d, w_bf16, x2d, tm=1024):
    m, k = y2d.shape
    n = w_bf16.shape[1]
    return pl.pallas_call(
        _out_proj_kernel,
        grid=(m // tm,),
        in_specs=[pl.BlockSpec((tm, k), lambda i: (i, 0)),
                  pl.BlockSpec((k, n), lambda i: (0, 0)),
                  pl.BlockSpec((tm, n), lambda i: (i, 0))],
        out_specs=pl.BlockSpec((tm, n), lambda i: (i, 0)),
        out_shape=jax.ShapeDtypeStruct((m, n), F32),
        compiler_params=_params(("arbitrary",)),
        name="out_proj",
    )(y2d, w_bf16, x2d)


def _hgrn_levels(chunk):
    return [1 << j for j in range(int(np.log2(chunk)))]


def _hgrn_level_exponents(b, log_f, b_view, sgn_ref, chunk):
    out = {}
    n_vregs = chunk // SUBLANES
    width = b.shape[-1]

    def row(r, n):
        return jnp.broadcast_to(b_view[pl.ds(r, 1), :], (n, width))

    for m in _hgrn_levels(chunk):
        if m >= SUBLANES:
            parts = []
            for j in range(chunk // (2 * m)):
                base = j * 2 * m
                ref_row = row(base + m - 1, m)
                parts.append(ref_row - b[base:base + m, :])
                parts.append(b[base + m:base + 2 * m, :] - ref_row)
            out[m] = jnp.concatenate(parts, axis=0)
        elif m == 4:
            ref = jnp.concatenate([row(SUBLANES * i + 3, SUBLANES) for i in range(n_vregs)], axis=0)
            out[m] = (b - ref) * sgn_ref[0]
        elif m == 2:
            lo = jnp.concatenate([row(SUBLANES * i + 1, SUBLANES) for i in range(n_vregs)], axis=0)
            hi = jnp.concatenate([row(SUBLANES * i + 5, SUBLANES) for i in range(n_vregs)], axis=0)
            ref = jnp.where(sgn_ref[0] > 0.0, hi, lo)
            out[m] = (b - ref) * sgn_ref[1]
        else:
            out[m] = log_f * sgn_ref[2]
    return out


def _hgrn_kernel(x_ref, nw_ref, w_ref, lbraw_ref, onw_ref, tri_ref, y_ref,
                 h_ref, proj_ref, b_ref, st_ref, lvl_ref, sgn_ref, *, layer_idx, seq, chunk, hps):
    step = pl.program_id(1)
    dh = A_HEAD_DIM
    rb = 256

    @pl.when(step == 0)
    def _():
        for i in range(seq // rb):
            xs = x_ref[0, i * rb:(i + 1) * rb, :]
            h_ref[i * rb:(i + 1) * rb, :] = _rms_rows(xs, nw_ref[...]).astype(BF16)

    pb = 512
    for i in range(seq // pb):
        proj_ref[i * pb:(i + 1) * pb, :] = _dot(h_ref[i * pb:(i + 1) * pb, :], w_ref[...])

    raw = lbraw_ref[...]
    n_layers = raw.shape[0]
    mx = raw[0:1, :]
    for i in range(1, n_layers):
        mx = jnp.maximum(mx, raw[i:i + 1, :])
    es = [jnp.exp(raw[i:i + 1, :] - mx) for i in range(n_layers)]
    z = es[0]
    for i in range(1, n_layers):
        z = z + es[i]
    soft = [e / z for e in es]
    csum = soft[0]
    for i in range(1, layer_idx + 1):
        csum = csum + soft[i]
    lb = csum - soft[0]
    log_lb = jnp.log(jnp.maximum(lb, LB_FLOOR))
    log1m_lb = jnp.log1p(-lb)
    one_m_lb = 1.0 - lb

    ti = lax.broadcasted_iota(jnp.int32, (chunk, chunk), 0)
    si = lax.broadcasted_iota(jnp.int32, (chunk, chunk), 1)
    lvl_ref[...] = jnp.where(si < ti, ti ^ si, jnp.where(si == ti, 0, -1))
    tr = lax.broadcasted_iota(jnp.int32, (chunk, dh), 0)
    sgn_ref[0] = jnp.where((tr & 4) != 0, 1.0, -1.0)
    sgn_ref[1] = jnp.where((tr & 2) != 0, 1.0, -1.0)
    sgn_ref[2] = jnp.where((tr & 1) != 0, 1.0, 0.0)
    st_ref[...] = jnp.zeros_like(st_ref)
    onw = onw_ref[...]
    levels = _hgrn_levels(chunk)
    heads = range(hps)

    def body(c, carry):
        r0 = pl.multiple_of(c * chunk, chunk)
        q, k, v16, log_f = [], [], [], []
        for a in heads:
            c0 = a * 4 * dh
            lbs = slice(a * dh, (a + 1) * dh)
            pq = proj_ref[pl.ds(r0, chunk), c0:c0 + dh]
            pf = proj_ref[pl.ds(r0, chunk), c0 + dh:c0 + 2 * dh]
            q.append(_silu(pq))
            ef = jnp.exp(-jnp.abs(pf))
            rf = 1.0 / (1.0 + ef)
            k.append(one_m_lb[:, lbs] * jnp.where(pf >= 0.0, ef * rf, rf))
            log_sig = jnp.minimum(pf, 0.0) - jnp.log(1.0 + ef)
            bb = log1m_lb[:, lbs] + log_sig
            ll = log_lb[:, lbs]
            log_f.append(jnp.maximum(ll, bb) + jnp.log(1.0 + jnp.exp(-jnp.abs(ll - bb))))
            v16.append(proj_ref[pl.ds(r0, chunk), c0 + 2 * dh:c0 + 3 * dh].astype(BF16))

        b = []
        for a in heads:
            lf_hi = log_f[a].astype(BF16)
            lf_lo = (log_f[a] - lf_hi.astype(F32)).astype(BF16)
            b.append(_dot(tri_ref[...], lf_hi) + _dot(tri_ref[...], lf_lo))
            b_ref[a] = b[a]

        expo, o, acc = [], [], []
        for a in heads:
            expo.append(_hgrn_level_exponents(b[a], log_f[a], b_ref.at[a], sgn_ref, chunk))
            qd = (q[a] * jnp.exp(b[a])).astype(BF16)
            o.append(_dot_nt(qd, st_ref[a].astype(BF16)))
            acc.append(jnp.where(lvl_ref[...] == 0, _dot_nt(q[a].astype(BF16), k[a].astype(BF16)), 0.0))
        for m in levels:
            for a in heads:
                p = jnp.exp(expo[a][m])
                a_m = _dot_nt((q[a] * p).astype(BF16), (k[a] * p).astype(BF16))
                acc[a] = jnp.where(lvl_ref[...] >= m, a_m, acc[a])

        for a in heads:
            c0 = a * 4 * dh
            oa = o[a] + _dot(acc[a].astype(BF16), v16[a])
            b_last = b_ref[a, chunk - 1:chunk, :]
            kd = (k[a] * jnp.exp(b_last - b[a])).astype(BF16)
            decay = jnp.exp(b_last)
            st_ref[a] = st_ref[a] * decay + _dot_tn(v16[a], kd)
            pg = proj_ref[pl.ds(r0, chunk), c0 + 3 * dh:c0 + 4 * dh]
            y = _rms_rows(oa, onw) * _silu(pg)
            y_ref[0, pl.ds(r0, chunk), a * dh:(a + 1) * dh] = y.astype(BF16)
        return carry

    lax.fori_loop(0, seq // chunk, body, 0)


def _hgrn_mixer(x, norm_w_row, w_in_heads, lb_raw, o_norm_w_row, layer_idx):
    bsz, seq, d = x.shape
    dh = A_HEAD_DIM
    hps = A_HEADS_PER_STEP
    n_steps = d // (dh * hps)
    chunk = A_CHUNK
    tri = jnp.asarray(np.tril(np.ones((chunk, chunk), np.float32)), BF16)
    kern = functools.partial(_hgrn_kernel, layer_idx=layer_idx, seq=seq, chunk=chunk, hps=hps)
    return pl.pallas_call(
        kern,
        grid=(bsz, n_steps),
        in_specs=[pl.BlockSpec((1, seq, d), lambda b, h: (b, 0, 0), pipeline_mode=pl.Buffered(1)),
                  pl.BlockSpec((1, d), lambda b, h: (0, 0)),
                  pl.BlockSpec((d, hps * 4 * dh), lambda b, h: (0, h)),
                  pl.BlockSpec((lb_raw.shape[0], hps * dh), lambda b, h: (0, h)),
                  pl.BlockSpec((1, dh), lambda b, h: (0, 0)),
                  pl.BlockSpec((chunk, chunk), lambda b, h: (0, 0))],
        out_specs=pl.BlockSpec((1, seq, hps * dh), lambda b, h: (b, 0, h)),
        out_shape=jax.ShapeDtypeStruct((bsz, seq, d), BF16),
        scratch_shapes=[pltpu.VMEM((seq, d), BF16),
                        pltpu.VMEM((seq, hps * 4 * dh), F32),
                        pltpu.VMEM((hps, chunk, dh), F32),
                        pltpu.VMEM((hps, dh, dh), F32),
                        pltpu.VMEM((chunk, chunk), jnp.int32),
                        pltpu.VMEM((3, chunk, dh), F32)],
        compiler_params=_params(("arbitrary", "arbitrary")),
        name="hgrn_mixer",
    )(x, norm_w_row, w_in_heads, lb_raw, o_norm_w_row, tri)


def _bproj_kernel(x_ref, nw_ref, w_ref, hw_ref, eseg_ref, esegt_ref, o_ref, h_ref, hn_ref, *, seq, dils):
    j = pl.program_id(1)
    n_qkv_tiles = 3 * len(dils)

    @pl.when(j == 0)
    def _():
        d = x_ref.shape[-1]
        n_planes = d // LANES
        rb = 256
        for i in range(seq // rb):
            hn = _rms_rows(x_ref[0, i * rb:(i + 1) * rb, :], nw_ref[...])
            for c in range(n_planes):
                hn_ref[c, i * rb:(i + 1) * rb, :] = hn[:, c * LANES:(c + 1) * LANES]
        for g, dil in enumerate(dils):
            sub = seq // dil
            for r in range(dil):
                for c0 in range(0, sub, rb):
                    rows = min(rb, sub - c0)
                    if dil == 1:
                        idx = pl.ds(c0, rows)
                    else:
                        idx = pl.ds(r + c0 * dil, rows, stride=dil)
                    for c in range(n_planes):
                        h_ref[g, r * sub + c0:r * sub + c0 + rows, c * LANES:(c + 1) * LANES] = (
                            hn_ref[c, idx, :].astype(BF16))

    grp = jnp.where(j < n_qkv_tiles, j // 3, 0)
    kind = jnp.where(j < n_qkv_tiles, j % 3, 3)
    pb = 512

    def tile(i):
        return _dot(h_ref[grp, i * pb:(i + 1) * pb, :], w_ref[...])

    @pl.when(kind < 2)
    def _():
        hw = hw_ref[0]
        for i in range(seq // pb):
            y = tile(i)
            ssq = _dot((y * y).astype(BF16), eseg_ref[...])
            r = lax.rsqrt(ssq * (1.0 / B_HEAD_DIM) + NORM_EPS)
            r_hi = r.astype(BF16)
            r_lo = (r - r_hi.astype(F32)).astype(BF16)
            rexp = _dot(r_hi, esegt_ref[...]) + _dot(r_lo, esegt_ref[...])
            o_ref[0, i * pb:(i + 1) * pb, :] = (y * rexp * hw).astype(BF16)

    @pl.when(kind == 2)
    def _():
        for i in range(seq // pb):
            o_ref[0, i * pb:(i + 1) * pb, :] = tile(i).astype(BF16)

    @pl.when(kind == 3)
    def _():
        for i in range(seq // pb):
            o_ref[0, i * pb:(i + 1) * pb, :] = _silu(tile(i)).astype(BF16)


def _attn_in_proj(x, norm_w_row, w_bf16, head_w):
    bsz, seq, d = x.shape
    n_out = w_bf16.shape[1]
    tn = d
    dils = tuple(dil for _, dil in DILATED_GROUPS)
    eseg = np.zeros((d, LANES), np.float32)
    eseg[np.arange(d), np.arange(d) // B_HEAD_DIM] = 1.0
    eseg = jnp.asarray(eseg, BF16)
    kern = functools.partial(_bproj_kernel, seq=seq, dils=dils)
    return pl.pallas_call(
        kern,
        grid=(bsz, n_out // tn),
        in_specs=[pl.BlockSpec((1, seq, d), lambda b, j: (b, 0, 0), pipeline_mode=pl.Buffered(1)),
                  pl.BlockSpec((1, d), lambda b, j: (0, 0)),
                  pl.BlockSpec((d, tn), lambda b, j: (0, j)),
                  pl.BlockSpec((1, 1, tn), lambda b, j: (j, 0, 0)),
                  pl.BlockSpec((d, LANES), lambda b, j: (0, 0)),
                  pl.BlockSpec((LANES, d), lambda b, j: (0, 0))],
        out_specs=pl.BlockSpec((1, seq, tn), lambda b, j: (b, 0, j)),
        out_shape=jax.ShapeDtypeStruct((bsz, seq, n_out), BF16),
        scratch_shapes=[pltpu.VMEM((len(dils), seq, d), BF16),
                        pltpu.VMEM((d // LANES, seq, LANES), F32)],
        compiler_params=_params(("arbitrary", "arbitrary")),
        name="attn_in_proj",
    )(x, norm_w_row, w_bf16, head_w, eseg, eseg.T)


def _attn_kernel(slopes_ref, q0, k0, v0, q1, k1, v1, q2, k2, v2, g_ref, o_ref,
                 num_ref, den_ref, max_ref, bias_ref, *, seq, dils, first_unroll, rest_unroll):
    hp = pl.program_id(1)
    blk = B_BLOCK
    hd = B_HEAD_DIM
    n_heads = (LANES // hd) * pl.num_programs(1)
    qkv = ((q0, k0, v0), (q1, k1, v1), (q2, k2, v2))
    head_a = lax.broadcasted_iota(jnp.int32, (blk, LANES), 1) < hd

    def windows(g, specs, has_prev):
        q_ref, k_ref, v_ref = qkv[g]
        dil = dils[g]
        nk = 2 * blk if has_prev else blk
        off = 0 if has_prev else blk
        head_a_k = lax.broadcasted_iota(jnp.int32, (nk, LANES), 1) < hd
        scores = []
        for row0, _ in specs:
            k_lo = row0 - blk if has_prev else row0
            q2d = q_ref[0, pl.ds(row0, blk), :]
            zero = jnp.zeros_like(q2d)
            qq = jnp.concatenate([jnp.where(head_a, q2d, zero), jnp.where(head_a, zero, q2d)], axis=0)
            scores.append(_dot_nt(qq, k_ref[0, pl.ds(k_lo, nk), :]))
        probs = []
        for s in scores:
            per_head = []
            for a in range(2):
                sa = s[a * blk:(a + 1) * blk, :] + bias_ref[a, :, off:off + nk]
                m = jnp.max(sa, axis=-1, keepdims=True)
                per_head.append((jnp.exp2(sa - m).astype(BF16), m))
            probs.append(per_head)
        for (row0, tok0), ((pa, ma), (pb, mb)) in zip(specs, probs):
            k_lo = row0 - blk if has_prev else row0
            v2d = v_ref[0, pl.ds(k_lo, nk), :]
            one = jnp.ones_like(v2d)
            oa = _dot(pa, jnp.where(head_a_k, v2d, one))
            ob = _dot(pb, jnp.where(head_a_k, one, v2d))
            idx = pl.ds(tok0, blk) if dil == 1 else pl.ds(tok0, blk, stride=dil)
            num_ref[g, idx, :] = jnp.where(head_a, oa, ob)
            den_ref[g, idx, :] = jnp.where(head_a, ob, oa)
            max_ref[g, idx, :] = jnp.where(head_a, ma, mb)

    qi = lax.broadcasted_iota(jnp.int32, (blk, 2 * blk), 0)
    kj = lax.broadcasted_iota(jnp.int32, (blk, 2 * blk), 1)
    dist = blk + qi - kj
    valid = (dist >= 0) & (dist <= blk)

    for g, dil in enumerate(dils):
        sub = seq // dil
        n_blk = sub // blk
        bias_base = jnp.where(valid, -(dist * dil).astype(F32), NEG_BIG)
        for a in range(2):
            slope = slopes_ref[g * n_heads + 2 * hp + a]
            bias_ref[a] = (slope * LOG2E) * bias_base

        u1 = min(first_unroll, dil)

        def first(i, carry, g=g, sub=sub, u1=u1):
            specs = []
            for w in range(u1):
                r = i * u1 + w
                specs.append((pl.multiple_of(r * sub, blk), r))
            windows(g, specs, False)
            return carry

        lax.fori_loop(0, dil // u1, first, 0)
        n_rest = dil * (n_blk - 1)
        if n_rest:
            u2 = rest_unroll

            def rest(i, carry, g=g, dil=dil, sub=sub, n_blk=n_blk, u2=u2):
                specs = []
                for w in range(u2):
                    idx = i * u2 + w
                    r = idx // (n_blk - 1)
                    b = idx % (n_blk - 1) + 1
                    specs.append((pl.multiple_of(r * sub + b * blk, blk), b * blk * dil + r))
                windows(g, specs, True)
                return carry

            lax.fori_loop(0, n_rest // u2, rest, 0)

    rb = 256
    n_g = len(dils)

    def merge(i, carry):
        r0 = pl.multiple_of(i * rb, rb)
        ms = [max_ref[g, pl.ds(r0, rb), :] for g in range(n_g)]
        mx = ms[0]
        for g in range(1, n_g):
            mx = jnp.maximum(mx, ms[g])
        num = jnp.zeros((rb, LANES), F32)
        den = jnp.zeros((rb, LANES), F32)
        for g in range(n_g):
            w = jnp.exp2(ms[g] - mx)
            num = num + w * num_ref[g, pl.ds(r0, rb), :]
            w_sw = pltpu.roll(w, hd, axis=1)
            den = den + w_sw * den_ref[g, pl.ds(r0, rb), :]
        den = pltpu.roll(den, hd, axis=1)
        gate = g_ref[0, pl.ds(r0, rb), :].astype(F32)
        o_ref[0, pl.ds(r0, rb), :] = (num / den * gate).astype(BF16)
        return carry

    lax.fori_loop(0, seq // rb, merge, 0)


def _attention(proj, slopes):
    bsz, seq, n_out = proj.shape
    d = n_out // (3 * B_N_GROUPS + 1)
    n_pairs = d // LANES
    dils = tuple(dil for _, dil in DILATED_GROUPS)
    first_unroll, rest_unroll = 2, 3
    for dil in dils:
        n_rest = dil * (seq // dil // B_BLOCK - 1)
        assert dil % min(first_unroll, dil) == 0 and n_rest % rest_unroll == 0

    def col_spec(tile):
        return pl.BlockSpec((1, seq, LANES), lambda b, h, s, tile=tile: (b, 0, tile * n_pairs + h))

    kern = functools.partial(_attn_kernel, seq=seq, dils=dils,
                             first_unroll=first_unroll, rest_unroll=rest_unroll)
    grid_spec = pltpu.PrefetchScalarGridSpec(
        num_scalar_prefetch=1,
        grid=(bsz, n_pairs),
        in_specs=[col_spec(t) for t in range(3 * B_N_GROUPS + 1)],
        out_specs=pl.BlockSpec((1, seq, LANES), lambda b, h, s: (b, 0, h)),
        scratch_shapes=[pltpu.VMEM((B_N_GROUPS, seq, LANES), F32),
                        pltpu.VMEM((B_N_GROUPS, seq, LANES), F32),
                        pltpu.VMEM((B_N_GROUPS, seq, LANES), F32),
                        pltpu.VMEM((2, B_BLOCK, 2 * B_BLOCK), F32)],
    )
    return pl.pallas_call(
        kern,
        grid_spec=grid_spec,
        out_shape=jax.ShapeDtypeStruct((bsz, seq, d), BF16),
        compiler_params=_params(("arbitrary", "arbitrary")),
        name="dilated_attention",
    )(slopes, *([proj] * (3 * B_N_GROUPS + 1)))


def _gmlp_kernel(x_ref, nw_ref, w_ref, vw_ref, vb_ref, ws_ref, bias_ref, y_ref,
                 h_ref, u_ref, v_ref, g_ref, *, tm):
    d = x_ref.shape[-1]
    cg = d // C_GROUPS
    rb = 256
    for i in range(tm // rb):
        h_ref[i * rb:(i + 1) * rb, :] = _rms_rows(x_ref[0, i * rb:(i + 1) * rb, :], nw_ref[...]).astype(BF16)
    for i in range(tm // rb):
        rows = slice(i * rb, (i + 1) * rb)
        hh = h_ref[rows, :]
        u_ref[rows, :] = _gelu_tanh(_dot(hh, w_ref[:, 0:d]))
        vv = _gelu_tanh(_dot(hh, w_ref[:, d:2 * d]))
        mu = jnp.mean(vv, axis=-1, keepdims=True)
        vc = vv - mu
        var = jnp.mean(vc * vc, axis=-1, keepdims=True)
        v_ref[rows, :] = (vc * lax.rsqrt(var + NORM_EPS) * vw_ref[...] + vb_ref[...]).astype(BF16)
        g_ref[rows, :] = _silu(_dot(hh, w_ref[:, 2 * d:3 * d]))
    ti = lax.broadcasted_iota(jnp.int32, (C_CHUNK, C_CHUNK), 0)
    si = lax.broadcasted_iota(jnp.int32, (C_CHUNK, C_CHUNK), 1)
    n_ch = tm // C_CHUNK
    for grp in range(C_GROUPS):
        cols = slice(grp * cg, (grp + 1) * cg)
        wc = jnp.where(si <= ti, ws_ref[grp], 0.0).astype(BF16)
        rhs = jnp.concatenate([v_ref[n * C_CHUNK:(n + 1) * C_CHUNK, cols] for n in range(n_ch)], axis=1)
        s = _dot(wc, rhs)
        bias = bias_ref[:, cols]
        for n in range(n_ch):
            rows = slice(n * C_CHUNK, (n + 1) * C_CHUNK)
            sn = s[:, n * cg:(n + 1) * cg] + bias
            y_ref[0, rows, cols] = (u_ref[rows, cols] * sn * g_ref[rows, cols]).astype(BF16)


def _gmlp_mixer(x, norm_w_row, w_bf16, vw_row, vb_row, w_s, bias_full, tm=512):
    bsz, seq, d = x.shape
    kern = functools.partial(_gmlp_kernel, tm=tm)
    return pl.pallas_call(
        kern,
        grid=(bsz, seq // tm),
        in_specs=[pl.BlockSpec((1, tm, d), lambda b, i: (b, i, 0)),
                  pl.BlockSpec((1, d), lambda b, i: (0, 0)),
                  pl.BlockSpec((d, 3 * d), lambda b, i: (0, 0)),
                  pl.BlockSpec((1, d), lambda b, i: (0, 0)),
                  pl.BlockSpec((1, d), lambda b, i: (0, 0)),
                  pl.BlockSpec(w_s.shape, lambda b, i: (0, 0, 0)),
                  pl.BlockSpec((C_CHUNK, d), lambda b, i: (0, 0))],
        out_specs=pl.BlockSpec((1, tm, d), lambda b, i: (b, i, 0)),
        out_shape=jax.ShapeDtypeStruct((bsz, seq, d), BF16),
        scratch_shapes=[pltpu.VMEM((tm, d), BF16),
                        pltpu.VMEM((tm, d), F32),
                        pltpu.VMEM((tm, d), BF16),
                        pltpu.VMEM((tm, d), F32)],
        compiler_params=_params(("arbitrary", "arbitrary")),
        name="gmlp_mixer",
    )(x, norm_w_row, w_bf16, vw_row, vb_row, w_s, bias_full)


def kernel(x, norm_w, a_w_in, a_lower_bounds, a_o_norm_w, a_w_out, b_w_in, b_q_norm_w, b_k_norm_w, b_w_out,
           c_w_in, c_v_norm_w, c_v_norm_b, c_w_s, c_b_s, c_w_out):
    bsz, seq, d = x.shape
    depth = norm_w.shape[0]
    n_mixers = 3
    assert seq == DILATED_GROUPS[-1][0] and seq % A_CHUNK == 0 and d % LANES == 0

    def residual(y, w_out, xin):
        out = _out_proj(y.reshape(bsz * seq, d), w_out.astype(BF16), xin.reshape(bsz * seq, d))
        return out.reshape(bsz, seq, d)

    for layer in range(depth):
        kind, idx = layer % n_mixers, layer // n_mixers
        nw = norm_w[layer][None, :]
        if kind == 0:
            n_heads = d // A_HEAD_DIM
            w = a_w_in[idx].reshape(d, 4, n_heads, A_HEAD_DIM).transpose(0, 2, 1, 3)
            w = w.reshape(d, 4 * d).astype(BF16)
            y = _hgrn_mixer(x, nw, w, a_lower_bounds, a_o_norm_w[idx][None, :], idx)
            x = residual(y, a_w_out[idx], x)
        elif kind == 1:
            n_heads = d // B_HEAD_DIM
            n_total = B_N_GROUPS * n_heads
            slopes = jnp.exp2(-ALIBI_MAX_EXP * jnp.arange(1, n_total + 1, dtype=F32) / n_total)
            rows = []
            for g in range(B_N_GROUPS):
                rows.append(jnp.tile(b_q_norm_w[idx, g], n_heads) * (B_HEAD_DIM ** -0.5 * LOG2E))
                rows.append(jnp.tile(b_k_norm_w[idx, g], n_heads))
                rows.append(jnp.ones((d,), F32))
            rows.append(jnp.ones((d,), F32))
            head_w = jnp.stack(rows)[:, None, :]
            proj = _attn_in_proj(x, nw, b_w_in[idx].astype(BF16), head_w)
            y = _attention(proj, slopes)
            x = residual(y, b_w_out[idx], x)
        else:
            bias_full = jnp.repeat(c_b_s[idx].T, d // C_GROUPS, axis=1)
            y = _gmlp_mixer(x, nw, c_w_in[idx].astype(BF16), c_v_norm_w[idx][None, :],
                            c_v_norm_b[idx][None, :], c_w_s[idx], bias_full)
            x = residual(y, c_w_out[idx], x)
    return x
```

```python
import functools

import numpy as np
import jax
import jax.numpy as jnp
from jax import lax
from jax.experimental import pallas as pl
from jax.experimental.pallas import tpu as pltpu

F32 = jnp.float32
BF16 = jnp.bfloat16

NORM_EPS = 1e-6
NEG_BIG = -1e30
LB_FLOOR = 1e-30
LOG2E = float(np.log2(np.e))

LANES = 128
SUBLANES = 8
VMEM_LIMIT_BYTES = 56 * 1024 * 1024

A_HEAD_DIM = 128
A_CHUNK = 128
A_CHUNKS_PER_ITER = 2
A_HEADS_PER_STEP = 4
B_HEAD_DIM = 64
DILATED_GROUPS = ((128, 1), (512, 4), (2048, 16))
B_N_GROUPS = len(DILATED_GROUPS)
B_BLOCK = 128
ALIBI_MAX_EXP = 8.0
C_CHUNK = 128
C_GROUPS = 8


def _silu(x):
    hx = 0.5 * x
    return hx + hx * jnp.tanh(hx)


def _gelu_tanh(x):
    c = np.float32(np.sqrt(2.0 / np.pi))
    return 0.5 * x * (1.0 + jnp.tanh(c * (x + 0.044715 * (x * x * x))))


def _rms_rows(xf, w):
    ms = jnp.mean(xf * xf, axis=-1, keepdims=True)
    return xf * lax.rsqrt(ms + NORM_EPS) * w


def _dot(a, b):
    return jnp.dot(a, b, preferred_element_type=F32)


def _dot_nt(a, b):
    return lax.dot_general(a, b, (((1,), (1,)), ((), ())), preferred_element_type=F32)


def _dot_tn(a, b):
    return lax.dot_general(a, b, (((0,), (0,)), ((), ())), preferred_element_type=F32)


def _params(sem):
    return pltpu.CompilerParams(dimension_semantics=sem, vmem_limit_bytes=VMEM_LIMIT_BYTES)


def _out_proj_kernel(y_ref, w_ref, x_ref, o_ref):
    o_ref[...] = x_ref[...] + _dot(y_ref[...], w_ref[...])


def _out_proj(y2d, w_bf16, x2d, tm=1024):
    m, k = y2d.shape
    n = w_bf16.shape[1]
    return pl.pallas_call(
        _out_proj_kernel,
        grid=(m // tm,),
        in_specs=[pl.BlockSpec((tm, k), lambda i: (i, 0)),
                  pl.BlockSpec((k, n), lambda i: (0, 0)),
                  pl.BlockSpec((tm, n), lambda i: (i, 0))],
        out_specs=pl.BlockSpec((tm, n), lambda i: (i, 0)),
        out_shape=jax.ShapeDtypeStruct((m, n), F32),
        compiler_params=_params(("arbitrary",)),
        name="out_proj",
    )(y2d, w_bf16, x2d)


def _hgrn_levels(chunk):
    return [1 << j for j in range(int(np.log2(chunk)))]


def _hgrn_level_exponents(b, log_f, b_view, sgn_ref, chunk):
    out = {}
    n_vregs = chunk // SUBLANES
    width = b.shape[-1]

    def row(r, n):
        return jnp.broadcast_to(b_view[pl.ds(r, 1), :], (n, width))

    for m in _hgrn_levels(chunk):
        if m >= SUBLANES:
            parts = []
            for j in range(chunk // (2 * m)):
                base = j * 2 * m
                ref_row = row(base + m - 1, m)
                parts.append(ref_row - b[base:base + m, :])
                parts.append(b[base + m:base + 2 * m, :] - ref_row)
            out[m] = jnp.concatenate(parts, axis=0)
        elif m == 4:
            ref = jnp.concatenate([row(SUBLANES * i + 3, SUBLANES) for i in range(n_vregs)], axis=0)
            out[m] = (b - ref) * sgn_ref[0]
        elif m == 2:
            lo = jnp.concatenate([row(SUBLANES * i + 1, SUBLANES) for i in range(n_vregs)], axis=0)
            hi = jnp.concatenate([row(SUBLANES * i + 5, SUBLANES) for i in range(n_vregs)], axis=0)
            ref = jnp.where(sgn_ref[0] > 0.0, hi, lo)
            out[m] = (b - ref) * sgn_ref[1]
        else:
            out[m] = log_f * sgn_ref[2]
    return out


def _hgrn_kernel(x_ref, nw_ref, w_ref, lbraw_ref, onw_ref, tri_ref, y_ref,
                 h_ref, act_ref, b_ref, st_ref, lvl_ref, sgn_ref, qd_ref, kd_ref, a16_ref, dec_ref,
                 *, layer_idx, seq, chunk, hps, cpi):
    step = pl.program_id(1)
    dh = A_HEAD_DIM
    rb = 256

    @pl.when(step == 0)
    def _():
        for i in range(seq // rb):
            xs = x_ref[0, i * rb:(i + 1) * rb, :]
            h_ref[i * rb:(i + 1) * rb, :] = _rms_rows(xs, nw_ref[...]).astype(BF16)

    raw = lbraw_ref[...]
    n_layers = raw.shape[0]
    mx = raw[0:1, :]
    for i in range(1, n_layers):
        mx = jnp.maximum(mx, raw[i:i + 1, :])
    es = [jnp.exp(raw[i:i + 1, :] - mx) for i in range(n_layers)]
    z = es[0]
    for i in range(1, n_layers):
        z = z + es[i]
    soft = [e / z for e in es]
    csum = soft[0]
    for i in range(1, layer_idx + 1):
        csum = csum + soft[i]
    lb = csum - soft[0]
    log_lb = jnp.log(jnp.maximum(lb, LB_FLOOR))
    log1m_lb = jnp.log1p(-lb)
    one_m_lb = 1.0 - lb
    heads = range(hps)

    pb = 512
    for i in range(seq // pb):
        rows = slice(i * pb, (i + 1) * pb)
        res = _dot(h_ref[rows, :], w_ref[...])
        for a in heads:
            c0 = a * 4 * dh
            o0 = a * 5 * dh
            lbs = slice(a * dh, (a + 1) * dh)
            pf = res[:, c0 + dh:c0 + 2 * dh]
            log_sig = jnp.minimum(pf, 0.0) - jnp.log(1.0 + jnp.exp(-jnp.abs(pf)))
            bb = log1m_lb[:, lbs] + log_sig
            ll = log_lb[:, lbs]
            log_f = jnp.maximum(ll, bb) + jnp.log(1.0 + jnp.exp(-jnp.abs(ll - bb)))
            act_ref[rows, o0:o0 + dh] = _silu(res[:, c0:c0 + dh])
            act_ref[rows, o0 + dh:o0 + 2 * dh] = one_m_lb[:, lbs] * jnp.exp(log_sig - pf)
            act_ref[rows, o0 + 2 * dh:o0 + 3 * dh] = res[:, c0 + 2 * dh:c0 + 3 * dh]
            act_ref[rows, o0 + 3 * dh:o0 + 4 * dh] = _silu(res[:, c0 + 3 * dh:c0 + 4 * dh])
            act_ref[rows, o0 + 4 * dh:o0 + 5 * dh] = log_f * LOG2E

    ti = lax.broadcasted_iota(jnp.int32, (chunk, chunk), 0)
    si = lax.broadcasted_iota(jnp.int32, (chunk, chunk), 1)
    lvl_ref[...] = jnp.where(si < ti, ti ^ si, jnp.where(si == ti, 0, -1))
    tr = lax.broadcasted_iota(jnp.int32, (chunk, dh), 0)
    sgn_ref[0] = jnp.where((tr & 4) != 0, 1.0, -1.0)
    sgn_ref[1] = jnp.where((tr & 2) != 0, 1.0, -1.0)
    sgn_ref[2] = jnp.where((tr & 1) != 0, 1.0, 0.0)
    st_ref[...] = jnp.zeros_like(st_ref)
    onw = onw_ref[...]
    levels = _hgrn_levels(chunk)

    def row_start(c):
        return c * chunk if isinstance(c, int) else pl.multiple_of(c * chunk, chunk)

    def front(chunks):
        q, k, log_f = [], [], []
        for c in chunks:
            for a in heads:
                o0 = a * 5 * dh
                q.append(act_ref[pl.ds(row_start(c), chunk), o0:o0 + dh])
                k.append(act_ref[pl.ds(row_start(c), chunk), o0 + dh:o0 + 2 * dh])
                log_f.append(act_ref[pl.ds(row_start(c), chunk), o0 + 4 * dh:o0 + 5 * dh])
        slots = range(len(q))

        b = []
        for a in slots:
            lf_hi = log_f[a].astype(BF16)
            lf_lo = (log_f[a] - lf_hi.astype(F32)).astype(BF16)
            b.append(_dot(tri_ref[...], lf_hi) + _dot(tri_ref[...], lf_lo))
            b_ref[a] = b[a]

        expo, acc = [], []
        for a in slots:
            expo.append(_hgrn_level_exponents(b[a], log_f[a], b_ref.at[a], sgn_ref, chunk))
            qd_ref[a] = (q[a] * jnp.exp2(b[a])).astype(BF16)
            b_last = b_ref[a, chunk - 1:chunk, :]
            kd_ref[a] = (k[a] * jnp.exp2(b_last - b[a])).astype(BF16)
            dec_ref[a] = jnp.exp2(b_last)
            acc.append(jnp.where(lvl_ref[...] == 0, _dot_nt(q[a].astype(BF16), k[a].astype(BF16)), 0.0))
        for m in levels:
            if m >= SUBLANES:
                break
            for a in slots:
                p = jnp.exp2(expo[a][m])
                a_m = _dot_nt((q[a] * p).astype(BF16), (k[a] * p).astype(BF16))
                acc[a] = jnp.where(lvl_ref[...] >= m, a_m, acc[a])
        n_grp = chunk // SUBLANES
        acc = [[acc_a[SUBLANES * i:SUBLANES * (i + 1), :] for i in range(n_grp)] for acc_a in acc]
        for m in levels:
            if m < SUBLANES:
                continue
            is_query = [(SUBLANES * i) % (2 * m) >= m for i in range(n_grp)]
            for a in slots:
                p = jnp.exp2(expo[a][m])
                qp, kp = q[a] * p, k[a] * p
                q_m = jnp.concatenate([qp[SUBLANES * i:SUBLANES * (i + 1), :]
                                       for i in range(n_grp) if is_query[i]], axis=0)
                k_m = jnp.concatenate([k[a][SUBLANES * i:SUBLANES * (i + 1), :] if is_query[i]
                                       else kp[SUBLANES * i:SUBLANES * (i + 1), :]
                                       for i in range(n_grp)], axis=0)
                a_m = _dot_nt(q_m.astype(BF16), k_m.astype(BF16))
                for n, i in enumerate([i for i in range(n_grp) if is_query[i]]):
                    rows = slice(SUBLANES * i, SUBLANES * (i + 1))
                    acc[a][i] = jnp.where(lvl_ref[rows, :] >= m,
                                          a_m[SUBLANES * n:SUBLANES * (n + 1), :], acc[a][i])
        for a in slots:
            a16_ref[a] = jnp.concatenate(acc[a], axis=0).astype(BF16)

    def back(chunks):
        for n, c in enumerate(chunks):
            r0 = row_start(c)
            for a in heads:
                o0 = a * 5 * dh
                slot = n * hps + a
                v16 = act_ref[pl.ds(r0, chunk), o0 + 2 * dh:o0 + 3 * dh].astype(BF16)
                st = st_ref[a]
                oa = _dot_nt(qd_ref[slot], st.astype(BF16)) + _dot(a16_ref[slot], v16)
                st_ref[a] = st * dec_ref[slot] + _dot_tn(v16, kd_ref[slot])
                y = _rms_rows(oa, onw) * act_ref[pl.ds(r0, chunk), o0 + 3 * dh:o0 + 4 * dh]
                y_ref[0, pl.ds(r0, chunk), a * dh:(a + 1) * dh] = y.astype(BF16)

    n_iter = seq // (chunk * cpi)
    front([n for n in range(cpi)])

    def body(i, carry):
        back([(i - 1) * cpi + n for n in range(cpi)])
        front([i * cpi + n for n in range(cpi)])
        return carry

    lax.fori_loop(1, n_iter, body, 0)
    back([(n_iter - 1) * cpi + n for n in range(cpi)])


def _hgrn_mixer(x, norm_w_row, w_in_heads, lb_raw, o_norm_w_row, layer_idx):
    bsz, seq, d = x.shape
    dh = A_HEAD_DIM
    hps = A_HEADS_PER_STEP
    n_steps = d // (dh * hps)
    chunk = A_CHUNK
    tri = jnp.asarray(np.tril(np.ones((chunk, chunk), np.float32)), BF16)
    cpi = A_CHUNKS_PER_ITER
    n_slots = hps * cpi
    assert seq % (chunk * cpi) == 0
    kern = functools.partial(_hgrn_kernel, layer_idx=layer_idx, seq=seq, chunk=chunk, hps=hps, cpi=cpi)
    return pl.pallas_call(
        kern,
        grid=(bsz, n_steps),
        in_specs=[pl.BlockSpec((1, seq, d), lambda b, h: (b, 0, 0), pipeline_mode=pl.Buffered(1)),
                  pl.BlockSpec((1, d), lambda b, h: (0, 0)),
                  pl.BlockSpec((d, hps * 4 * dh), lambda b, h: (0, h)),
                  pl.BlockSpec((lb_raw.shape[0], hps * dh), lambda b, h: (0, h)),
                  pl.BlockSpec((1, dh), lambda b, h: (0, 0)),
                  pl.BlockSpec((chunk, chunk), lambda b, h: (0, 0))],
        out_specs=pl.BlockSpec((1, seq, hps * dh), lambda b, h: (b, 0, h)),
        out_shape=jax.ShapeDtypeStruct((bsz, seq, d), BF16),
        scratch_shapes=[pltpu.VMEM((seq, d), BF16),
                        pltpu.VMEM((seq, hps * 5 * dh), F32),
                        pltpu.VMEM((n_slots, chunk, dh), F32),
                        pltpu.VMEM((hps, dh, dh), F32),
                        pltpu.VMEM((chunk, chunk), jnp.int32),
                        pltpu.VMEM((3, chunk, dh), F32),
                        pltpu.VMEM((n_slots, chunk, dh), BF16),
                        pltpu.VMEM((n_slots, chunk, dh), BF16),
                        pltpu.VMEM((n_slots, chunk, chunk), BF16),
                        pltpu.VMEM((n_slots, 1, dh), F32)],
        compiler_params=_params(("arbitrary", "arbitrary")),
        name="hgrn_mixer",
    )(x, norm_w_row, w_in_heads, lb_raw, o_norm_w_row, tri)


def _bproj_kernel(x_ref, nw_ref, w_ref, hw_ref, eseg_ref, esegt_ref, o_ref, h_ref, hn_ref, *, seq, dils):
    j = pl.program_id(1)
    n_qkv_tiles = 3 * len(dils)

    @pl.when(j == 0)
    def _():
        d = x_ref.shape[-1]
        n_planes = d // LANES
        rb = 256
        for i in range(seq // rb):
            hn = _rms_rows(x_ref[0, i * rb:(i + 1) * rb, :], nw_ref[...])
            for c in range(n_planes):
                hn_ref[c, i * rb:(i + 1) * rb, :] = hn[:, c * LANES:(c + 1) * LANES]
        for g, dil in enumerate(dils):
            sub = seq // dil
            for r in range(dil):
                for c0 in range(0, sub, rb):
                    rows = min(rb, sub - c0)
                    if dil == 1:
                        idx = pl.ds(c0, rows)
                    else:
                        idx = pl.ds(r + c0 * dil, rows, stride=dil)
                    for c in range(n_planes):
                        h_ref[g, r * sub + c0:r * sub + c0 + rows, c * LANES:(c + 1) * LANES] = (
                            hn_ref[c, idx, :].astype(BF16))

    grp = jnp.where(j < n_qkv_tiles, j // 3, 0)
    kind = jnp.where(j < n_qkv_tiles, j % 3, 3)
    pb = 512

    def tile(i):
        return _dot(h_ref[grp, i * pb:(i + 1) * pb, :], w_ref[...])

    @pl.when(kind < 2)
    def _():
        hw = hw_ref[0]
        for i in range(seq // pb):
            y = tile(i)
            ssq = _dot((y * y).astype(BF16), eseg_ref[...])
            r = lax.rsqrt(ssq * (1.0 / B_HEAD_DIM) + NORM_EPS)
            r_hi = r.astype(BF16)
            r_lo = (r - r_hi.astype(F32)).astype(BF16)
            rexp = _dot(r_hi, esegt_ref[...]) + _dot(r_lo, esegt_ref[...])
            o_ref[0, i * pb:(i + 1) * pb, :] = (y * rexp * hw).astype(BF16)

    @pl.when(kind == 2)
    def _():
        for i in range(seq // pb):
            o_ref[0, i * pb:(i + 1) * pb, :] = tile(i).astype(BF16)

    @pl.when(kind == 3)
    def _():
        for i in range(seq // pb):
            o_ref[0, i * pb:(i + 1) * pb, :] = _silu(tile(i)).astype(BF16)


def _attn_in_proj(x, norm_w_row, w_bf16, head_w):
    bsz, seq, d = x.shape
    n_out = w_bf16.shape[1]
    tn = d
    dils = tuple(dil for _, dil in DILATED_GROUPS)
    eseg = np.zeros((d, LANES), np.float32)
    eseg[np.arange(d), np.arange(d) // B_HEAD_DIM] = 1.0
    eseg = jnp.asarray(eseg, BF16)
    kern = functools.partial(_bproj_kernel, seq=seq, dils=dils)
    return pl.pallas_call(
        kern,
        grid=(bsz, n_out // tn),
        in_specs=[pl.BlockSpec((1, seq, d), lambda b, j: (b, 0, 0), pipeline_mode=pl.Buffered(1)),
                  pl.BlockSpec((1, d), lambda b, j: (0, 0)),
                  pl.BlockSpec((d, tn), lambda b, j: (0, j)),
                  pl.BlockSpec((1, 1, tn), lambda b, j: (j, 0, 0)),
                  pl.BlockSpec((d, LANES), lambda b, j: (0, 0)),
                  pl.BlockSpec((LANES, d), lambda b, j: (0, 0))],
        out_specs=pl.BlockSpec((1, seq, tn), lambda b, j: (b, 0, j)),
        out_shape=jax.ShapeDtypeStruct((bsz, seq, n_out), BF16),
        scratch_shapes=[pltpu.VMEM((len(dils), seq, d), BF16),
                        pltpu.VMEM((d // LANES, seq, LANES), F32)],
        compiler_params=_params(("arbitrary", "arbitrary")),
        name="attn_in_proj",
    )(x, norm_w_row, w_bf16, head_w, eseg, eseg.T)


def _attn_kernel(slopes_ref, q0, k0, v0, q1, k1, v1, q2, k2, v2, g_ref, o_ref,
                 num_ref, den_ref, max_ref, maxsw_ref, bias_ref, *, seq, dils, unrolls):
    hp = pl.program_id(1)
    blk = B_BLOCK
    hd = B_HEAD_DIM
    n_heads = (LANES // hd) * pl.num_programs(1)
    qkv = ((q0, k0, v0), (q1, k1, v1), (q2, k2, v2))
    head_a = lax.broadcasted_iota(jnp.int32, (blk, LANES), 1) < hd

    def windows(g, specs, has_prev):
        q_ref, k_ref, v_ref = qkv[g]
        dil = dils[g]
        nk = 2 * blk if has_prev else blk
        off = 0 if has_prev else blk
        head_a_k = lax.broadcasted_iota(jnp.int32, (nk, LANES), 1) < hd
        scores = []
        for row0, _ in specs:
            k_lo = row0 - blk if has_prev else row0
            q2d = q_ref[0, pl.ds(row0, blk), :]
            zero = jnp.zeros_like(q2d)
            qq = jnp.concatenate([jnp.where(head_a, q2d, zero), jnp.where(head_a, zero, q2d)], axis=0)
            scores.append(_dot_nt(qq, k_ref[0, pl.ds(k_lo, nk), :]))
        probs = []
        for s in scores:
            per_head = []
            for a in range(2):
                sa = s[a * blk:(a + 1) * blk, :] + bias_ref[a, :, off:off + nk]
                m = jnp.max(sa, axis=-1, keepdims=True)
                per_head.append((jnp.exp2(sa - m).astype(BF16), m))
            probs.append(per_head)
        for (row0, tok0), ((pa, ma), (pb, mb)) in zip(specs, probs):
            k_lo = row0 - blk if has_prev else row0
            v2d = v_ref[0, pl.ds(k_lo, nk), :]
            one = jnp.ones_like(v2d)
            oa = _dot(pa, jnp.where(head_a_k, v2d, one))
            ob = _dot(pb, jnp.where(head_a_k, one, v2d))
            idx = pl.ds(tok0, blk) if dil == 1 else pl.ds(tok0, blk, stride=dil)
            num_ref[g, idx, :] = jnp.where(head_a, oa, ob)
            den_ref[g, idx, :] = jnp.where(head_a, ob, oa)
            max_ref[g, idx, :] = jnp.where(head_a, ma, mb)
            maxsw_ref[g, idx, :] = jnp.where(head_a, mb, ma)

    qi = lax.broadcasted_iota(jnp.int32, (blk, 2 * blk), 0)
    kj = lax.broadcasted_iota(jnp.int32, (blk, 2 * blk), 1)
    dist = blk + qi - kj
    valid = (dist >= 0) & (dist <= blk)

    for g, dil in enumerate(dils):
        sub = seq // dil
        n_blk = sub // blk
        bias_base = jnp.where(valid, -(dist * dil).astype(F32), NEG_BIG)
        for a in range(2):
            slope = slopes_ref[g * n_heads + 2 * hp + a]
            bias_ref[a] = (slope * LOG2E) * bias_base

        u1, u2 = unrolls[g]

        def first(i, carry, g=g, sub=sub, u1=u1):
            specs = []
            for w in range(u1):
                r = i * u1 + w
                specs.append((pl.multiple_of(r * sub, blk), r))
            windows(g, specs, False)
            return carry

        lax.fori_loop(0, dil // u1, first, 0)
        n_rest = dil * (n_blk - 1)
        if n_rest:
            def rest(i, carry, g=g, dil=dil, sub=sub, n_blk=n_blk, u2=u2):
                specs = []
                for w in range(u2):
                    idx = i * u2 + w
                    r = idx // (n_blk - 1)
                    b = idx % (n_blk - 1) + 1
                    specs.append((pl.multiple_of(r * sub + b * blk, blk), b * blk * dil + r))
                windows(g, specs, True)
                return carry

            lax.fori_loop(0, n_rest // u2, rest, 0)

    rb = 256
    n_g = len(dils)

    def merge(i, carry):
        r0 = pl.multiple_of(i * rb, rb)
        ms = [max_ref[g, pl.ds(r0, rb), :] for g in range(n_g)]
        mx = ms[0]
        for g in range(1, n_g):
            mx = jnp.maximum(mx, ms[g])
        ms_sw = [maxsw_ref[g, pl.ds(r0, rb), :] for g in range(n_g)]
        mx_sw = ms_sw[0]
        for g in range(1, n_g):
            mx_sw = jnp.maximum(mx_sw, ms_sw[g])
        num = jnp.zeros((rb, LANES), F32)
        den = jnp.zeros((rb, LANES), F32)
        for g in range(n_g):
            num = num + jnp.exp2(ms[g] - mx) * num_ref[g, pl.ds(r0, rb), :]
            den = den + jnp.exp2(ms_sw[g] - mx_sw) * den_ref[g, pl.ds(r0, rb), :]
        den = pltpu.roll(den, hd, axis=1)
        gate = g_ref[0, pl.ds(r0, rb), :].astype(F32)
        o_ref[0, pl.ds(r0, rb), :] = (num / den * gate).astype(BF16)
        return carry

    lax.fori_loop(0, seq // rb, merge, 0)


def _attention(proj, slopes):
    bsz, seq, n_out = proj.shape
    d = n_out // (3 * B_N_GROUPS + 1)
    n_pairs = d // LANES
    dils = tuple(dil for _, dil in DILATED_GROUPS)
    unrolls = ((1, 5), (4, 6), (4, 1))
    for (u1, u2), dil in zip(unrolls, dils):
        n_rest = dil * (seq // dil // B_BLOCK - 1)
        assert dil % u1 == 0 and n_rest % u2 == 0

    def col_spec(tile):
        return pl.BlockSpec((1, seq, LANES), lambda b, h, s, tile=tile: (b, 0, tile * n_pairs + h))

    kern = functools.partial(_attn_kernel, seq=seq, dils=dils, unrolls=unrolls)
    grid_spec = pltpu.PrefetchScalarGridSpec(
        num_scalar_prefetch=1,
        grid=(bsz, n_pairs),
        in_specs=[col_spec(t) for t in range(3 * B_N_GROUPS + 1)],
        out_specs=pl.BlockSpec((1, seq, LANES), lambda b, h, s: (b, 0, h)),
        scratch_shapes=[pltpu.VMEM((B_N_GROUPS, seq, LANES), F32)] * 4
        + [pltpu.VMEM((2, B_BLOCK, 2 * B_BLOCK), F32)],
    )
    return pl.pallas_call(
        kern,
        grid_spec=grid_spec,
        out_shape=jax.ShapeDtypeStruct((bsz, seq, d), BF16),
        compiler_params=_params(("arbitrary", "arbitrary")),
        name="dilated_attention",
    )(slopes, *([proj] * (3 * B_N_GROUPS + 1)))


def _gmlp_kernel(x_ref, nw_ref, w_ref, vw_ref, vb_ref, ws_ref, bias_ref, y_ref,
                 h_ref, u_ref, v_ref, g_ref, *, tm):
    d = x_ref.shape[-1]
    cg = d // C_GROUPS
    rb = 256
    for i in range(tm // rb):
        h_ref[i * rb:(i + 1) * rb, :] = _rms_rows(x_ref[0, i * rb:(i + 1) * rb, :], nw_ref[...]).astype(BF16)
    for i in range(tm // rb):
        rows = slice(i * rb, (i + 1) * rb)
        hh = h_ref[rows, :]
        u_ref[rows, :] = _gelu_tanh(_dot(hh, w_ref[:, 0:d]))
        vv = _gelu_tanh(_dot(hh, w_ref[:, d:2 * d]))
        mu = jnp.mean(vv, axis=-1, keepdims=True)
        vc = vv - mu
        var = jnp.mean(vc * vc, axis=-1, keepdims=True)
        v_ref[rows, :] = (vc * lax.rsqrt(var + NORM_EPS) * vw_ref[...] + vb_ref[...]).astype(BF16)
        g_ref[rows, :] = _silu(_dot(hh, w_ref[:, 2 * d:3 * d]))
    ti = lax.broadcasted_iota(jnp.int32, (C_CHUNK, C_CHUNK), 0)
    si = lax.broadcasted_iota(jnp.int32, (C_CHUNK, C_CHUNK), 1)
    n_ch = tm // C_CHUNK
    for grp in range(C_GROUPS):
        cols = slice(grp * cg, (grp + 1) * cg)
        wc = jnp.where(si <= ti, ws_ref[grp], 0.0).astype(BF16)
        rhs = jnp.concatenate([v_ref[n * C_CHUNK:(n + 1) * C_CHUNK, cols] for n in range(n_ch)], axis=1)
        s = _dot(wc, rhs)
        bias = bias_ref[:, cols]
        for n in range(n_ch):
            rows = slice(n * C_CHUNK, (n + 1) * C_CHUNK)
            sn = s[:, n * cg:(n + 1) * cg] + bias
            y_ref[0, rows, cols] = (u_ref[rows, cols] * sn * g_ref[rows, cols]).astype(BF16)


def _gmlp_mixer(x, norm_w_row, w_bf16, vw_row, vb_row, w_s, bias_full, tm=512):
    bsz, seq, d = x.shape
    kern = functools.partial(_gmlp_kernel, tm=tm)
    return pl.pallas_call(
        kern,
        grid=(bsz, seq // tm),
        in_specs=[pl.BlockSpec((1, tm, d), lambda b, i: (b, i, 0)),
                  pl.BlockSpec((1, d), lambda b, i: (0, 0)),
                  pl.BlockSpec((d, 3 * d), lambda b, i: (0, 0)),
                  pl.BlockSpec((1, d), lambda b, i: (0, 0)),
                  pl.BlockSpec((1, d), lambda b, i: (0, 0)),
                  pl.BlockSpec(w_s.shape, lambda b, i: (0, 0, 0)),
                  pl.BlockSpec((C_CHUNK, d), lambda b, i: (0, 0))],
        out_specs=pl.BlockSpec((1, tm, d), lambda b, i: (b, i, 0)),
        out_shape=jax.ShapeDtypeStruct((bsz, seq, d), BF16),
        scratch_shapes=[pltpu.VMEM((tm, d), BF16),
                        pltpu.VMEM((tm, d), F32),
                        pltpu.VMEM((tm, d), BF16),
                        pltpu.VMEM((tm, d), F32)],
        compiler_params=_params(("arbitrary", "arbitrary")),
        name="gmlp_mixer",
    )(x, norm_w_row, w_bf16, vw_row, vb_row, w_s, bias_full)


def kernel(x, norm_w, a_w_in, a_lower_bounds, a_o_norm_w, a_w_out, b_w_in, b_q_norm_w, b_k_norm_w, b_w_out,
           c_w_in, c_v_norm_w, c_v_norm_b, c_w_s, c_b_s, c_w_out):
    bsz, seq, d = x.shape
    depth = norm_w.shape[0]
    n_mixers = 3
    assert seq == DILATED_GROUPS[-1][0] and seq % A_CHUNK == 0 and d % LANES == 0

    def residual(y, w_out, xin):
        out = _out_proj(y.reshape(bsz * seq, d), w_out.astype(BF16), xin.reshape(bsz * seq, d))
        return out.reshape(bsz, seq, d)

    for layer in range(depth):
        kind, idx = layer % n_mixers, layer // n_mixers
        nw = norm_w[layer][None, :]
        if kind == 0:
            n_heads = d // A_HEAD_DIM
            w = a_w_in[idx].reshape(d, 4, n_heads, A_HEAD_DIM).transpose(0, 2, 1, 3)
            w = w.reshape(d, 4 * d).astype(BF16)
            y = _hgrn_mixer(x, nw, w, a_lower_bounds, a_o_norm_w[idx][None, :], idx)
            x = residual(y, a_w_out[idx], x)
        elif kind == 1:
            n_heads = d // B_HEAD_DIM
            n_total = B_N_GROUPS * n_heads
            slopes = jnp.exp2(-ALIBI_MAX_EXP * jnp.arange(1, n_total + 1, dtype=F32) / n_total)
            rows = []
            for g in range(B_N_GROUPS):
                rows.append(jnp.tile(b_q_norm_w[idx, g], n_heads) * (B_HEAD_DIM ** -0.5 * LOG2E))
                rows.append(jnp.tile(b_k_norm_w[idx, g], n_heads))
                rows.append(jnp.ones((d,), F32))
            rows.append(jnp.ones((d,), F32))
            head_w = jnp.stack(rows)[:, None, :]
            proj = _attn_in_proj(x, nw, b_w_in[idx].astype(BF16), head_w)
            y = _attention(proj, slopes)
            x = residual(y, b_w_out[idx], x)
        else:
            bias_full = jnp.repeat(c_b_s[idx].T, d // C_GROUPS, axis=1)
            y = _gmlp_mixer(x, nw, c_w_in[idx].astype(BF16), c_v_norm_w[idx][None, :],
                            c_v_norm_b[idx][None, :], c_w_s[idx], bias_full)
            x = residual(y, c_w_out[idx], x)
    return x
```

```python
import functools

import numpy as np
import jax
import jax.numpy as jnp
from jax import lax
from jax.experimental import pallas as pl
from jax.experimental.pallas import tpu as pltpu

F32 = jnp.float32
BF16 = jnp.bfloat16

NORM_EPS = 1e-6
NEG_BIG = -1e30
LB_FLOOR = 1e-30
LOG2E = float(np.log2(np.e))

LANES = 128
SUBLANES = 8
VMEM_LIMIT_BYTES = 56 * 1024 * 1024

A_HEAD_DIM = 128
A_CHUNK = 128
A_CHUNKS_PER_ITER = 2
A_HEADS_PER_STEP = 4
B_HEAD_DIM = 64
DILATED_GROUPS = ((128, 1), (512, 4), (2048, 16))
B_N_GROUPS = len(DILATED_GROUPS)
B_BLOCK = 128
ALIBI_MAX_EXP = 8.0
C_CHUNK = 128
C_GROUPS = 8


def _silu(x):
    hx = 0.5 * x
    return hx + hx * jnp.tanh(hx)


def _gelu_tanh(x):
    c = np.float32(np.sqrt(2.0 / np.pi))
    return 0.5 * x * (1.0 + jnp.tanh(c * (x + 0.044715 * (x * x * x))))


def _rms_rows(xf, w):
    ms = jnp.mean(xf * xf, axis=-1, keepdims=True)
    return xf * lax.rsqrt(ms + NORM_EPS) * w


def _dot(a, b):
    return jnp.dot(a, b, preferred_element_type=F32)


def _dot_nt(a, b):
    return lax.dot_general(a, b, (((1,), (1,)), ((), ())), preferred_element_type=F32)


def _dot_tn(a, b):
    return lax.dot_general(a, b, (((0,), (0,)), ((), ())), preferred_element_type=F32)


def _params(sem):
    return pltpu.CompilerParams(dimension_semantics=sem, vmem_limit_bytes=VMEM_LIMIT_BYTES)


def _out_proj_kernel(y_ref, w_ref, x_ref, o_ref):
    o_ref[...] = x_ref[...] + _dot(y_ref[...], w_ref[...])


def _out_proj(y2d, w_bf16, x2d, tm=1024):
    m, k = y2d.shape
    n = w_bf16.shape[1]
    return pl.pallas_call(
        _out_proj_kernel,
        grid=(m // tm,),
        in_specs=[pl.BlockSpec((tm, k), lambda i: (i, 0)),
                  pl.BlockSpec((k, n), lambda i: (0, 0)),
                  pl.BlockSpec((tm, n), lambda i: (i, 0))],
        out_specs=pl.BlockSpec((tm, n), lambda i: (i, 0)),
        out_shape=jax.ShapeDtypeStruct((m, n), F32),
        compiler_params=_params(("arbitrary",)),
        name="out_proj",
    )(y2d, w_bf16, x2d)


def _hgrn_levels(chunk):
    return [1 << j for j in range(int(np.log2(chunk)))]


def _hgrn_level_exponents(b, log_f, b_view, sgn_ref, chunk):
    out = {}
    n_vregs = chunk // SUBLANES
    width = b.shape[-1]

    def row(r, n):
        return jnp.broadcast_to(b_view[pl.ds(r, 1), :], (n, width))

    for m in _hgrn_levels(chunk):
        if m >= SUBLANES:
            parts = []
            for j in range(chunk // (2 * m)):
                base = j * 2 * m
                ref_row = row(base + m - 1, m)
                parts.append(ref_row - b[base:base + m, :])
                parts.append(b[base + m:base + 2 * m, :] - ref_row)
            out[m] = jnp.concatenate(parts, axis=0)
        elif m == 4:
            ref = jnp.concatenate([row(SUBLANES * i + 3, SUBLANES) for i in range(n_vregs)], axis=0)
            out[m] = (b - ref) * sgn_ref[0]
        elif m == 2:
            lo = jnp.concatenate([row(SUBLANES * i + 1, SUBLANES) for i in range(n_vregs)], axis=0)
            hi = jnp.concatenate([row(SUBLANES * i + 5, SUBLANES) for i in range(n_vregs)], axis=0)
            ref = jnp.where(sgn_ref[0] > 0.0, hi, lo)
            out[m] = (b - ref) * sgn_ref[1]
        else:
            out[m] = log_f * sgn_ref[2]
    return out


def _hgrn_kernel(x_ref, nw_ref, w_ref, lbraw_ref, onw_ref, tri_ref, y_ref,
                 h_ref, act_ref, b_ref, st_ref, lvl_ref, sgn_ref, qd_ref, kd_ref, a16_ref, dec_ref,
                 *, layer_idx, seq, chunk, hps, cpi):
    step = pl.program_id(1)
    dh = A_HEAD_DIM
    rb = 256

    @pl.when(step == 0)
    def _():
        for i in range(seq // rb):
            xs = x_ref[0, i * rb:(i + 1) * rb, :]
            h_ref[i * rb:(i + 1) * rb, :] = _rms_rows(xs, nw_ref[...]).astype(BF16)

    raw = lbraw_ref[...]
    n_layers = raw.shape[0]
    mx = raw[0:1, :]
    for i in range(1, n_layers):
        mx = jnp.maximum(mx, raw[i:i + 1, :])
    es = [jnp.exp(raw[i:i + 1, :] - mx) for i in range(n_layers)]
    z = es[0]
    for i in range(1, n_layers):
        z = z + es[i]
    soft = [e / z for e in es]
    csum = soft[0]
    for i in range(1, layer_idx + 1):
        csum = csum + soft[i]
    lb = csum - soft[0]
    log_lb = jnp.log(jnp.maximum(lb, LB_FLOOR))
    log1m_lb = jnp.log1p(-lb)
    one_m_lb = 1.0 - lb
    heads = range(hps)

    pb = 512
    for i in range(seq // pb):
        rows = slice(i * pb, (i + 1) * pb)
        res = _dot(h_ref[rows, :], w_ref[...])
        for a in heads:
            c0 = a * 4 * dh
            o0 = a * 5 * dh
            lbs = slice(a * dh, (a + 1) * dh)
            pf = res[:, c0 + dh:c0 + 2 * dh]
            log_sig = jnp.minimum(pf, 0.0) - jnp.log(1.0 + jnp.exp(-jnp.abs(pf)))
            bb = log1m_lb[:, lbs] + log_sig
            ll = log_lb[:, lbs]
            log_f = jnp.maximum(ll, bb) + jnp.log(1.0 + jnp.exp(-jnp.abs(ll - bb)))
            act_ref[rows, o0:o0 + dh] = _silu(res[:, c0:c0 + dh])
            act_ref[rows, o0 + dh:o0 + 2 * dh] = one_m_lb[:, lbs] * jnp.exp(log_sig - pf)
            act_ref[rows, o0 + 2 * dh:o0 + 3 * dh] = res[:, c0 + 2 * dh:c0 + 3 * dh]
            act_ref[rows, o0 + 3 * dh:o0 + 4 * dh] = _silu(res[:, c0 + 3 * dh:c0 + 4 * dh])
            act_ref[rows, o0 + 4 * dh:o0 + 5 * dh] = log_f * LOG2E

    ti = lax.broadcasted_iota(jnp.int32, (chunk, chunk), 0)
    si = lax.broadcasted_iota(jnp.int32, (chunk, chunk), 1)
    lvl_ref[...] = jnp.where(si < ti, ti ^ si, jnp.where(si == ti, 0, -1))
    tr = lax.broadcasted_iota(jnp.int32, (chunk, dh), 0)
    sgn_ref[0] = jnp.where((tr & 4) != 0, 1.0, -1.0)
    sgn_ref[1] = jnp.where((tr & 2) != 0, 1.0, -1.0)
    sgn_ref[2] = jnp.where((tr & 1) != 0, 1.0, 0.0)
    st_ref[...] = jnp.zeros_like(st_ref)
    onw = onw_ref[...]
    levels = _hgrn_levels(chunk)

    def row_start(c):
        return c * chunk if isinstance(c, int) else pl.multiple_of(c * chunk, chunk)

    def front(chunks):
        q, k, log_f = [], [], []
        for c in chunks:
            for a in heads:
                o0 = a * 5 * dh
                q.append(act_ref[pl.ds(row_start(c), chunk), o0:o0 + dh])
                k.append(act_ref[pl.ds(row_start(c), chunk), o0 + dh:o0 + 2 * dh])
                log_f.append(act_ref[pl.ds(row_start(c), chunk), o0 + 4 * dh:o0 + 5 * dh])
        slots = range(len(q))

        b = []
        for a in slots:
            lf_hi = log_f[a].astype(BF16)
            lf_lo = (log_f[a] - lf_hi.astype(F32)).astype(BF16)
            b.append(_dot(tri_ref[...], lf_hi) + _dot(tri_ref[...], lf_lo))
            b_ref[a] = b[a]

        expo, acc = [], []
        for a in slots:
            expo.append(_hgrn_level_exponents(b[a], log_f[a], b_ref.at[a], sgn_ref, chunk))
            qd_ref[a] = (q[a] * jnp.exp2(b[a])).astype(BF16)
            b_last = b_ref[a, chunk - 1:chunk, :]
            kd_ref[a] = (k[a] * jnp.exp2(b_last - b[a])).astype(BF16)
            dec_ref[a] = jnp.exp2(b_last)
            acc.append(jnp.where(lvl_ref[...] == 0, _dot_nt(q[a].astype(BF16), k[a].astype(BF16)), 0.0))
        for m in levels:
            if m >= SUBLANES:
                break
            for a in slots:
                p = jnp.exp2(expo[a][m])
                a_m = _dot_nt((q[a] * p).astype(BF16), (k[a] * p).astype(BF16))
                acc[a] = jnp.where(lvl_ref[...] >= m, a_m, acc[a])
        n_grp = chunk // SUBLANES
        acc = [[acc_a[SUBLANES * i:SUBLANES * (i + 1), :] for i in range(n_grp)] for acc_a in acc]
        for m in levels:
            if m < SUBLANES:
                continue
            is_query = [(SUBLANES * i) % (2 * m) >= m for i in range(n_grp)]
            for a in slots:
                p = jnp.exp2(expo[a][m])
                qp, kp = q[a] * p, k[a] * p
                q_m = jnp.concatenate([qp[SUBLANES * i:SUBLANES * (i + 1), :]
                                       for i in range(n_grp) if is_query[i]], axis=0)
                k_m = jnp.concatenate([k[a][SUBLANES * i:SUBLANES * (i + 1), :] if is_query[i]
                                       else kp[SUBLANES * i:SUBLANES * (i + 1), :]
                                       for i in range(n_grp)], axis=0)
                a_m = _dot_nt(q_m.astype(BF16), k_m.astype(BF16))
                for n, i in enumerate([i for i in range(n_grp) if is_query[i]]):
                    rows = slice(SUBLANES * i, SUBLANES * (i + 1))
                    acc[a][i] = jnp.where(lvl_ref[rows, :] >= m,
                                          a_m[SUBLANES * n:SUBLANES * (n + 1), :], acc[a][i])
        for a in slots:
            a16_ref[a] = jnp.concatenate(acc[a], axis=0).astype(BF16)

    def back(chunks):
        for n, c in enumerate(chunks):
            r0 = row_start(c)
            for a in heads:
                o0 = a * 5 * dh
                slot = n * hps + a
                v16 = act_ref[pl.ds(r0, chunk), o0 + 2 * dh:o0 + 3 * dh].astype(BF16)
                st = st_ref[a]
                oa = _dot_nt(qd_ref[slot], st.astype(BF16)) + _dot(a16_ref[slot], v16)
                st_ref[a] = st * dec_ref[slot] + _dot_tn(v16, kd_ref[slot])
                y = _rms_rows(oa, onw) * act_ref[pl.ds(r0, chunk), o0 + 3 * dh:o0 + 4 * dh]
                y_ref[0, pl.ds(r0, chunk), a * dh:(a + 1) * dh] = y.astype(BF16)

    n_iter = seq // (chunk * cpi)
    front([n for n in range(cpi)])

    def body(i, carry):
        back([(i - 1) * cpi + n for n in range(cpi)])
        front([i * cpi + n for n in range(cpi)])
        return carry

    lax.fori_loop(1, n_iter, body, 0)
    back([(n_iter - 1) * cpi + n for n in range(cpi)])


def _hgrn_mixer(x, norm_w_row, w_in_heads, lb_raw, o_norm_w_row, layer_idx):
    bsz, seq, d = x.shape
    dh = A_HEAD_DIM
    hps = A_HEADS_PER_STEP
    n_steps = d // (dh * hps)
    chunk = A_CHUNK
    tri = jnp.asarray(np.tril(np.ones((chunk, chunk), np.float32)), BF16)
    cpi = A_CHUNKS_PER_ITER
    n_slots = hps * cpi
    assert seq % (chunk * cpi) == 0
    kern = functools.partial(_hgrn_kernel, layer_idx=layer_idx, seq=seq, chunk=chunk, hps=hps, cpi=cpi)
    return pl.pallas_call(
        kern,
        grid=(bsz, n_steps),
        in_specs=[pl.BlockSpec((1, seq, d), lambda b, h: (b, 0, 0), pipeline_mode=pl.Buffered(1)),
                  pl.BlockSpec((1, d), lambda b, h: (0, 0)),
                  pl.BlockSpec((d, hps * 4 * dh), lambda b, h: (0, h)),
                  pl.BlockSpec((lb_raw.shape[0], hps * dh), lambda b, h: (0, h)),
                  pl.BlockSpec((1, dh), lambda b, h: (0, 0)),
                  pl.BlockSpec((chunk, chunk), lambda b, h: (0, 0))],
        out_specs=pl.BlockSpec((1, seq, hps * dh), lambda b, h: (b, 0, h)),
        out_shape=jax.ShapeDtypeStruct((bsz, seq, d), BF16),
        scratch_shapes=[pltpu.VMEM((seq, d), BF16),
                        pltpu.VMEM((seq, hps * 5 * dh), F32),
                        pltpu.VMEM((n_slots, chunk, dh), F32),
                        pltpu.VMEM((hps, dh, dh), F32),
                        pltpu.VMEM((chunk, chunk), jnp.int32),
                        pltpu.VMEM((3, chunk, dh), F32),
                        pltpu.VMEM((n_slots, chunk, dh), BF16),
                        pltpu.VMEM((n_slots, chunk, dh), BF16),
                        pltpu.VMEM((n_slots, chunk, chunk), BF16),
                        pltpu.VMEM((n_slots, 1, dh), F32)],
        compiler_params=_params(("arbitrary", "arbitrary")),
        name="hgrn_mixer",
    )(x, norm_w_row, w_in_heads, lb_raw, o_norm_w_row, tri)


def _bproj_kernel(x_ref, nw_ref, w_ref, hw_ref, eseg_ref, esegt_ref, o_ref, h_ref, hn_ref, *, seq, dils):
    j = pl.program_id(1)
    n_qkv_tiles = 3 * len(dils)

    @pl.when(j == 0)
    def _():
        d = x_ref.shape[-1]
        n_planes = d // LANES
        rb = 256
        for i in range(seq // rb):
            hn = _rms_rows(x_ref[0, i * rb:(i + 1) * rb, :], nw_ref[...])
            for c in range(n_planes):
                hn_ref[c, i * rb:(i + 1) * rb, :] = hn[:, c * LANES:(c + 1) * LANES]
        for g, dil in enumerate(dils):
            sub = seq // dil
            for r in range(dil):
                for c0 in range(0, sub, rb):
                    rows = min(rb, sub - c0)
                    if dil == 1:
                        idx = pl.ds(c0, rows)
                    else:
                        idx = pl.ds(r + c0 * dil, rows, stride=dil)
                    for c in range(n_planes):
                        h_ref[g, r * sub + c0:r * sub + c0 + rows, c * LANES:(c + 1) * LANES] = (
                            hn_ref[c, idx, :].astype(BF16))

    grp = jnp.where(j < n_qkv_tiles, j // 3, 0)
    kind = jnp.where(j < n_qkv_tiles, j % 3, 3)
    pb = 512

    def tile(i):
        return _dot(h_ref[grp, i * pb:(i + 1) * pb, :], w_ref[...])

    @pl.when(kind < 2)
    def _():
        hw = hw_ref[0]
        n_heads = x_ref.shape[-1] // B_HEAD_DIM
        hi_lanes = lax.broadcasted_iota(jnp.int32, (pb, LANES), 1) < n_heads
        for i in range(seq // pb):
            y = tile(i)
            ssq = _dot((y * y).astype(BF16), eseg_ref[...])
            r = lax.rsqrt(ssq * (1.0 / B_HEAD_DIM) + NORM_EPS)
            r_hi = r.astype(BF16).astype(F32)
            r_split = jnp.where(hi_lanes, r_hi, r - r_hi).astype(BF16)
            rexp = _dot(r_split, esegt_ref[...])
            o_ref[0, i * pb:(i + 1) * pb, :] = (y * rexp * hw).astype(BF16)

    @pl.when(kind == 2)
    def _():
        for i in range(seq // pb):
            o_ref[0, i * pb:(i + 1) * pb, :] = tile(i).astype(BF16)

    @pl.when(kind == 3)
    def _():
        for i in range(seq // pb):
            o_ref[0, i * pb:(i + 1) * pb, :] = _silu(tile(i)).astype(BF16)


def _attn_in_proj(x, norm_w_row, w_bf16, head_w):
    bsz, seq, d = x.shape
    n_out = w_bf16.shape[1]
    tn = d
    dils = tuple(dil for _, dil in DILATED_GROUPS)
    n_heads = d // B_HEAD_DIM
    assert 2 * n_heads <= LANES
    eseg = np.zeros((d, LANES), np.float32)
    eseg[np.arange(d), np.arange(d) // B_HEAD_DIM] = 1.0
    eseg[np.arange(d), n_heads + np.arange(d) // B_HEAD_DIM] = 1.0
    eseg = jnp.asarray(eseg, BF16)
    kern = functools.partial(_bproj_kernel, seq=seq, dils=dils)
    return pl.pallas_call(
        kern,
        grid=(bsz, n_out // tn),
        in_specs=[pl.BlockSpec((1, seq, d), lambda b, j: (b, 0, 0), pipeline_mode=pl.Buffered(1)),
                  pl.BlockSpec((1, d), lambda b, j: (0, 0)),
                  pl.BlockSpec((d, tn), lambda b, j: (0, j)),
                  pl.BlockSpec((1, 1, tn), lambda b, j: (j, 0, 0)),
                  pl.BlockSpec((d, LANES), lambda b, j: (0, 0)),
                  pl.BlockSpec((LANES, d), lambda b, j: (0, 0))],
        out_specs=pl.BlockSpec((1, seq, tn), lambda b, j: (b, 0, j)),
        out_shape=jax.ShapeDtypeStruct((bsz, seq, n_out), BF16),
        scratch_shapes=[pltpu.VMEM((len(dils), seq, d), BF16),
                        pltpu.VMEM((d // LANES, seq, LANES), F32)],
        compiler_params=_params(("arbitrary", "arbitrary")),
        name="attn_in_proj",
    )(x, norm_w_row, w_bf16, head_w, eseg, eseg.T)


def _attn_kernel(slopes_ref, q0, k0, v0, q1, k1, v1, q2, k2, v2, g_ref, o_ref,
                 num_ref, den_ref, max_ref, maxsw_ref, bias_ref, s_ref, p_ref, *, seq, dils, unrolls):
    hp = pl.program_id(1)
    blk = B_BLOCK
    hd = B_HEAD_DIM
    n_heads = (LANES // hd) * pl.num_programs(1)
    qkv = ((q0, k0, v0), (q1, k1, v1), (q2, k2, v2))
    head_a = lax.broadcasted_iota(jnp.int32, (blk, LANES), 1) < hd


    def token_rows(g, tok0):
        return pl.ds(tok0, blk) if dils[g] == 1 else pl.ds(tok0, blk, stride=dils[g])

    def qk_phase(g, specs, has_prev, slot):
        q_ref, k_ref, _ = qkv[g]
        nk = 2 * blk if has_prev else blk
        for w, (row0, _) in enumerate(specs):
            k_lo = row0 - blk if has_prev else row0
            q2d = q_ref[0, pl.ds(row0, blk), :]
            zero = jnp.zeros_like(q2d)
            qq = jnp.concatenate([jnp.where(head_a, q2d, zero), jnp.where(head_a, zero, q2d)], axis=0)
            s_ref[slot, w, :, 0:nk] = _dot_nt(qq, k_ref[0, pl.ds(k_lo, nk), :])

    def softmax_phase(g, specs, has_prev, slot):
        nk = 2 * blk if has_prev else blk
        off = 0 if has_prev else blk
        for w, (_, tok0) in enumerate(specs):
            ms = []
            for a in range(2):
                sa = s_ref[slot, w, a * blk:(a + 1) * blk, 0:nk] + bias_ref[a, :, off:off + nk]
                m = jnp.max(sa, axis=-1, keepdims=True)
                p_ref[slot, w, a, :, 0:nk] = jnp.exp2(sa - m).astype(BF16)
                ms.append(m)
            idx = token_rows(g, tok0)
            max_ref[g, idx, :] = jnp.where(head_a, ms[0], ms[1])
            maxsw_ref[g, idx, :] = jnp.where(head_a, ms[1], ms[0])

    def pv_phase(g, specs, has_prev, slot):
        v_ref = qkv[g][2]
        nk = 2 * blk if has_prev else blk
        head_a_k = lax.broadcasted_iota(jnp.int32, (nk, LANES), 1) < hd
        for w, (row0, tok0) in enumerate(specs):
            k_lo = row0 - blk if has_prev else row0
            v2d = v_ref[0, pl.ds(k_lo, nk), :]
            one = jnp.ones_like(v2d)
            oa = _dot(p_ref[slot, w, 0, :, 0:nk], jnp.where(head_a_k, v2d, one))
            ob = _dot(p_ref[slot, w, 1, :, 0:nk], jnp.where(head_a_k, one, v2d))
            idx = token_rows(g, tok0)
            num_ref[g, idx, :] = jnp.where(head_a, oa, ob)
            den_ref[g, idx, :] = jnp.where(head_a, ob, oa)

    def pipelined(n_iter, make_specs, g, has_prev):
        def step(t):
            static = isinstance(t, int)
            if (not static) or 0 <= t - 2 < n_iter:
                pv_phase(g, make_specs(t - 2), has_prev, t % 2)
            if (not static) or 0 <= t - 1 < n_iter:
                softmax_phase(g, make_specs(t - 1), has_prev, (t - 1) % 2)
            if (not static) or 0 <= t < n_iter:
                qk_phase(g, make_specs(t), has_prev, t % 2)

        for t in range(2):
            step(t)
        if n_iter > 2:
            def body(t, carry):
                step(t)
                return carry

            lax.fori_loop(2, n_iter, body, 0)
        for t in range(max(n_iter, 2), n_iter + 2):
            step(t)

    qi = lax.broadcasted_iota(jnp.int32, (blk, 2 * blk), 0)
    kj = lax.broadcasted_iota(jnp.int32, (blk, 2 * blk), 1)
    dist = blk + qi - kj
    valid = (dist >= 0) & (dist <= blk)

    for g, dil in enumerate(dils):
        sub = seq // dil
        n_blk = sub // blk
        bias_base = jnp.where(valid, -(dist * dil).astype(F32), NEG_BIG)
        for a in range(2):
            slope = slopes_ref[g * n_heads + 2 * hp + a]
            bias_ref[a] = (slope * LOG2E) * bias_base

        u1, u2 = unrolls[g]

        def aligned(row):
            return row if isinstance(row, int) else pl.multiple_of(row, blk)

        def first_specs(i, sub=sub, u1=u1):
            return [(aligned((i * u1 + w) * sub), i * u1 + w) for w in range(u1)]

        def rest_specs(i, dil=dil, sub=sub, n_blk=n_blk, u2=u2):
            specs = []
            for w in range(u2):
                idx = i * u2 + w
                r = idx // (n_blk - 1)
                b = idx % (n_blk - 1) + 1
                specs.append((aligned(r * sub + b * blk), b * blk * dil + r))
            return specs

        pipelined(dil // u1, first_specs, g, False)
        n_rest = dil * (n_blk - 1)
        if n_rest:
            pipelined(n_rest // u2, rest_specs, g, True)

    rb = 256
    n_g = len(dils)

    def merge(i, carry):
        r0 = pl.multiple_of(i * rb, rb)
        ms = [max_ref[g, pl.ds(r0, rb), :] for g in range(n_g)]
        mx = ms[0]
        for g in range(1, n_g):
            mx = jnp.maximum(mx, ms[g])
        ms_sw = [maxsw_ref[g, pl.ds(r0, rb), :] for g in range(n_g)]
        mx_sw = ms_sw[0]
        for g in range(1, n_g):
            mx_sw = jnp.maximum(mx_sw, ms_sw[g])
        num = jnp.zeros((rb, LANES), F32)
        den = jnp.zeros((rb, LANES), F32)
        for g in range(n_g):
            num = num + jnp.exp2(ms[g] - mx) * num_ref[g, pl.ds(r0, rb), :]
            den = den + jnp.exp2(ms_sw[g] - mx_sw) * den_ref[g, pl.ds(r0, rb), :]
        den = pltpu.roll(den, hd, axis=1)
        gate = g_ref[0, pl.ds(r0, rb), :].astype(F32)
        o_ref[0, pl.ds(r0, rb), :] = (num / den * gate).astype(BF16)
        return carry

    lax.fori_loop(0, seq // rb, merge, 0)


def _attention(proj, slopes):
    bsz, seq, n_out = proj.shape
    d = n_out // (3 * B_N_GROUPS + 1)
    n_pairs = d // LANES
    dils = tuple(dil for _, dil in DILATED_GROUPS)
    unrolls = ((1, 5), (4, 6), (4, 1))
    for (u1, u2), dil in zip(unrolls, dils):
        n_rest = dil * (seq // dil // B_BLOCK - 1)
        assert dil % u1 == 0 and n_rest % u2 == 0

    def col_spec(tile):
        return pl.BlockSpec((1, seq, LANES), lambda b, h, s, tile=tile: (b, 0, tile * n_pairs + h))

    kern = functools.partial(_attn_kernel, seq=seq, dils=dils, unrolls=unrolls)
    grid_spec = pltpu.PrefetchScalarGridSpec(
        num_scalar_prefetch=1,
        grid=(bsz, n_pairs),
        in_specs=[col_spec(t) for t in range(3 * B_N_GROUPS + 1)],
        out_specs=pl.BlockSpec((1, seq, LANES), lambda b, h, s: (b, 0, h)),
        scratch_shapes=[pltpu.VMEM((B_N_GROUPS, seq, LANES), F32)] * 4
        + [pltpu.VMEM((2, B_BLOCK, 2 * B_BLOCK), F32),
           pltpu.VMEM((2, max(max(u) for u in unrolls), 2 * B_BLOCK, 2 * B_BLOCK), F32),
           pltpu.VMEM((2, max(max(u) for u in unrolls), 2, B_BLOCK, 2 * B_BLOCK), BF16)],
    )
    return pl.pallas_call(
        kern,
        grid_spec=grid_spec,
        out_shape=jax.ShapeDtypeStruct((bsz, seq, d), BF16),
        compiler_params=_params(("arbitrary", "arbitrary")),
        name="dilated_attention",
    )(slopes, *([proj] * (3 * B_N_GROUPS + 1)))


def _gmlp_kernel(x_ref, nw_ref, w_ref, vw_ref, vb_ref, ws_ref, bias_ref, y_ref,
                 h_ref, u_ref, v_ref, g_ref, *, tm):
    d = x_ref.shape[-1]
    cg = d // C_GROUPS
    rb = 256
    for i in range(tm // rb):
        h_ref[i * rb:(i + 1) * rb, :] = _rms_rows(x_ref[0, i * rb:(i + 1) * rb, :], nw_ref[...]).astype(BF16)
    for i in range(tm // rb):
        rows = slice(i * rb, (i + 1) * rb)
        hh = h_ref[rows, :]
        u_ref[rows, :] = _gelu_tanh(_dot(hh, w_ref[:, 0:d]))
        vv = _gelu_tanh(_dot(hh, w_ref[:, d:2 * d]))
        mu = jnp.mean(vv, axis=-1, keepdims=True)
        vc = vv - mu
        var = jnp.mean(vc * vc, axis=-1, keepdims=True)
        v_ref[rows, :] = (vc * lax.rsqrt(var + NORM_EPS) * vw_ref[...] + vb_ref[...]).astype(BF16)
        g_ref[rows, :] = _silu(_dot(hh, w_ref[:, 2 * d:3 * d]))
    ti = lax.broadcasted_iota(jnp.int32, (C_CHUNK, C_CHUNK), 0)
    si = lax.broadcasted_iota(jnp.int32, (C_CHUNK, C_CHUNK), 1)
    n_ch = tm // C_CHUNK
    for grp in range(C_GROUPS):
        cols = slice(grp * cg, (grp + 1) * cg)
        wc = jnp.where(si <= ti, ws_ref[grp], 0.0).astype(BF16)
        rhs = jnp.concatenate([v_ref[n * C_CHUNK:(n + 1) * C_CHUNK, cols] for n in range(n_ch)], axis=1)
        s = _dot(wc, rhs)
        bias = bias_ref[:, cols]
        for n in range(n_ch):
            rows = slice(n * C_CHUNK, (n + 1) * C_CHUNK)
            sn = s[:, n * cg:(n + 1) * cg] + bias
            y_ref[0, rows, cols] = (u_ref[rows, cols] * sn * g_ref[rows, cols]).astype(BF16)


def _gmlp_mixer(x, norm_w_row, w_bf16, vw_row, vb_row, w_s, bias_full, tm=512):
    bsz, seq, d = x.shape
    kern = functools.partial(_gmlp_kernel, tm=tm)
    return pl.pallas_call(
        kern,
        grid=(bsz, seq // tm),
        in_specs=[pl.BlockSpec((1, tm, d), lambda b, i: (b, i, 0)),
                  pl.BlockSpec((1, d), lambda b, i: (0, 0)),
                  pl.BlockSpec((d, 3 * d), lambda b, i: (0, 0)),
                  pl.BlockSpec((1, d), lambda b, i: (0, 0)),
                  pl.BlockSpec((1, d), lambda b, i: (0, 0)),
                  pl.BlockSpec(w_s.shape, lambda b, i: (0, 0, 0)),
                  pl.BlockSpec((C_CHUNK, d), lambda b, i: (0, 0))],
        out_specs=pl.BlockSpec((1, tm, d), lambda b, i: (b, i, 0)),
        out_shape=jax.ShapeDtypeStruct((bsz, seq, d), BF16),
        scratch_shapes=[pltpu.VMEM((tm, d), BF16),
                        pltpu.VMEM((tm, d), F32),
                        pltpu.VMEM((tm, d), BF16),
                        pltpu.VMEM((tm, d), F32)],
        compiler_params=_params(("arbitrary", "arbitrary")),
        name="gmlp_mixer",
    )(x, norm_w_row, w_bf16, vw_row, vb_row, w_s, bias_full)


def kernel(x, norm_w, a_w_in, a_lower_bounds, a_o_norm_w, a_w_out, b_w_in, b_q_norm_w, b_k_norm_w, b_w_out,
           c_w_in, c_v_norm_w, c_v_norm_b, c_w_s, c_b_s, c_w_out):
    bsz, seq, d = x.shape
    depth = norm_w.shape[0]
    n_mixers = 3
    assert seq == DILATED_GROUPS[-1][0] and seq % A_CHUNK == 0 and d % LANES == 0

    def residual(y, w_out, xin):
        out = _out_proj(y.reshape(bsz * seq, d), w_out.astype(BF16), xin.reshape(bsz * seq, d))
        return out.reshape(bsz, seq, d)

    for layer in range(depth):
        kind, idx = layer % n_mixers, layer // n_mixers
        nw = norm_w[layer][None, :]
        if kind == 0:
            n_heads = d // A_HEAD_DIM
            w = a_w_in[idx].reshape(d, 4, n_heads, A_HEAD_DIM).transpose(0, 2, 1, 3)
            w = w.reshape(d, 4 * d).astype(BF16)
            y = _hgrn_mixer(x, nw, w, a_lower_bounds, a_o_norm_w[idx][None, :], idx)
            x = residual(y, a_w_out[idx], x)
        elif kind == 1:
            n_heads = d // B_HEAD_DIM
            n_total = B_N_GROUPS * n_heads
            slopes = jnp.exp2(-ALIBI_MAX_EXP * jnp.arange(1, n_total + 1, dtype=F32) / n_total)
            rows = []
            for g in range(B_N_GROUPS):
                rows.append(jnp.tile(b_q_norm_w[idx, g], n_heads) * (B_HEAD_DIM ** -0.5 * LOG2E))
                rows.append(jnp.tile(b_k_norm_w[idx, g], n_heads))
                rows.append(jnp.ones((d,), F32))
            rows.append(jnp.ones((d,), F32))
            head_w = jnp.stack(rows)[:, None, :]
            proj = _attn_in_proj(x, nw, b_w_in[idx].astype(BF16), head_w)
            y = _attention(proj, slopes)
            x = residual(y, b_w_out[idx], x)
        else:
            bias_full = jnp.repeat(c_b_s[idx].T, d // C_GROUPS, axis=1)
            y = _gmlp_mixer(x, nw, c_w_in[idx].astype(BF16), c_v_norm_w[idx][None, :],
                            c_v_norm_b[idx][None, :], c_w_s[idx], bias_full)
            x = residual(y, c_w_out[idx], x)
    return x
```

```python
import functools

import numpy as np
import jax
import jax.numpy as jnp
from jax import lax
from jax.experimental import pallas as pl
from jax.experimental.pallas import tpu as pltpu

F32 = jnp.float32
BF16 = jnp.bfloat16

NORM_EPS = 1e-6
NEG_BIG = -1e30
LB_FLOOR = 1e-30
LOG2E = float(np.log2(np.e))

LANES = 128
SUBLANES = 8
VMEM_LIMIT_BYTES = 56 * 1024 * 1024

A_HEAD_DIM = 128
A_CHUNK = 128
A_CHUNKS_PER_ITER = 2
A_HEADS_PER_STEP = 4
B_HEAD_DIM = 64
DILATED_GROUPS = ((128, 1), (512, 4), (2048, 16))
B_N_GROUPS = len(DILATED_GROUPS)
B_BLOCK = 128
ALIBI_MAX_EXP = 8.0
C_CHUNK = 128
C_GROUPS = 8


def _silu(x):
    hx = 0.5 * x
    return hx + hx * jnp.tanh(hx)


def _gelu_tanh(x):
    c = np.float32(np.sqrt(2.0 / np.pi))
    return 0.5 * x * (1.0 + jnp.tanh(c * (x + 0.044715 * (x * x * x))))


def _rms_rows(xf, w):
    ms = jnp.mean(xf * xf, axis=-1, keepdims=True)
    return xf * lax.rsqrt(ms + NORM_EPS) * w


def _dot(a, b):
    return jnp.dot(a, b, preferred_element_type=F32)


def _dot_nt(a, b):
    return lax.dot_general(a, b, (((1,), (1,)), ((), ())), preferred_element_type=F32)


def _dot_tn(a, b):
    return lax.dot_general(a, b, (((0,), (0,)), ((), ())), preferred_element_type=F32)


def _params(sem):
    return pltpu.CompilerParams(dimension_semantics=sem, vmem_limit_bytes=VMEM_LIMIT_BYTES)


def _out_proj_kernel(y_ref, w_ref, x_ref, o_ref):
    o_ref[...] = x_ref[...] + _dot(y_ref[...], w_ref[...])


def _out_proj(y2d, w_bf16, x2d, tm=1024):
    m, k = y2d.shape
    n = w_bf16.shape[1]
    return pl.pallas_call(
        _out_proj_kernel,
        grid=(m // tm,),
        in_specs=[pl.BlockSpec((tm, k), lambda i: (i, 0)),
                  pl.BlockSpec((k, n), lambda i: (0, 0)),
                  pl.BlockSpec((tm, n), lambda i: (i, 0))],
        out_specs=pl.BlockSpec((tm, n), lambda i: (i, 0)),
        out_shape=jax.ShapeDtypeStruct((m, n), F32),
        compiler_params=_params(("arbitrary",)),
        name="out_proj",
    )(y2d, w_bf16, x2d)


def _hgrn_levels(chunk):
    return [1 << j for j in range(int(np.log2(chunk)))]


def _hgrn_level_exponents(b, log_f, b_view, sgn_ref, chunk):
    out = {}
    n_vregs = chunk // SUBLANES
    width = b.shape[-1]

    def row(r, n):
        one_group = jnp.broadcast_to(b_view[pl.ds(r, 1), :], (SUBLANES, width))
        return jnp.concatenate([one_group] * (n // SUBLANES), axis=0)

    for m in _hgrn_levels(chunk):
        if m >= SUBLANES:
            parts = []
            for j in range(chunk // (2 * m)):
                base = j * 2 * m
                ref_row = row(base + m - 1, m)
                parts.append(ref_row - b[base:base + m, :])
                parts.append(b[base + m:base + 2 * m, :] - ref_row)
            out[m] = jnp.concatenate(parts, axis=0)
        elif m == 4:
            ref = jnp.concatenate([row(SUBLANES * i + 3, SUBLANES) for i in range(n_vregs)], axis=0)
            out[m] = (b - ref) * sgn_ref[0]
        elif m == 2:
            lo = jnp.concatenate([row(SUBLANES * i + 1, SUBLANES) for i in range(n_vregs)], axis=0)
            hi = jnp.concatenate([row(SUBLANES * i + 5, SUBLANES) for i in range(n_vregs)], axis=0)
            ref = jnp.where(sgn_ref[0] > 0.0, hi, lo)
            out[m] = (b - ref) * sgn_ref[1]
        else:
            out[m] = log_f * sgn_ref[2]
    return out


def _hgrn_kernel(x_ref, nw_ref, wq_ref, wf_ref, wi_ref, wg_ref, lbraw_ref, onw_ref, tri_ref, y_ref,
                 h_ref, act_ref, v16_ref, b_ref, st_ref, lvl_ref, sgn_ref, qd_ref, kd_ref, a16_ref, dec_ref,
                 *, layer_idx, seq, chunk, hps, cpi):
    step = pl.program_id(1)
    dh = A_HEAD_DIM
    rb = 256

    @pl.when(step == 0)
    def _():
        for i in range(seq // rb):
            xs = x_ref[0, i * rb:(i + 1) * rb, :]
            h_ref[i * rb:(i + 1) * rb, :] = _rms_rows(xs, nw_ref[...]).astype(BF16)

    raw = lbraw_ref[...]
    n_layers = raw.shape[0]
    mx = raw[0:1, :]
    for i in range(1, n_layers):
        mx = jnp.maximum(mx, raw[i:i + 1, :])
    es = [jnp.exp(raw[i:i + 1, :] - mx) for i in range(n_layers)]
    z = es[0]
    for i in range(1, n_layers):
        z = z + es[i]
    soft = [e / z for e in es]
    csum = soft[0]
    for i in range(1, layer_idx + 1):
        csum = csum + soft[i]
    lb = csum - soft[0]
    log_lb = jnp.log(jnp.maximum(lb, LB_FLOOR))
    log1m_lb = jnp.log1p(-lb)
    one_m_lb = 1.0 - lb
    heads = range(hps)

    pb = 512
    for i in range(seq // pb):
        rows = slice(i * pb, (i + 1) * pb)
        hh = h_ref[rows, :]
        res_q, res_f, res_i, res_g = (_dot(hh, w[...]) for w in (wq_ref, wf_ref, wi_ref, wg_ref))
        v16_ref[rows, :] = res_i.astype(BF16)
        for a in heads:
            o0 = a * 4 * dh
            lbs = slice(a * dh, (a + 1) * dh)
            pf = res_f[:, lbs]
            log_sig = jnp.minimum(pf, 0.0) - jnp.log(1.0 + jnp.exp(-jnp.abs(pf)))
            bb = log1m_lb[:, lbs] + log_sig
            ll = log_lb[:, lbs]
            log_f = jnp.maximum(ll, bb) + jnp.log(1.0 + jnp.exp(-jnp.abs(ll - bb)))
            act_ref[rows, o0:o0 + dh] = _silu(res_q[:, lbs])
            act_ref[rows, o0 + dh:o0 + 2 * dh] = one_m_lb[:, lbs] * jnp.exp(log_sig - pf)
            act_ref[rows, o0 + 2 * dh:o0 + 3 * dh] = _silu(res_g[:, lbs])
            act_ref[rows, o0 + 3 * dh:o0 + 4 * dh] = log_f * LOG2E

    ti = lax.broadcasted_iota(jnp.int32, (chunk, chunk), 0)
    si = lax.broadcasted_iota(jnp.int32, (chunk, chunk), 1)
    lvl_ref[...] = jnp.where(si < ti, ti ^ si, jnp.where(si == ti, 0, -1))
    tr = lax.broadcasted_iota(jnp.int32, (chunk, dh), 0)
    sgn_ref[0] = jnp.where((tr & 4) != 0, 1.0, -1.0)
    sgn_ref[1] = jnp.where((tr & 2) != 0, 1.0, -1.0)
    sgn_ref[2] = jnp.where((tr & 1) != 0, 1.0, 0.0)
    st_ref[...] = jnp.zeros_like(st_ref)
    onw = onw_ref[...]
    levels = _hgrn_levels(chunk)

    def row_start(c):
        return c * chunk if isinstance(c, int) else pl.multiple_of(c * chunk, chunk)

    def front(chunks):
        q, k, log_f = [], [], []
        for c in chunks:
            for a in heads:
                o0 = a * 4 * dh
                q.append(act_ref[pl.ds(row_start(c), chunk), o0:o0 + dh])
                k.append(act_ref[pl.ds(row_start(c), chunk), o0 + dh:o0 + 2 * dh])
                log_f.append(act_ref[pl.ds(row_start(c), chunk), o0 + 3 * dh:o0 + 4 * dh])
        slots = range(len(q))

        b = []
        for a in slots:
            lf_hi = log_f[a].astype(BF16)
            lf_lo = (log_f[a] - lf_hi.astype(F32)).astype(BF16)
            b.append(_dot(tri_ref[...], lf_hi) + _dot(tri_ref[...], lf_lo))
            b_ref[a] = b[a]

        expo, acc = [], []
        for a in slots:
            expo.append(_hgrn_level_exponents(b[a], log_f[a], b_ref.at[a], sgn_ref, chunk))
            qd_ref[a] = (q[a] * jnp.exp2(b[a])).astype(BF16)
            b_last = b_ref[a, chunk - 1:chunk, :]
            kd_ref[a] = (k[a] * jnp.exp2(b_last - b[a])).astype(BF16)
            dec_ref[a] = jnp.exp2(b_last)
            acc.append(jnp.where(lvl_ref[...] == 0, _dot_nt(q[a].astype(BF16), k[a].astype(BF16)), 0.0))
        for m in levels:
            if m >= SUBLANES:
                break
            is_query_row = sgn_ref[{4: 0, 2: 1, 1: 2}[m]] > 0.0
            for a in slots:
                w = (jnp.where(is_query_row, q[a], k[a]) * jnp.exp2(expo[a][m])).astype(BF16)
                acc[a] = jnp.where(lvl_ref[...] >= m, _dot_nt(w, w), acc[a])
        n_grp = chunk // SUBLANES
        acc = [[acc_a[SUBLANES * i:SUBLANES * (i + 1), :] for i in range(n_grp)] for acc_a in acc]
        for m in levels:
            if m < SUBLANES:
                continue
            is_query = [(SUBLANES * i) % (2 * m) >= m for i in range(n_grp)]
            for a in slots:
                roles = jnp.concatenate([(q[a] if is_query[i] else k[a])[SUBLANES * i:SUBLANES * (i + 1), :]
                                         for i in range(n_grp)], axis=0)
                w = roles * jnp.exp2(expo[a][m])
                q_m = jnp.concatenate([w[SUBLANES * i:SUBLANES * (i + 1), :]
                                       for i in range(n_grp) if is_query[i]], axis=0)
                a_m = _dot_nt(q_m.astype(BF16), w.astype(BF16))
                for n, i in enumerate([i for i in range(n_grp) if is_query[i]]):
                    rows = slice(SUBLANES * i, SUBLANES * (i + 1))
                    acc[a][i] = jnp.where(lvl_ref[rows, :] >= m,
                                          a_m[SUBLANES * n:SUBLANES * (n + 1), :], acc[a][i])
        for a in slots:
            a16_ref[a] = jnp.concatenate(acc[a], axis=0).astype(BF16)

    def back(chunks):
        for n, c in enumerate(chunks):
            r0 = row_start(c)
            for a in heads:
                o0 = a * 4 * dh
                slot = n * hps + a
                v16 = v16_ref[pl.ds(r0, chunk), a * dh:(a + 1) * dh]
                st = st_ref[a]
                oa = _dot_nt(qd_ref[slot], st.astype(BF16)) + _dot(a16_ref[slot], v16)
                st_ref[a] = st * dec_ref[slot] + _dot_tn(v16, kd_ref[slot])
                y = _rms_rows(oa, onw) * act_ref[pl.ds(r0, chunk), o0 + 2 * dh:o0 + 3 * dh]
                y_ref[0, pl.ds(r0, chunk), a * dh:(a + 1) * dh] = y.astype(BF16)

    n_iter = seq // (chunk * cpi)
    front([n for n in range(cpi)])

    def body(i, carry):
        back([(i - 1) * cpi + n for n in range(cpi)])
        front([i * cpi + n for n in range(cpi)])
        return carry

    lax.fori_loop(1, n_iter, body, 0)
    back([(n_iter - 1) * cpi + n for n in range(cpi)])


def _hgrn_mixer(x, norm_w_row, w_in, lb_raw, o_norm_w_row, layer_idx):
    bsz, seq, d = x.shape
    dh = A_HEAD_DIM
    hps = A_HEADS_PER_STEP
    n_steps = d // (dh * hps)
    chunk = A_CHUNK
    tri = jnp.asarray(np.tril(np.ones((chunk, chunk), np.float32)), BF16)
    cpi = A_CHUNKS_PER_ITER
    n_slots = hps * cpi
    assert seq % (chunk * cpi) == 0
    kern = functools.partial(_hgrn_kernel, layer_idx=layer_idx, seq=seq, chunk=chunk, hps=hps, cpi=cpi)

    def w_spec(part):
        return pl.BlockSpec((d, hps * dh), lambda b, h, part=part: (0, part * n_steps + h))

    return pl.pallas_call(
        kern,
        grid=(bsz, n_steps),
        in_specs=[pl.BlockSpec((1, seq, d), lambda b, h: (b, 0, 0)),
                  pl.BlockSpec((1, d), lambda b, h: (0, 0)),
                  w_spec(0), w_spec(1), w_spec(2), w_spec(3),
                  pl.BlockSpec((lb_raw.shape[0], hps * dh), lambda b, h: (0, h)),
                  pl.BlockSpec((1, dh), lambda b, h: (0, 0)),
                  pl.BlockSpec((chunk, chunk), lambda b, h: (0, 0))],
        out_specs=pl.BlockSpec((1, seq, hps * dh), lambda b, h: (b, 0, h)),
        out_shape=jax.ShapeDtypeStruct((bsz, seq, d), BF16),
        scratch_shapes=[pltpu.VMEM((seq, d), BF16),
                        pltpu.VMEM((seq, hps * 4 * dh), F32),
                        pltpu.VMEM((seq, hps * dh), BF16),
                        pltpu.VMEM((n_slots, chunk, dh), F32),
                        pltpu.VMEM((hps, dh, dh), F32),
                        pltpu.VMEM((chunk, chunk), jnp.int32),
                        pltpu.VMEM((3, chunk, dh), F32),
                        pltpu.VMEM((n_slots, chunk, dh), BF16),
                        pltpu.VMEM((n_slots, chunk, dh), BF16),
                        pltpu.VMEM((n_slots, chunk, chunk), BF16),
                        pltpu.VMEM((n_slots, 1, dh), F32)],
        compiler_params=_params(("arbitrary", "arbitrary")),
        name="hgrn_mixer",
    )(x, norm_w_row, w_in, w_in, w_in, w_in, lb_raw, o_norm_w_row, tri)


def _bproj_kernel(x_ref, nw_ref, w_ref, hw_ref, eseg_ref, esegt_ref, o_ref, h_ref, hn_ref, *, seq, dils):
    j = pl.program_id(1)
    n_qkv_tiles = 3 * len(dils)

    @pl.when(j == 0)
    def _():
        d = x_ref.shape[-1]
        n_planes = d // LANES
        rb = 256
        for i in range(seq // rb):
            hn = _rms_rows(x_ref[0, i * rb:(i + 1) * rb, :], nw_ref[...])
            for c in range(n_planes):
                hn_ref[c, i * rb:(i + 1) * rb, :] = hn[:, c * LANES:(c + 1) * LANES]
        for g, dil in enumerate(dils):
            sub = seq // dil
            for r in range(dil):
                for c0 in range(0, sub, rb):
                    rows = min(rb, sub - c0)
                    if dil == 1:
                        idx = pl.ds(c0, rows)
                    else:
                        idx = pl.ds(r + c0 * dil, rows, stride=dil)
                    for c in range(n_planes):
                        h_ref[g, r * sub + c0:r * sub + c0 + rows, c * LANES:(c + 1) * LANES] = (
                            hn_ref[c, idx, :].astype(BF16))

    grp = jnp.where(j < n_qkv_tiles, j // 3, 0)
    kind = jnp.where(j < n_qkv_tiles, j % 3, 3)
    pb = 512

    def tile(i):
        return _dot(h_ref[grp, i * pb:(i + 1) * pb, :], w_ref[...])

    @pl.when(kind < 2)
    def _():
        hw = hw_ref[0]
        n_heads = x_ref.shape[-1] // B_HEAD_DIM
        hi_lanes = lax.broadcasted_iota(jnp.int32, (pb, LANES), 1) < n_heads
        for i in range(seq // pb):
            y = tile(i)
            ssq = _dot((y * y).astype(BF16), eseg_ref[...])
            r = lax.rsqrt(ssq * (1.0 / B_HEAD_DIM) + NORM_EPS)
            r_hi = r.astype(BF16).astype(F32)
            r_split = jnp.where(hi_lanes, r_hi, r - r_hi).astype(BF16)
            rexp = _dot(r_split, esegt_ref[...])
            o_ref[0, i * pb:(i + 1) * pb, :] = (y * rexp * hw).astype(BF16)

    @pl.when(kind == 2)
    def _():
        for i in range(seq // pb):
            o_ref[0, i * pb:(i + 1) * pb, :] = tile(i).astype(BF16)

    @pl.when(kind == 3)
    def _():
        for i in range(seq // pb):
            o_ref[0, i * pb:(i + 1) * pb, :] = _silu(tile(i)).astype(BF16)


def _attn_in_proj(x, norm_w_row, w_bf16, head_w):
    bsz, seq, d = x.shape
    n_out = w_bf16.shape[1]
    tn = d
    dils = tuple(dil for _, dil in DILATED_GROUPS)
    n_heads = d // B_HEAD_DIM
    assert 2 * n_heads <= LANES
    eseg = np.zeros((d, LANES), np.float32)
    eseg[np.arange(d), np.arange(d) // B_HEAD_DIM] = 1.0
    eseg[np.arange(d), n_heads + np.arange(d) // B_HEAD_DIM] = 1.0
    eseg = jnp.asarray(eseg, BF16)
    kern = functools.partial(_bproj_kernel, seq=seq, dils=dils)
    return pl.pallas_call(
        kern,
        grid=(bsz, n_out // tn),
        in_specs=[pl.BlockSpec((1, seq, d), lambda b, j: (b, 0, 0)),
                  pl.BlockSpec((1, d), lambda b, j: (0, 0)),
                  pl.BlockSpec((d, tn), lambda b, j: (0, j)),
                  pl.BlockSpec((1, 1, tn), lambda b, j: (j, 0, 0)),
                  pl.BlockSpec((d, LANES), lambda b, j: (0, 0)),
                  pl.BlockSpec((LANES, d), lambda b, j: (0, 0))],
        out_specs=pl.BlockSpec((1, seq, tn), lambda b, j: (b, 0, j)),
        out_shape=jax.ShapeDtypeStruct((bsz, seq, n_out), BF16),
        scratch_shapes=[pltpu.VMEM((len(dils), seq, d), BF16),
                        pltpu.VMEM((d // LANES, seq, LANES), F32)],
        compiler_params=_params(("arbitrary", "arbitrary")),
        name="attn_in_proj",
    )(x, norm_w_row, w_bf16, head_w, eseg, eseg.T)


def _attn_kernel(slopes_ref, q0, k0, v0, q1, k1, v1, q2, k2, v2, g_ref, o_ref,
                 num_ref, den_ref, max_ref, maxsw_ref, bias_ref, s_ref, p_ref, *, seq, dils, unrolls):
    hp = pl.program_id(1)
    blk = B_BLOCK
    hd = B_HEAD_DIM
    n_heads = (LANES // hd) * pl.num_programs(1)
    qkv = ((q0, k0, v0), (q1, k1, v1), (q2, k2, v2))
    head_a = lax.broadcasted_iota(jnp.int32, (blk, LANES), 1) < hd


    def token_rows(g, tok0):
        return pl.ds(tok0, blk) if dils[g] == 1 else pl.ds(tok0, blk, stride=dils[g])

    def qk_phase(g, specs, has_prev, slot):
        q_ref, k_ref, _ = qkv[g]
        nk = 2 * blk if has_prev else blk
        for w, (row0, _) in enumerate(specs):
            k_lo = row0 - blk if has_prev else row0
            q2d = q_ref[0, pl.ds(row0, blk), :]
            zero = jnp.zeros_like(q2d)
            qq = jnp.concatenate([jnp.where(head_a, q2d, zero), jnp.where(head_a, zero, q2d)], axis=0)
            s_ref[slot, w, :, 0:nk] = _dot_nt(qq, k_ref[0, pl.ds(k_lo, nk), :])

    def softmax_phase(g, specs, has_prev, slot):
        nk = 2 * blk if has_prev else blk
        off = 0 if has_prev else blk
        for w, (_, tok0) in enumerate(specs):
            ms = []
            for a in range(2):
                sa = s_ref[slot, w, a * blk:(a + 1) * blk, 0:nk] + bias_ref[a, :, off:off + nk]
                m = jnp.max(sa, axis=-1, keepdims=True)
                p_ref[slot, w, a, :, 0:nk] = jnp.exp2(sa - m).astype(BF16)
                ms.append(m)
            idx = token_rows(g, tok0)
            max_ref[g, idx, :] = jnp.where(head_a, ms[0], ms[1])
            maxsw_ref[g, idx, :] = jnp.where(head_a, ms[1], ms[0])

    def pv_phase(g, specs, has_prev, slot):
        v_ref = qkv[g][2]
        nk = 2 * blk if has_prev else blk
        head_a_k = lax.broadcasted_iota(jnp.int32, (nk, LANES), 1) < hd
        for w, (row0, tok0) in enumerate(specs):
            k_lo = row0 - blk if has_prev else row0
            v2d = v_ref[0, pl.ds(k_lo, nk), :]
            one = jnp.ones_like(v2d)
            oa = _dot(p_ref[slot, w, 0, :, 0:nk], jnp.where(head_a_k, v2d, one))
            ob = _dot(p_ref[slot, w, 1, :, 0:nk], jnp.where(head_a_k, one, v2d))
            idx = token_rows(g, tok0)
            num_ref[g, idx, :] = jnp.where(head_a, oa, ob)
            den_ref[g, idx, :] = jnp.where(head_a, ob, oa)

    def pipelined(n_iter, make_specs, g, has_prev):
        def step(t):
            static = isinstance(t, int)
            if (not static) or 0 <= t - 2 < n_iter:
                pv_phase(g, make_specs(t - 2), has_prev, t % 2)
            if (not static) or 0 <= t - 1 < n_iter:
                softmax_phase(g, make_specs(t - 1), has_prev, (t - 1) % 2)
            if (not static) or 0 <= t < n_iter:
                qk_phase(g, make_specs(t), has_prev, t % 2)

        for t in range(2):
            step(t)
        if n_iter > 2:
            def body(t, carry):
                step(t)
                return carry

            lax.fori_loop(2, n_iter, body, 0)
        for t in range(max(n_iter, 2), n_iter + 2):
            step(t)

    qi = lax.broadcasted_iota(jnp.int32, (blk, 2 * blk), 0)
    kj = lax.broadcasted_iota(jnp.int32, (blk, 2 * blk), 1)
    dist = blk + qi - kj
    valid = (dist >= 0) & (dist <= blk)

    for g, dil in enumerate(dils):
        sub = seq // dil
        n_blk = sub // blk
        bias_base = jnp.where(valid, -(dist * dil).astype(F32), NEG_BIG)
        for a in range(2):
            slope = slopes_ref[g * n_heads + 2 * hp + a]
            bias_ref[a] = (slope * LOG2E) * bias_base

        u1, u2 = unrolls[g]

        def aligned(row):
            return row if isinstance(row, int) else pl.multiple_of(row, blk)

        def first_specs(i, sub=sub, u1=u1):
            return [(aligned((i * u1 + w) * sub), i * u1 + w) for w in range(u1)]

        def rest_specs(i, dil=dil, sub=sub, n_blk=n_blk, u2=u2):
            specs = []
            for w in range(u2):
                idx = i * u2 + w
                r = idx // (n_blk - 1)
                b = idx % (n_blk - 1) + 1
                specs.append((aligned(r * sub + b * blk), b * blk * dil + r))
            return specs

        pipelined(dil // u1, first_specs, g, False)
        n_rest = dil * (n_blk - 1)
        if n_rest:
            pipelined(n_rest // u2, rest_specs, g, True)

    rb = 256
    n_g = len(dils)

    def merge(i, carry):
        r0 = pl.multiple_of(i * rb, rb)
        ms = [max_ref[g, pl.ds(r0, rb), :] for g in range(n_g)]
        mx = ms[0]
        for g in range(1, n_g):
            mx = jnp.maximum(mx, ms[g])
        ms_sw = [maxsw_ref[g, pl.ds(r0, rb), :] for g in range(n_g)]
        mx_sw = ms_sw[0]
        for g in range(1, n_g):
            mx_sw = jnp.maximum(mx_sw, ms_sw[g])
        num = jnp.zeros((rb, LANES), F32)
        den = jnp.zeros((rb, LANES), F32)
        for g in range(n_g):
            num = num + jnp.exp2(ms[g] - mx) * num_ref[g, pl.ds(r0, rb), :]
            den = den + jnp.exp2(ms_sw[g] - mx_sw) * den_ref[g, pl.ds(r0, rb), :]
        den = pltpu.roll(den, hd, axis=1)
        gate = g_ref[0, pl.ds(r0, rb), :].astype(F32)
        o_ref[0, pl.ds(r0, rb), :] = (num / den * gate).astype(BF16)
        return carry

    lax.fori_loop(0, seq // rb, merge, 0)


def _attention(proj, slopes):
    bsz, seq, n_out = proj.shape
    d = n_out // (3 * B_N_GROUPS + 1)
    n_pairs = d // LANES
    dils = tuple(dil for _, dil in DILATED_GROUPS)
    unrolls = ((1, 5), (4, 6), (4, 1))
    for (u1, u2), dil in zip(unrolls, dils):
        n_rest = dil * (seq // dil // B_BLOCK - 1)
        assert dil % u1 == 0 and n_rest % u2 == 0

    def col_spec(tile):
        return pl.BlockSpec((1, seq, LANES), lambda b, h, s, tile=tile: (b, 0, tile * n_pairs + h))

    kern = functools.partial(_attn_kernel, seq=seq, dils=dils, unrolls=unrolls)
    grid_spec = pltpu.PrefetchScalarGridSpec(
        num_scalar_prefetch=1,
        grid=(bsz, n_pairs),
        in_specs=[col_spec(t) for t in range(3 * B_N_GROUPS + 1)],
        out_specs=pl.BlockSpec((1, seq, LANES), lambda b, h, s: (b, 0, h)),
        scratch_shapes=[pltpu.VMEM((B_N_GROUPS, seq, LANES), F32)] * 4
        + [pltpu.VMEM((2, B_BLOCK, 2 * B_BLOCK), F32),
           pltpu.VMEM((2, max(max(u) for u in unrolls), 2 * B_BLOCK, 2 * B_BLOCK), F32),
           pltpu.VMEM((2, max(max(u) for u in unrolls), 2, B_BLOCK, 2 * B_BLOCK), BF16)],
    )
    return pl.pallas_call(
        kern,
        grid_spec=grid_spec,
        out_shape=jax.ShapeDtypeStruct((bsz, seq, d), BF16),
        compiler_params=_params(("arbitrary", "arbitrary")),
        name="dilated_attention",
    )(slopes, *([proj] * (3 * B_N_GROUPS + 1)))


def _gmlp_kernel(x_ref, nw_ref, w_ref, vw_ref, vb_ref, ws_ref, bias_ref, wout_ref, o_ref,
                 h_ref, u_ref, v_ref, g_ref, y_ref, *, tm):
    d = x_ref.shape[-1]
    cg = d // C_GROUPS
    rb = 256
    for i in range(tm // rb):
        h_ref[i * rb:(i + 1) * rb, :] = _rms_rows(x_ref[0, i * rb:(i + 1) * rb, :], nw_ref[...]).astype(BF16)
    for i in range(tm // rb):
        rows = slice(i * rb, (i + 1) * rb)
        hh = h_ref[rows, :]
        u_ref[rows, :] = _gelu_tanh(_dot(hh, w_ref[:, 0:d]))
        vv = _gelu_tanh(_dot(hh, w_ref[:, d:2 * d]))
        mu = jnp.mean(vv, axis=-1, keepdims=True)
        vc = vv - mu
        var = jnp.mean(vc * vc, axis=-1, keepdims=True)
        v_ref[rows, :] = (vc * lax.rsqrt(var + NORM_EPS) * vw_ref[...] + vb_ref[...]).astype(BF16)
        g_ref[rows, :] = _silu(_dot(hh, w_ref[:, 2 * d:3 * d]))
    ti = lax.broadcasted_iota(jnp.int32, (C_CHUNK, C_CHUNK), 0)
    si = lax.broadcasted_iota(jnp.int32, (C_CHUNK, C_CHUNK), 1)
    n_ch = tm // C_CHUNK
    for grp in range(C_GROUPS):
        cols = slice(grp * cg, (grp + 1) * cg)
        wc = jnp.where(si <= ti, ws_ref[grp], 0.0).astype(BF16)
        rhs = jnp.concatenate([v_ref[n * C_CHUNK:(n + 1) * C_CHUNK, cols] for n in range(n_ch)], axis=1)
        s = _dot(wc, rhs)
        bias = bias_ref[:, cols]
        for n in range(n_ch):
            rows = slice(n * C_CHUNK, (n + 1) * C_CHUNK)
            sn = s[:, n * cg:(n + 1) * cg] + bias
            y_ref[rows, cols] = (u_ref[rows, cols] * sn * g_ref[rows, cols]).astype(BF16)
    for i in range(tm // rb):
        rows = slice(i * rb, (i + 1) * rb)
        o_ref[0, rows, :] = x_ref[0, rows, :] + _dot(y_ref[rows, :], wout_ref[...])


def _gmlp_mixer(x, norm_w_row, w_bf16, vw_row, vb_row, w_s, bias_full, w_out_bf16, tm=512):
    bsz, seq, d = x.shape
    kern = functools.partial(_gmlp_kernel, tm=tm)
    return pl.pallas_call(
        kern,
        grid=(bsz, seq // tm),
        in_specs=[pl.BlockSpec((1, tm, d), lambda b, i: (b, i, 0)),
                  pl.BlockSpec((1, d), lambda b, i: (0, 0)),
                  pl.BlockSpec((d, 3 * d), lambda b, i: (0, 0)),
                  pl.BlockSpec((1, d), lambda b, i: (0, 0)),
                  pl.BlockSpec((1, d), lambda b, i: (0, 0)),
                  pl.BlockSpec(w_s.shape, lambda b, i: (0, 0, 0)),
                  pl.BlockSpec((C_CHUNK, d), lambda b, i: (0, 0)),
                  pl.BlockSpec((d, d), lambda b, i: (0, 0))],
        out_specs=pl.BlockSpec((1, tm, d), lambda b, i: (b, i, 0)),
        out_shape=jax.ShapeDtypeStruct((bsz, seq, d), F32),
        scratch_shapes=[pltpu.VMEM((tm, d), BF16),
                        pltpu.VMEM((tm, d), F32),
                        pltpu.VMEM((tm, d), BF16),
                        pltpu.VMEM((tm, d), F32),
                        pltpu.VMEM((tm, d), BF16)],
        compiler_params=_params(("arbitrary", "arbitrary")),
        name="gmlp_mixer",
    )(x, norm_w_row, w_bf16, vw_row, vb_row, w_s, bias_full, w_out_bf16)


def kernel(x, norm_w, a_w_in, a_lower_bounds, a_o_norm_w, a_w_out, b_w_in, b_q_norm_w, b_k_norm_w, b_w_out,
           c_w_in, c_v_norm_w, c_v_norm_b, c_w_s, c_b_s, c_w_out):
    bsz, seq, d = x.shape
    depth = norm_w.shape[0]
    n_mixers = 3
    assert seq == DILATED_GROUPS[-1][0] and seq % A_CHUNK == 0 and d % LANES == 0

    def residual(y, w_out, xin):
        out = _out_proj(y.reshape(bsz * seq, d), w_out.astype(BF16), xin.reshape(bsz * seq, d))
        return out.reshape(bsz, seq, d)

    for layer in range(depth):
        kind, idx = layer % n_mixers, layer // n_mixers
        nw = norm_w[layer][None, :]
        if kind == 0:
            y = _hgrn_mixer(x, nw, a_w_in[idx].astype(BF16), a_lower_bounds, a_o_norm_w[idx][None, :], idx)
            x = residual(y, a_w_out[idx], x)
        elif kind == 1:
            n_heads = d // B_HEAD_DIM
            n_total = B_N_GROUPS * n_heads
            slopes = jnp.exp2(-ALIBI_MAX_EXP * jnp.arange(1, n_total + 1, dtype=F32) / n_total)
            rows = []
            for g in range(B_N_GROUPS):
                rows.append(jnp.tile(b_q_norm_w[idx, g], n_heads) * (B_HEAD_DIM ** -0.5 * LOG2E))
                rows.append(jnp.tile(b_k_norm_w[idx, g], n_heads))
                rows.append(jnp.ones((d,), F32))
            rows.append(jnp.ones((d,), F32))
            head_w = jnp.stack(rows)[:, None, :]
            proj = _attn_in_proj(x, nw, b_w_in[idx].astype(BF16), head_w)
            y = _attention(proj, slopes)
            x = residual(y, b_w_out[idx], x)
        else:
            bias_full = jnp.repeat(c_b_s[idx].T, d // C_GROUPS, axis=1)
            x = _gmlp_mixer(x, nw, c_w_in[idx].astype(BF16), c_v_norm_w[idx][None, :],
                            c_v_norm_b[idx][None, :], c_w_s[idx], bias_full, c_w_out[idx].astype(BF16))
    return x
```

```python
import functools

import numpy as np
import jax
import jax.numpy as jnp
from jax import lax
from jax.experimental import pallas as pl
from jax.experimental.pallas import tpu as pltpu

F32 = jnp.float32
BF16 = jnp.bfloat16

NORM_EPS = 1e-6
NEG_BIG = -1e30
LB_FLOOR = 1e-30
LOG2E = float(np.log2(np.e))

LANES = 128
SUBLANES = 8
VMEM_LIMIT_BYTES = 56 * 1024 * 1024

A_HEAD_DIM = 128
A_CHUNK = 128
A_CHUNKS_PER_ITER = 2
A_HEADS_PER_STEP = 4
B_HEAD_DIM = 64
DILATED_GROUPS = ((128, 1), (512, 4), (2048, 16))
B_N_GROUPS = len(DILATED_GROUPS)
B_BLOCK = 128
ALIBI_MAX_EXP = 8.0
C_CHUNK = 128
C_GROUPS = 8


def _silu(x):
    hx = 0.5 * x
    return hx + hx * jnp.tanh(hx)


def _gelu_tanh(x):
    c = np.float32(np.sqrt(2.0 / np.pi))
    return 0.5 * x * (1.0 + jnp.tanh(c * (x + 0.044715 * (x * x * x))))


def _rms_rows(xf, w):
    ms = jnp.mean(xf * xf, axis=-1, keepdims=True)
    return xf * lax.rsqrt(ms + NORM_EPS) * w


def _dot(a, b):
    return jnp.dot(a, b, preferred_element_type=F32)


def _dot_nt(a, b):
    return lax.dot_general(a, b, (((1,), (1,)), ((), ())), preferred_element_type=F32)


def _dot_tn(a, b):
    return lax.dot_general(a, b, (((0,), (0,)), ((), ())), preferred_element_type=F32)


def _params(sem):
    return pltpu.CompilerParams(dimension_semantics=sem, vmem_limit_bytes=VMEM_LIMIT_BYTES)


def _out_proj_kernel(y_ref, w_ref, x_ref, o_ref):
    o_ref[...] = x_ref[...] + _dot(y_ref[...], w_ref[...])


def _out_proj(y2d, w_bf16, x2d, tm=1024):
    m, k = y2d.shape
    n = w_bf16.shape[1]
    return pl.pallas_call(
        _out_proj_kernel,
        grid=(m // tm,),
        in_specs=[pl.BlockSpec((tm, k), lambda i: (i, 0)),
                  pl.BlockSpec((k, n), lambda i: (0, 0)),
                  pl.BlockSpec((tm, n), lambda i: (i, 0))],
        out_specs=pl.BlockSpec((tm, n), lambda i: (i, 0)),
        out_shape=jax.ShapeDtypeStruct((m, n), F32),
        compiler_params=_params(("arbitrary",)),
        name="out_proj",
    )(y2d, w_bf16, x2d)


def _out_proj_pairs_kernel(y_ref, w_ref, x_ref, o_ref):
    y = jnp.concatenate([y_ref[hp] for hp in range(y_ref.shape[0])], axis=1)
    o_ref[...] = x_ref[...] + _dot(y, w_ref[...])


def _out_proj_pairs(y_pairs, w_bf16, x, tm=1024):
    bsz, n_pairs, seq, lanes = y_pairs.shape
    d = n_pairs * lanes
    n = w_bf16.shape[1]
    return pl.pallas_call(
        _out_proj_pairs_kernel,
        grid=(bsz, seq // tm),
        in_specs=[pl.BlockSpec((None, n_pairs, tm, lanes), lambda b, i: (b, 0, i, 0)),
                  pl.BlockSpec((d, n), lambda b, i: (0, 0)),
                  pl.BlockSpec((None, tm, n), lambda b, i: (b, i, 0))],
        out_specs=pl.BlockSpec((None, tm, n), lambda b, i: (b, i, 0)),
        out_shape=jax.ShapeDtypeStruct((bsz, seq, n), F32),
        compiler_params=_params(("arbitrary", "arbitrary")),
        name="out_proj_pairs",
    )(y_pairs, w_bf16, x)


def _hgrn_levels(chunk):
    return [1 << j for j in range(int(np.log2(chunk)))]


def _hgrn_level_exponents(b, log_f, b_view, sgn_ref, chunk):
    out = {}
    n_vregs = chunk // SUBLANES
    width = b.shape[-1]

    def row(r, n):
        one_group = jnp.broadcast_to(b_view[pl.ds(r, 1), :], (SUBLANES, width))
        return jnp.concatenate([one_group] * (n // SUBLANES), axis=0)

    for m in _hgrn_levels(chunk):
        if m >= SUBLANES:
            parts = []
            for j in range(chunk // (2 * m)):
                base = j * 2 * m
                ref_row = row(base + m - 1, m)
                parts.append(ref_row - b[base:base + m, :])
                parts.append(b[base + m:base + 2 * m, :] - ref_row)
            out[m] = jnp.concatenate(parts, axis=0)
        elif m == 4:
            ref = jnp.concatenate([row(SUBLANES * i + 3, SUBLANES) for i in range(n_vregs)], axis=0)
            out[m] = (b - ref) * sgn_ref[0]
        elif m == 2:
            lo = jnp.concatenate([row(SUBLANES * i + 1, SUBLANES) for i in range(n_vregs)], axis=0)
            hi = jnp.concatenate([row(SUBLANES * i + 5, SUBLANES) for i in range(n_vregs)], axis=0)
            ref = jnp.where(sgn_ref[0] > 0.0, hi, lo)
            out[m] = (b - ref) * sgn_ref[1]
        else:
            out[m] = log_f * sgn_ref[2]
    return out


def _hgrn_kernel(x_ref, nw_ref, wq_ref, wf_ref, wi_ref, wg_ref, lbraw_ref, onw_ref, tri_ref, y_ref,
                 h_ref, act_ref, v16_ref, b_ref, st_ref, lvl_ref, sgn_ref, qd_ref, kd_ref, a16_ref, dec_ref,
                 *, layer_idx, seq, chunk, hps, cpi):
    step = pl.program_id(1)
    dh = A_HEAD_DIM
    rb = 256

    @pl.when(step == 0)
    def _():
        for i in range(seq // rb):
            xs = x_ref[0, i * rb:(i + 1) * rb, :]
            h_ref[i * rb:(i + 1) * rb, :] = _rms_rows(xs, nw_ref[...]).astype(BF16)

    raw = lbraw_ref[...]
    n_layers = raw.shape[0]
    mx = raw[0:1, :]
    for i in range(1, n_layers):
        mx = jnp.maximum(mx, raw[i:i + 1, :])
    es = [jnp.exp(raw[i:i + 1, :] - mx) for i in range(n_layers)]
    z = es[0]
    for i in range(1, n_layers):
        z = z + es[i]
    soft = [e / z for e in es]
    csum = soft[0]
    for i in range(1, layer_idx + 1):
        csum = csum + soft[i]
    lb = csum - soft[0]
    log_lb = jnp.log(jnp.maximum(lb, LB_FLOOR))
    log1m_lb = jnp.log1p(-lb)
    one_m_lb = 1.0 - lb
    heads = range(hps)

    pb = 512
    for i in range(seq // pb):
        rows = slice(i * pb, (i + 1) * pb)
        hh = h_ref[rows, :]
        res_q, res_f, res_i, res_g = (_dot(hh, w[...]) for w in (wq_ref, wf_ref, wi_ref, wg_ref))
        v16_ref[rows, :] = res_i.astype(BF16)
        for a in heads:
            o0 = a * 4 * dh
            lbs = slice(a * dh, (a + 1) * dh)
            pf = res_f[:, lbs]
            log_sig = jnp.minimum(pf, 0.0) - jnp.log(1.0 + jnp.exp(-jnp.abs(pf)))
            bb = log1m_lb[:, lbs] + log_sig
            ll = log_lb[:, lbs]
            log_f = jnp.maximum(ll, bb) + jnp.log(1.0 + jnp.exp(-jnp.abs(ll - bb)))
            act_ref[rows, o0:o0 + dh] = _silu(res_q[:, lbs])
            act_ref[rows, o0 + dh:o0 + 2 * dh] = one_m_lb[:, lbs] * jnp.exp(log_sig - pf)
            act_ref[rows, o0 + 2 * dh:o0 + 3 * dh] = _silu(res_g[:, lbs])
            act_ref[rows, o0 + 3 * dh:o0 + 4 * dh] = log_f * LOG2E

    ti = lax.broadcasted_iota(jnp.int32, (chunk, chunk), 0)
    si = lax.broadcasted_iota(jnp.int32, (chunk, chunk), 1)
    lvl_ref[...] = jnp.where(si < ti, ti ^ si, jnp.where(si == ti, 0, -1))
    tr = lax.broadcasted_iota(jnp.int32, (chunk, dh), 0)
    sgn_ref[0] = jnp.where((tr & 4) != 0, 1.0, -1.0)
    sgn_ref[1] = jnp.where((tr & 2) != 0, 1.0, -1.0)
    sgn_ref[2] = jnp.where((tr & 1) != 0, 1.0, 0.0)
    st_ref[...] = jnp.zeros_like(st_ref)
    onw = onw_ref[...]
    levels = _hgrn_levels(chunk)

    def row_start(c):
        return c * chunk if isinstance(c, int) else pl.multiple_of(c * chunk, chunk)

    def front(chunks):
        q, k, log_f = [], [], []
        for c in chunks:
            for a in heads:
                o0 = a * 4 * dh
                q.append(act_ref[pl.ds(row_start(c), chunk), o0:o0 + dh])
                k.append(act_ref[pl.ds(row_start(c), chunk), o0 + dh:o0 + 2 * dh])
                log_f.append(act_ref[pl.ds(row_start(c), chunk), o0 + 3 * dh:o0 + 4 * dh])
        slots = range(len(q))

        b = []
        for a in slots:
            lf_hi = log_f[a].astype(BF16)
            lf_lo = (log_f[a] - lf_hi.astype(F32)).astype(BF16)
            b.append(_dot(tri_ref[...], lf_hi) + _dot(tri_ref[...], lf_lo))
            b_ref[a] = b[a]

        expo, acc = [], []
        for a in slots:
            expo.append(_hgrn_level_exponents(b[a], log_f[a], b_ref.at[a], sgn_ref, chunk))
            qd_ref[a] = (q[a] * jnp.exp2(b[a])).astype(BF16)
            b_last = b_ref[a, chunk - 1:chunk, :]
            kd_ref[a] = (k[a] * jnp.exp2(b_last - b[a])).astype(BF16)
            dec_ref[a] = jnp.exp2(b_last)
            acc.append(jnp.where(lvl_ref[...] == 0, _dot_nt(q[a].astype(BF16), k[a].astype(BF16)), 0.0))
        for m in levels:
            if m >= SUBLANES:
                break
            is_query_row = sgn_ref[{4: 0, 2: 1, 1: 2}[m]] > 0.0
            for a in slots:
                w = (jnp.where(is_query_row, q[a], k[a]) * jnp.exp2(expo[a][m])).astype(BF16)
                acc[a] = jnp.where(lvl_ref[...] >= m, _dot_nt(w, w), acc[a])
        n_grp = chunk // SUBLANES
        acc = [[acc_a[SUBLANES * i:SUBLANES * (i + 1), :] for i in range(n_grp)] for acc_a in acc]
        for m in levels:
            if m < SUBLANES:
                continue
            is_query = [(SUBLANES * i) % (2 * m) >= m for i in range(n_grp)]
            for a in slots:
                roles = jnp.concatenate([(q[a] if is_query[i] else k[a])[SUBLANES * i:SUBLANES * (i + 1), :]
                                         for i in range(n_grp)], axis=0)
                w = roles * jnp.exp2(expo[a][m])
                q_m = jnp.concatenate([w[SUBLANES * i:SUBLANES * (i + 1), :]
                                       for i in range(n_grp) if is_query[i]], axis=0)
                a_m = _dot_nt(q_m.astype(BF16), w.astype(BF16))
                for n, i in enumerate([i for i in range(n_grp) if is_query[i]]):
                    rows = slice(SUBLANES * i, SUBLANES * (i + 1))
                    acc[a][i] = jnp.where(lvl_ref[rows, :] >= m,
                                          a_m[SUBLANES * n:SUBLANES * (n + 1), :], acc[a][i])
        for a in slots:
            a16_ref[a] = jnp.concatenate(acc[a], axis=0).astype(BF16)

    def back(chunks):
        for n, c in enumerate(chunks):
            r0 = row_start(c)
            for a in heads:
                o0 = a * 4 * dh
                slot = n * hps + a
                v16 = v16_ref[pl.ds(r0, chunk), a * dh:(a + 1) * dh]
                st = st_ref[a]
                oa = _dot_nt(qd_ref[slot], st.astype(BF16)) + _dot(a16_ref[slot], v16)
                st_ref[a] = st * dec_ref[slot] + _dot_tn(v16, kd_ref[slot])
                y = _rms_rows(oa, onw) * act_ref[pl.ds(r0, chunk), o0 + 2 * dh:o0 + 3 * dh]
                y_ref[0, pl.ds(r0, chunk), a * dh:(a + 1) * dh] = y.astype(BF16)

    n_iter = seq // (chunk * cpi)
    front([n for n in range(cpi)])

    def body(i, carry):
        back([(i - 1) * cpi + n for n in range(cpi)])
        front([i * cpi + n for n in range(cpi)])
        return carry

    lax.fori_loop(1, n_iter, body, 0)
    back([(n_iter - 1) * cpi + n for n in range(cpi)])


def _hgrn_mixer(x, norm_w_row, w_in, lb_raw, o_norm_w_row, layer_idx):
    bsz, seq, d = x.shape
    dh = A_HEAD_DIM
    hps = A_HEADS_PER_STEP
    n_steps = d // (dh * hps)
    chunk = A_CHUNK
    tri = jnp.asarray(np.tril(np.ones((chunk, chunk), np.float32)), BF16)
    cpi = A_CHUNKS_PER_ITER
    n_slots = hps * cpi
    assert seq % (chunk * cpi) == 0
    kern = functools.partial(_hgrn_kernel, layer_idx=layer_idx, seq=seq, chunk=chunk, hps=hps, cpi=cpi)

    def w_spec(part):
        return pl.BlockSpec((d, hps * dh), lambda b, h, part=part: (0, part * n_steps + h))

    return pl.pallas_call(
        kern,
        grid=(bsz, n_steps),
        in_specs=[pl.BlockSpec((1, seq, d), lambda b, h: (b, 0, 0)),
                  pl.BlockSpec((1, d), lambda b, h: (0, 0)),
                  w_spec(0), w_spec(1), w_spec(2), w_spec(3),
                  pl.BlockSpec((lb_raw.shape[0], hps * dh), lambda b, h: (0, h)),
                  pl.BlockSpec((1, dh), lambda b, h: (0, 0)),
                  pl.BlockSpec((chunk, chunk), lambda b, h: (0, 0))],
        out_specs=pl.BlockSpec((1, seq, hps * dh), lambda b, h: (b, 0, h)),
        out_shape=jax.ShapeDtypeStruct((bsz, seq, d), BF16),
        scratch_shapes=[pltpu.VMEM((seq, d), BF16),
                        pltpu.VMEM((seq, hps * 4 * dh), F32),
                        pltpu.VMEM((seq, hps * dh), BF16),
                        pltpu.VMEM((n_slots, chunk, dh), F32),
                        pltpu.VMEM((hps, dh, dh), F32),
                        pltpu.VMEM((chunk, chunk), jnp.int32),
                        pltpu.VMEM((3, chunk, dh), F32),
                        pltpu.VMEM((n_slots, chunk, dh), BF16),
                        pltpu.VMEM((n_slots, chunk, dh), BF16),
                        pltpu.VMEM((n_slots, chunk, chunk), BF16),
                        pltpu.VMEM((n_slots, 1, dh), F32)],
        compiler_params=_params(("arbitrary", "arbitrary")),
        name="hgrn_mixer",
    )(x, norm_w_row, w_in, w_in, w_in, w_in, lb_raw, o_norm_w_row, tri)


def _bproj_kernel(x_ref, nw_ref, w_ref, hw_ref, eseg_ref, esegt_ref, o_ref, h_ref, hn_ref, *, seq, dils):
    j = pl.program_id(1)
    n_qkv_tiles = 3 * len(dils)

    @pl.when(j == 0)
    def _():
        d = x_ref.shape[-1]
        n_planes = d // LANES
        rb = 256
        for i in range(seq // rb):
            hn = _rms_rows(x_ref[0, i * rb:(i + 1) * rb, :], nw_ref[...])
            for c in range(n_planes):
                hn_ref[c, i * rb:(i + 1) * rb, :] = hn[:, c * LANES:(c + 1) * LANES]
        for g, dil in enumerate(dils):
            sub = seq // dil
            for r in range(dil):
                for c0 in range(0, sub, rb):
                    rows = min(rb, sub - c0)
                    if dil == 1:
                        idx = pl.ds(c0, rows)
                    else:
                        idx = pl.ds(r + c0 * dil, rows, stride=dil)
                    for c in range(n_planes):
                        h_ref[g, r * sub + c0:r * sub + c0 + rows, c * LANES:(c + 1) * LANES] = (
                            hn_ref[c, idx, :].astype(BF16))

    grp = jnp.where(j < n_qkv_tiles, j // 3, 0)
    kind = jnp.where(j < n_qkv_tiles, j % 3, 3)
    pb = 512

    def tile(i):
        return _dot(h_ref[grp, i * pb:(i + 1) * pb, :], w_ref[...])

    def store(i, val):
        for hp in range(val.shape[-1] // LANES):
            o_ref[hp, i * pb:(i + 1) * pb, :] = val[:, hp * LANES:(hp + 1) * LANES].astype(BF16)

    @pl.when(kind < 2)
    def _():
        hw = hw_ref[0]
        n_heads = x_ref.shape[-1] // B_HEAD_DIM
        hi_lanes = lax.broadcasted_iota(jnp.int32, (pb, LANES), 1) < n_heads
        for i in range(seq // pb):
            y = tile(i)
            ssq = _dot((y * y).astype(BF16), eseg_ref[...])
            r = lax.rsqrt(ssq * (1.0 / B_HEAD_DIM) + NORM_EPS)
            r_hi = r.astype(BF16).astype(F32)
            r_split = jnp.where(hi_lanes, r_hi, r - r_hi).astype(BF16)
            rexp = _dot(r_split, esegt_ref[...])
            store(i, y * rexp * hw)

    @pl.when(kind == 2)
    def _():
        for i in range(seq // pb):
            store(i, tile(i))

    @pl.when(kind == 3)
    def _():
        for i in range(seq // pb):
            store(i, _silu(tile(i)))


def _attn_in_proj(x, norm_w_row, w_bf16, head_w):
    bsz, seq, d = x.shape
    n_out = w_bf16.shape[1]
    tn = d
    dils = tuple(dil for _, dil in DILATED_GROUPS)
    n_heads = d // B_HEAD_DIM
    assert 2 * n_heads <= LANES
    eseg = np.zeros((d, LANES), np.float32)
    eseg[np.arange(d), np.arange(d) // B_HEAD_DIM] = 1.0
    eseg[np.arange(d), n_heads + np.arange(d) // B_HEAD_DIM] = 1.0
    eseg = jnp.asarray(eseg, BF16)
    kern = functools.partial(_bproj_kernel, seq=seq, dils=dils)
    return pl.pallas_call(
        kern,
        grid=(bsz, n_out // tn),
        in_specs=[pl.BlockSpec((1, seq, d), lambda b, j: (b, 0, 0)),
                  pl.BlockSpec((1, d), lambda b, j: (0, 0)),
                  pl.BlockSpec((d, tn), lambda b, j: (0, j)),
                  pl.BlockSpec((1, 1, tn), lambda b, j: (j, 0, 0)),
                  pl.BlockSpec((d, LANES), lambda b, j: (0, 0)),
                  pl.BlockSpec((LANES, d), lambda b, j: (0, 0))],
        out_specs=pl.BlockSpec((None, None, tn // LANES, seq, LANES), lambda b, j: (b, j, 0, 0, 0)),
        out_shape=jax.ShapeDtypeStruct((bsz, n_out // tn, tn // LANES, seq, LANES), BF16),
        scratch_shapes=[pltpu.VMEM((len(dils), seq, d), BF16),
                        pltpu.VMEM((d // LANES, seq, LANES), F32)],
        compiler_params=_params(("arbitrary", "arbitrary")),
        name="attn_in_proj",
    )(x, norm_w_row, w_bf16, head_w, eseg, eseg.T)


def _attn_kernel(slopes_ref, q0, k0, v0, q1, k1, v1, q2, k2, v2, g_ref, o_ref,
                 num_ref, den_ref, max_ref, maxsw_ref, bias_ref, s_ref, p_ref, *, seq, dils, unrolls):
    hp = pl.program_id(1)
    blk = B_BLOCK
    hd = B_HEAD_DIM
    n_heads = (LANES // hd) * pl.num_programs(1)
    qkv = ((q0, k0, v0), (q1, k1, v1), (q2, k2, v2))
    head_a = lax.broadcasted_iota(jnp.int32, (blk, LANES), 1) < hd


    def token_rows(g, tok0):
        return pl.ds(tok0, blk) if dils[g] == 1 else pl.ds(tok0, blk, stride=dils[g])

    def qk_phase(g, specs, has_prev, slot):
        q_ref, k_ref, _ = qkv[g]
        nk = 2 * blk if has_prev else blk
        for w, (row0, _) in enumerate(specs):
            k_lo = row0 - blk if has_prev else row0
            q2d = q_ref[pl.ds(row0, blk), :]
            zero = jnp.zeros_like(q2d)
            qq = jnp.concatenate([jnp.where(head_a, q2d, zero), jnp.where(head_a, zero, q2d)], axis=0)
            s_ref[slot, w, :, 0:nk] = _dot_nt(qq, k_ref[pl.ds(k_lo, nk), :])

    def softmax_phase(g, specs, has_prev, slot):
        nk = 2 * blk if has_prev else blk
        off = 0 if has_prev else blk
        for w, (_, tok0) in enumerate(specs):
            ms = []
            for a in range(2):
                sa = s_ref[slot, w, a * blk:(a + 1) * blk, 0:nk] + bias_ref[a, :, off:off + nk]
                m = jnp.max(sa, axis=-1, keepdims=True)
                p_ref[slot, w, a, :, 0:nk] = jnp.exp2(sa - m).astype(BF16)
                ms.append(m)
            idx = token_rows(g, tok0)
            max_ref[g, idx, :] = jnp.where(head_a, ms[0], ms[1])
            maxsw_ref[g, idx, :] = jnp.where(head_a, ms[1], ms[0])

    def pv_phase(g, specs, has_prev, slot):
        v_ref = qkv[g][2]
        nk = 2 * blk if has_prev else blk
        head_a_k = lax.broadcasted_iota(jnp.int32, (nk, LANES), 1) < hd
        for w, (row0, tok0) in enumerate(specs):
            k_lo = row0 - blk if has_prev else row0
            v2d = v_ref[pl.ds(k_lo, nk), :]
            one = jnp.ones_like(v2d)
            oa = _dot(p_ref[slot, w, 0, :, 0:nk], jnp.where(head_a_k, v2d, one))
            ob = _dot(p_ref[slot, w, 1, :, 0:nk], jnp.where(head_a_k, one, v2d))
            idx = token_rows(g, tok0)
            num_ref[g, idx, :] = jnp.where(head_a, oa, ob)
            den_ref[g, idx, :] = jnp.where(head_a, ob, oa)

    def pipelined(n_iter, make_specs, g, has_prev):
        def step(t):
            static = isinstance(t, int)
            if (not static) or 0 <= t - 2 < n_iter:
                pv_phase(g, make_specs(t - 2), has_prev, t % 2)
            if (not static) or 0 <= t - 1 < n_iter:
                softmax_phase(g, make_specs(t - 1), has_prev, (t - 1) % 2)
            if (not static) or 0 <= t < n_iter:
                qk_phase(g, make_specs(t), has_prev, t % 2)

        for t in range(2):
            step(t)
        if n_iter > 2:
            def body(t, carry):
                step(t)
                return carry

            lax.fori_loop(2, n_iter, body, 0)
        for t in range(max(n_iter, 2), n_iter + 2):
            step(t)

    qi = lax.broadcasted_iota(jnp.int32, (blk, 2 * blk), 0)
    kj = lax.broadcasted_iota(jnp.int32, (blk, 2 * blk), 1)
    dist = blk + qi - kj
    valid = (dist >= 0) & (dist <= blk)

    for g, dil in enumerate(dils):
        sub = seq // dil
        n_blk = sub // blk
        bias_base = jnp.where(valid, -(dist * dil).astype(F32), NEG_BIG)
        for a in range(2):
            slope = slopes_ref[g * n_heads + 2 * hp + a]
            bias_ref[a] = (slope * LOG2E) * bias_base

        u1, u2 = unrolls[g]

        def aligned(row):
            return row if isinstance(row, int) else pl.multiple_of(row, blk)

        def first_specs(i, sub=sub, u1=u1):
            return [(aligned((i * u1 + w) * sub), i * u1 + w) for w in range(u1)]

        def rest_specs(i, dil=dil, sub=sub, n_blk=n_blk, u2=u2):
            specs = []
            for w in range(u2):
                idx = i * u2 + w
                r = idx // (n_blk - 1)
                b = idx % (n_blk - 1) + 1
                specs.append((aligned(r * sub + b * blk), b * blk * dil + r))
            return specs

        pipelined(dil // u1, first_specs, g, False)
        n_rest = dil * (n_blk - 1)
        if n_rest:
            pipelined(n_rest // u2, rest_specs, g, True)

    rb = 256
    n_g = len(dils)

    def merge(i, carry):
        r0 = pl.multiple_of(i * rb, rb)
        ms = [max_ref[g, pl.ds(r0, rb), :] for g in range(n_g)]
        mx = ms[0]
        for g in range(1, n_g):
            mx = jnp.maximum(mx, ms[g])
        ms_sw = [maxsw_ref[g, pl.ds(r0, rb), :] for g in range(n_g)]
        mx_sw = ms_sw[0]
        for g in range(1, n_g):
            mx_sw = jnp.maximum(mx_sw, ms_sw[g])
        num = jnp.zeros((rb, LANES), F32)
        den = jnp.zeros((rb, LANES), F32)
        for g in range(n_g):
            num = num + jnp.exp2(ms[g] - mx) * num_ref[g, pl.ds(r0, rb), :]
            den = den + jnp.exp2(ms_sw[g] - mx_sw) * den_ref[g, pl.ds(r0, rb), :]
        den = pltpu.roll(den, hd, axis=1)
        gate = g_ref[pl.ds(r0, rb), :].astype(F32)
        o_ref[pl.ds(r0, rb), :] = (num / den * gate).astype(BF16)
        return carry

    lax.fori_loop(0, seq // rb, merge, 0)


def _attention(proj, slopes):
    bsz, n_tiles, n_pairs, seq, _ = proj.shape
    assert n_tiles == 3 * B_N_GROUPS + 1
    dils = tuple(dil for _, dil in DILATED_GROUPS)
    unrolls = ((1, 5), (4, 6), (4, 1))
    for (u1, u2), dil in zip(unrolls, dils):
        n_rest = dil * (seq // dil // B_BLOCK - 1)
        assert dil % u1 == 0 and n_rest % u2 == 0

    def col_spec(tile):
        return pl.BlockSpec((None, None, None, seq, LANES), lambda b, h, s, tile=tile: (b, tile, h, 0, 0))

    kern = functools.partial(_attn_kernel, seq=seq, dils=dils, unrolls=unrolls)
    grid_spec = pltpu.PrefetchScalarGridSpec(
        num_scalar_prefetch=1,
        grid=(bsz, n_pairs),
        in_specs=[col_spec(t) for t in range(n_tiles)],
        out_specs=pl.BlockSpec((None, None, seq, LANES), lambda b, h, s: (b, h, 0, 0)),
        scratch_shapes=[pltpu.VMEM((B_N_GROUPS, seq, LANES), F32)] * 4
        + [pltpu.VMEM((2, B_BLOCK, 2 * B_BLOCK), F32),
           pltpu.VMEM((2, max(max(u) for u in unrolls), 2 * B_BLOCK, 2 * B_BLOCK), F32),
           pltpu.VMEM((2, max(max(u) for u in unrolls), 2, B_BLOCK, 2 * B_BLOCK), BF16)],
    )
    return pl.pallas_call(
        kern,
        grid_spec=grid_spec,
        out_shape=jax.ShapeDtypeStruct((bsz, n_pairs, seq, LANES), BF16),
        compiler_params=_params(("arbitrary", "arbitrary")),
        name="dilated_attention",
    )(slopes, *([proj] * n_tiles))


def _gmlp_kernel(x_ref, nw_ref, w_ref, vw_ref, vb_ref, ws_ref, bias_ref, wout_ref, o_ref,
                 h_ref, u_ref, v_ref, g_ref, y_ref, *, tm):
    d = x_ref.shape[-1]
    cg = d // C_GROUPS
    rb = 256
    for i in range(tm // rb):
        h_ref[i * rb:(i + 1) * rb, :] = _rms_rows(x_ref[0, i * rb:(i + 1) * rb, :], nw_ref[...]).astype(BF16)
    for i in range(tm // rb):
        rows = slice(i * rb, (i + 1) * rb)
        hh = h_ref[rows, :]
        u_ref[rows, :] = _gelu_tanh(_dot(hh, w_ref[:, 0:d]))
        vv = _gelu_tanh(_dot(hh, w_ref[:, d:2 * d]))
        mu = jnp.mean(vv, axis=-1, keepdims=True)
        vc = vv - mu
        var = jnp.mean(vc * vc, axis=-1, keepdims=True)
        v_ref[rows, :] = (vc * lax.rsqrt(var + NORM_EPS) * vw_ref[...] + vb_ref[...]).astype(BF16)
        g_ref[rows, :] = _silu(_dot(hh, w_ref[:, 2 * d:3 * d]))
    ti = lax.broadcasted_iota(jnp.int32, (C_CHUNK, C_CHUNK), 0)
    si = lax.broadcasted_iota(jnp.int32, (C_CHUNK, C_CHUNK), 1)
    n_ch = tm // C_CHUNK
    for grp in range(C_GROUPS):
        cols = slice(grp * cg, (grp + 1) * cg)
        wc = jnp.where(si <= ti, ws_ref[grp], 0.0).astype(BF16)
        rhs = jnp.concatenate([v_ref[n * C_CHUNK:(n + 1) * C_CHUNK, cols] for n in range(n_ch)], axis=1)
        s = _dot(wc, rhs)
        bias = bias_ref[:, cols]
        for n in range(n_ch):
            rows = slice(n * C_CHUNK, (n + 1) * C_CHUNK)
            sn = s[:, n * cg:(n + 1) * cg] + bias
            y_ref[rows, cols] = (u_ref[rows, cols] * sn * g_ref[rows, cols]).astype(BF16)
    for i in range(tm // rb):
        rows = slice(i * rb, (i + 1) * rb)
        o_ref[0, rows, :] = x_ref[0, rows, :] + _dot(y_ref[rows, :], wout_ref[...])


def _gmlp_mixer(x, norm_w_row, w_bf16, vw_row, vb_row, w_s, bias_full, w_out_bf16, tm=512):
    bsz, seq, d = x.shape
    kern = functools.partial(_gmlp_kernel, tm=tm)
    return pl.pallas_call(
        kern,
        grid=(bsz, seq // tm),
        in_specs=[pl.BlockSpec((1, tm, d), lambda b, i: (b, i, 0)),
                  pl.BlockSpec((1, d), lambda b, i: (0, 0)),
                  pl.BlockSpec((d, 3 * d), lambda b, i: (0, 0)),
                  pl.BlockSpec((1, d), lambda b, i: (0, 0)),
                  pl.BlockSpec((1, d), lambda b, i: (0, 0)),
                  pl.BlockSpec(w_s.shape, lambda b, i: (0, 0, 0)),
                  pl.BlockSpec((C_CHUNK, d), lambda b, i: (0, 0)),
                  pl.BlockSpec((d, d), lambda b, i: (0, 0))],
        out_specs=pl.BlockSpec((1, tm, d), lambda b, i: (b, i, 0)),
        out_shape=jax.ShapeDtypeStruct((bsz, seq, d), F32),
        scratch_shapes=[pltpu.VMEM((tm, d), BF16),
                        pltpu.VMEM((tm, d), F32),
                        pltpu.VMEM((tm, d), BF16),
                        pltpu.VMEM((tm, d), F32),
                        pltpu.VMEM((tm, d), BF16)],
        compiler_params=_params(("arbitrary", "arbitrary")),
        name="gmlp_mixer",
    )(x, norm_w_row, w_bf16, vw_row, vb_row, w_s, bias_full, w_out_bf16)


def kernel(x, norm_w, a_w_in, a_lower_bounds, a_o_norm_w, a_w_out, b_w_in, b_q_norm_w, b_k_norm_w, b_w_out,
           c_w_in, c_v_norm_w, c_v_norm_b, c_w_s, c_b_s, c_w_out):
    bsz, seq, d = x.shape
    depth = norm_w.shape[0]
    n_mixers = 3
    assert seq == DILATED_GROUPS[-1][0] and seq % A_CHUNK == 0 and d % LANES == 0

    def residual(y, w_out, xin):
        out = _out_proj(y.reshape(bsz * seq, d), w_out.astype(BF16), xin.reshape(bsz * seq, d))
        return out.reshape(bsz, seq, d)

    for layer in range(depth):
        kind, idx = layer % n_mixers, layer // n_mixers
        nw = norm_w[layer][None, :]
        if kind == 0:
            y = _hgrn_mixer(x, nw, a_w_in[idx].astype(BF16), a_lower_bounds, a_o_norm_w[idx][None, :], idx)
            x = residual(y, a_w_out[idx], x)
        elif kind == 1:
            n_heads = d // B_HEAD_DIM
            n_total = B_N_GROUPS * n_heads
            slopes = jnp.exp2(-ALIBI_MAX_EXP * jnp.arange(1, n_total + 1, dtype=F32) / n_total)
            rows = []
            for g in range(B_N_GROUPS):
                rows.append(jnp.tile(b_q_norm_w[idx, g], n_heads) * (B_HEAD_DIM ** -0.5 * LOG2E))
                rows.append(jnp.tile(b_k_norm_w[idx, g], n_heads))
                rows.append(jnp.ones((d,), F32))
            rows.append(jnp.ones((d,), F32))
            head_w = jnp.stack(rows)[:, None, :]
            proj = _attn_in_proj(x, nw, b_w_in[idx].astype(BF16), head_w)
            x = _out_proj_pairs(_attention(proj, slopes), b_w_out[idx].astype(BF16), x)
        else:
            bias_full = jnp.repeat(c_b_s[idx].T, d // C_GROUPS, axis=1)
            x = _gmlp_mixer(x, nw, c_w_in[idx].astype(BF16), c_v_norm_w[idx][None, :],
                            c_v_norm_b[idx][None, :], c_w_s[idx], bias_full, c_w_out[idx].astype(BF16))
    return x
```

```python
import functools

import numpy as np
import jax
import jax.numpy as jnp
from jax import lax
from jax.experimental import pallas as pl
from jax.experimental.pallas import tpu as pltpu

F32 = jnp.float32
BF16 = jnp.bfloat16

NORM_EPS = 1e-6
NEG_BIG = -1e30
LB_FLOOR = 1e-30
LOG2E = float(np.log2(np.e))

LANES = 128
SUBLANES = 8
VMEM_LIMIT_BYTES = 56 * 1024 * 1024

A_HEAD_DIM = 128
A_CHUNK = 128
A_CHUNKS_PER_ITER = 2
A_HEADS_PER_STEP = 4
B_HEAD_DIM = 64
DILATED_GROUPS = ((128, 1), (512, 4), (2048, 16))
B_N_GROUPS = len(DILATED_GROUPS)
B_BLOCK = 128
ALIBI_MAX_EXP = 8.0
C_CHUNK = 128
C_GROUPS = 8


def _silu(x):
    hx = 0.5 * x
    return hx + hx * jnp.tanh(hx)


def _gelu_tanh(x):
    c = np.float32(np.sqrt(2.0 / np.pi))
    return 0.5 * x * (1.0 + jnp.tanh(c * (x + 0.044715 * (x * x * x))))


def _rms_rows(xf, w):
    ms = jnp.mean(xf * xf, axis=-1, keepdims=True)
    return xf * lax.rsqrt(ms + NORM_EPS) * w


def _dot(a, b):
    return jnp.dot(a, b, preferred_element_type=F32)


def _dot_nt(a, b):
    return lax.dot_general(a, b, (((1,), (1,)), ((), ())), preferred_element_type=F32)


def _dot_tn(a, b):
    return lax.dot_general(a, b, (((0,), (0,)), ((), ())), preferred_element_type=F32)


def _params(sem):
    return pltpu.CompilerParams(dimension_semantics=sem, vmem_limit_bytes=VMEM_LIMIT_BYTES)


def _out_proj_kernel(y_ref, w_ref, x_ref, o_ref):
    o_ref[...] = x_ref[...] + _dot(y_ref[...], w_ref[...])


def _out_proj(y2d, w_bf16, x2d, tm=2048):
    m, k = y2d.shape
    n = w_bf16.shape[1]
    return pl.pallas_call(
        _out_proj_kernel,
        grid=(m // tm,),
        in_specs=[pl.BlockSpec((tm, k), lambda i: (i, 0)),
                  pl.BlockSpec((k, n), lambda i: (0, 0)),
                  pl.BlockSpec((tm, n), lambda i: (i, 0))],
        out_specs=pl.BlockSpec((tm, n), lambda i: (i, 0)),
        out_shape=jax.ShapeDtypeStruct((m, n), F32),
        compiler_params=_params(("arbitrary",)),
        name="out_proj",
    )(y2d, w_bf16, x2d)


def _out_proj_pairs_kernel(y_ref, w_ref, x_ref, o_ref):
    y = jnp.concatenate([y_ref[hp] for hp in range(y_ref.shape[0])], axis=1)
    o_ref[...] = x_ref[...] + _dot(y, w_ref[...])


def _out_proj_pairs(y_pairs, w_bf16, x, tm=1024):
    bsz, n_pairs, seq, lanes = y_pairs.shape
    d = n_pairs * lanes
    n = w_bf16.shape[1]
    return pl.pallas_call(
        _out_proj_pairs_kernel,
        grid=(bsz, seq // tm),
        in_specs=[pl.BlockSpec((None, n_pairs, tm, lanes), lambda b, i: (b, 0, i, 0)),
                  pl.BlockSpec((d, n), lambda b, i: (0, 0)),
                  pl.BlockSpec((None, tm, n), lambda b, i: (b, i, 0))],
        out_specs=pl.BlockSpec((None, tm, n), lambda b, i: (b, i, 0)),
        out_shape=jax.ShapeDtypeStruct((bsz, seq, n), F32),
        compiler_params=_params(("arbitrary", "arbitrary")),
        name="out_proj_pairs",
    )(y_pairs, w_bf16, x)


def _hgrn_levels(chunk):
    return [1 << j for j in range(int(np.log2(chunk)))]


def _hgrn_level_exponents(b, log_f, b_view, sgn_ref, chunk):
    out = {}
    n_vregs = chunk // SUBLANES
    width = b.shape[-1]

    def row(r, n):
        one_group = jnp.broadcast_to(b_view[pl.ds(r, 1), :], (SUBLANES, width))
        return jnp.concatenate([one_group] * (n // SUBLANES), axis=0)

    for m in _hgrn_levels(chunk):
        if m >= SUBLANES:
            parts = []
            for j in range(chunk // (2 * m)):
                base = j * 2 * m
                ref_row = row(base + m - 1, m)
                parts.append(ref_row - b[base:base + m, :])
                parts.append(b[base + m:base + 2 * m, :] - ref_row)
            out[m] = jnp.concatenate(parts, axis=0)
        elif m == 4:
            ref = jnp.concatenate([row(SUBLANES * i + 3, SUBLANES) for i in range(n_vregs)], axis=0)
            out[m] = (b - ref) * sgn_ref[0]
        elif m == 2:
            lo = jnp.concatenate([row(SUBLANES * i + 1, SUBLANES) for i in range(n_vregs)], axis=0)
            hi = jnp.concatenate([row(SUBLANES * i + 5, SUBLANES) for i in range(n_vregs)], axis=0)
            ref = jnp.where(sgn_ref[0] > 0.0, hi, lo)
            out[m] = (b - ref) * sgn_ref[1]
        else:
            out[m] = log_f * sgn_ref[2]
    return out


def _hgrn_kernel(x_ref, nw_ref, wq_ref, wf_ref, wi_ref, wg_ref, lbraw_ref, onw_ref, tri_ref, y_ref,
                 h_ref, act_ref, v16_ref, b_ref, st_ref, lvl_ref, sgn_ref, qd_ref, kd_ref, a16_ref, dec_ref,
                 *, layer_idx, seq, chunk, hps, cpi):
    step = pl.program_id(1)
    dh = A_HEAD_DIM
    rb = 256

    @pl.when(step == 0)
    def _():
        for i in range(seq // rb):
            xs = x_ref[0, i * rb:(i + 1) * rb, :]
            h_ref[i * rb:(i + 1) * rb, :] = _rms_rows(xs, nw_ref[...]).astype(BF16)

    raw = lbraw_ref[...]
    n_layers = raw.shape[0]
    mx = raw[0:1, :]
    for i in range(1, n_layers):
        mx = jnp.maximum(mx, raw[i:i + 1, :])
    es = [jnp.exp(raw[i:i + 1, :] - mx) for i in range(n_layers)]
    z = es[0]
    for i in range(1, n_layers):
        z = z + es[i]
    soft = [e / z for e in es]
    csum = soft[0]
    for i in range(1, layer_idx + 1):
        csum = csum + soft[i]
    lb = csum - soft[0]
    log_lb = jnp.log(jnp.maximum(lb, LB_FLOOR))
    log1m_lb = jnp.log1p(-lb)
    one_m_lb = 1.0 - lb
    heads = range(hps)

    pb = 512
    for i in range(seq // pb):
        rows = slice(i * pb, (i + 1) * pb)
        hh = h_ref[rows, :]
        res_q, res_f, res_i, res_g = (_dot(hh, w[...]) for w in (wq_ref, wf_ref, wi_ref, wg_ref))
        v16_ref[rows, :] = res_i.astype(BF16)
        for a in heads:
            o0 = a * 4 * dh
            lbs = slice(a * dh, (a + 1) * dh)
            pf = res_f[:, lbs]
            log_sig = jnp.minimum(pf, 0.0) - jnp.log(1.0 + jnp.exp(-jnp.abs(pf)))
            bb = log1m_lb[:, lbs] + log_sig
            ll = log_lb[:, lbs]
            log_f = jnp.maximum(ll, bb) + jnp.log(1.0 + jnp.exp(-jnp.abs(ll - bb)))
            act_ref[rows, o0:o0 + dh] = _silu(res_q[:, lbs])
            act_ref[rows, o0 + dh:o0 + 2 * dh] = one_m_lb[:, lbs] * jnp.exp(log_sig - pf)
            act_ref[rows, o0 + 2 * dh:o0 + 3 * dh] = _silu(res_g[:, lbs])
            act_ref[rows, o0 + 3 * dh:o0 + 4 * dh] = log_f * LOG2E

    ti = lax.broadcasted_iota(jnp.int32, (chunk, chunk), 0)
    si = lax.broadcasted_iota(jnp.int32, (chunk, chunk), 1)
    lvl_ref[...] = jnp.where(si < ti, ti ^ si, jnp.where(si == ti, 0, -1))
    tr = lax.broadcasted_iota(jnp.int32, (chunk, dh), 0)
    sgn_ref[0] = jnp.where((tr & 4) != 0, 1.0, -1.0)
    sgn_ref[1] = jnp.where((tr & 2) != 0, 1.0, -1.0)
    sgn_ref[2] = jnp.where((tr & 1) != 0, 1.0, 0.0)
    st_ref[...] = jnp.zeros_like(st_ref)
    onw = onw_ref[...]
    levels = _hgrn_levels(chunk)

    def row_start(c):
        return c * chunk if isinstance(c, int) else pl.multiple_of(c * chunk, chunk)

    def front(chunks):
        q, k, log_f = [], [], []
        for c in chunks:
            for a in heads:
                o0 = a * 4 * dh
                q.append(act_ref[pl.ds(row_start(c), chunk), o0:o0 + dh])
                k.append(act_ref[pl.ds(row_start(c), chunk), o0 + dh:o0 + 2 * dh])
                log_f.append(act_ref[pl.ds(row_start(c), chunk), o0 + 3 * dh:o0 + 4 * dh])
        slots = range(len(q))

        b = []
        for a in slots:
            lf_hi = log_f[a].astype(BF16)
            lf_lo = (log_f[a] - lf_hi.astype(F32)).astype(BF16)
            b.append(_dot(tri_ref[...], lf_hi) + _dot(tri_ref[...], lf_lo))
            b_ref[a] = b[a]

        expo, acc = [], []
        for a in slots:
            expo.append(_hgrn_level_exponents(b[a], log_f[a], b_ref.at[a], sgn_ref, chunk))
            qd_ref[a] = (q[a] * jnp.exp2(b[a])).astype(BF16)
            b_last = b_ref[a, chunk - 1:chunk, :]
            kd_ref[a] = (k[a] * jnp.exp2(b_last - b[a])).astype(BF16)
            dec_ref[a] = jnp.exp2(b_last)
            acc.append(jnp.where(lvl_ref[...] == 0, _dot_nt(q[a].astype(BF16), k[a].astype(BF16)), 0.0))
        for m in levels:
            if m >= SUBLANES:
                break
            is_query_row = sgn_ref[{4: 0, 2: 1, 1: 2}[m]] > 0.0
            for a in slots:
                w = (jnp.where(is_query_row, q[a], k[a]) * jnp.exp2(expo[a][m])).astype(BF16)
                acc[a] = jnp.where(lvl_ref[...] >= m, _dot_nt(w, w), acc[a])
        n_grp = chunk // SUBLANES
        acc = [[acc_a[SUBLANES * i:SUBLANES * (i + 1), :] for i in range(n_grp)] for acc_a in acc]
        for m in levels:
            if m < SUBLANES:
                continue
            is_query = [(SUBLANES * i) % (2 * m) >= m for i in range(n_grp)]
            for a in slots:
                roles = jnp.concatenate([(q[a] if is_query[i] else k[a])[SUBLANES * i:SUBLANES * (i + 1), :]
                                         for i in range(n_grp)], axis=0)
                w = roles * jnp.exp2(expo[a][m])
                q_m = jnp.concatenate([w[SUBLANES * i:SUBLANES * (i + 1), :]
                                       for i in range(n_grp) if is_query[i]], axis=0)
                a_m = _dot_nt(q_m.astype(BF16), w.astype(BF16))
                for n, i in enumerate([i for i in range(n_grp) if is_query[i]]):
                    rows = slice(SUBLANES * i, SUBLANES * (i + 1))
                    acc[a][i] = jnp.where(lvl_ref[rows, :] >= m,
                                          a_m[SUBLANES * n:SUBLANES * (n + 1), :], acc[a][i])
        for a in slots:
            a16_ref[a] = jnp.concatenate(acc[a], axis=0).astype(BF16)

    def back(chunks):
        for n, c in enumerate(chunks):
            r0 = row_start(c)
            for a in heads:
                o0 = a * 4 * dh
                slot = n * hps + a
                v16 = v16_ref[pl.ds(r0, chunk), a * dh:(a + 1) * dh]
                st = st_ref[a]
                oa = _dot_nt(qd_ref[slot], st.astype(BF16)) + _dot(a16_ref[slot], v16)
                st_ref[a] = st * dec_ref[slot] + _dot_tn(v16, kd_ref[slot])
                y = _rms_rows(oa, onw) * act_ref[pl.ds(r0, chunk), o0 + 2 * dh:o0 + 3 * dh]
                y_ref[0, pl.ds(r0, chunk), a * dh:(a + 1) * dh] = y.astype(BF16)

    n_iter = seq // (chunk * cpi)
    front([n for n in range(cpi)])

    def body(i, carry):
        back([(i - 1) * cpi + n for n in range(cpi)])
        front([i * cpi + n for n in range(cpi)])
        return carry

    lax.fori_loop(1, n_iter, body, 0)
    back([(n_iter - 1) * cpi + n for n in range(cpi)])


def _hgrn_mixer(x, norm_w_row, w_in, lb_raw, o_norm_w_row, layer_idx):
    bsz, seq, d = x.shape
    dh = A_HEAD_DIM
    hps = A_HEADS_PER_STEP
    n_steps = d // (dh * hps)
    chunk = A_CHUNK
    tri = jnp.asarray(np.tril(np.ones((chunk, chunk), np.float32)), BF16)
    cpi = A_CHUNKS_PER_ITER
    n_slots = hps * cpi
    assert seq % (chunk * cpi) == 0
    kern = functools.partial(_hgrn_kernel, layer_idx=layer_idx, seq=seq, chunk=chunk, hps=hps, cpi=cpi)

    def w_spec(part):
        return pl.BlockSpec((d, hps * dh), lambda b, h, part=part: (0, part * n_steps + h))

    return pl.pallas_call(
        kern,
        grid=(bsz, n_steps),
        in_specs=[pl.BlockSpec((1, seq, d), lambda b, h: (b, 0, 0)),
                  pl.BlockSpec((1, d), lambda b, h: (0, 0)),
                  w_spec(0), w_spec(1), w_spec(2), w_spec(3),
                  pl.BlockSpec((lb_raw.shape[0], hps * dh), lambda b, h: (0, h)),
                  pl.BlockSpec((1, dh), lambda b, h: (0, 0)),
                  pl.BlockSpec((chunk, chunk), lambda b, h: (0, 0))],
        out_specs=pl.BlockSpec((1, seq, hps * dh), lambda b, h: (b, 0, h)),
        out_shape=jax.ShapeDtypeStruct((bsz, seq, d), BF16),
        scratch_shapes=[pltpu.VMEM((seq, d), BF16),
                        pltpu.VMEM((seq, hps * 4 * dh), F32),
                        pltpu.VMEM((seq, hps * dh), BF16),
                        pltpu.VMEM((n_slots, chunk, dh), F32),
                        pltpu.VMEM((hps, dh, dh), F32),
                        pltpu.VMEM((chunk, chunk), jnp.int32),
                        pltpu.VMEM((3, chunk, dh), F32),
                        pltpu.VMEM((n_slots, chunk, dh), BF16),
                        pltpu.VMEM((n_slots, chunk, dh), BF16),
                        pltpu.VMEM((n_slots, chunk, chunk), BF16),
                        pltpu.VMEM((n_slots, 1, dh), F32)],
        compiler_params=_params(("arbitrary", "arbitrary")),
        name="hgrn_mixer",
    )(x, norm_w_row, w_in, w_in, w_in, w_in, lb_raw, o_norm_w_row, tri)


def _bproj_kernel(x_ref, nw_ref, w_ref, hw_ref, eseg_ref, esegt_ref, o_ref, h_ref, hn_ref, *, seq, dils):
    j = pl.program_id(1)
    n_qkv_tiles = 3 * len(dils)

    @pl.when(j == 0)
    def _():
        d = x_ref.shape[-1]
        n_planes = d // LANES
        rb = 256
        for i in range(seq // rb):
            hn = _rms_rows(x_ref[0, i * rb:(i + 1) * rb, :], nw_ref[...])
            for c in range(n_planes):
                hn_ref[c, i * rb:(i + 1) * rb, :] = hn[:, c * LANES:(c + 1) * LANES]
        for g, dil in enumerate(dils):
            sub = seq // dil
            for r in range(dil):
                for c0 in range(0, sub, rb):
                    rows = min(rb, sub - c0)
                    if dil == 1:
                        idx = pl.ds(c0, rows)
                    else:
                        idx = pl.ds(r + c0 * dil, rows, stride=dil)
                    for c in range(n_planes):
                        h_ref[g, r * sub + c0:r * sub + c0 + rows, c * LANES:(c + 1) * LANES] = (
                            hn_ref[c, idx, :].astype(BF16))

    grp = jnp.where(j < n_qkv_tiles, j // 3, 0)
    kind = jnp.where(j < n_qkv_tiles, j % 3, 3)
    pb = 512

    def tile(i):
        return _dot(h_ref[grp, i * pb:(i + 1) * pb, :], w_ref[...])

    def store(i, val):
        for hp in range(val.shape[-1] // LANES):
            o_ref[hp, i * pb:(i + 1) * pb, :] = val[:, hp * LANES:(hp + 1) * LANES].astype(BF16)

    @pl.when(kind < 2)
    def _():
        hw = hw_ref[0]
        n_heads = x_ref.shape[-1] // B_HEAD_DIM
        hi_lanes = lax.broadcasted_iota(jnp.int32, (pb, LANES), 1) < n_heads
        for i in range(seq // pb):
            y = tile(i)
            ssq = _dot((y * y).astype(BF16), eseg_ref[...])
            r = lax.rsqrt(ssq * (1.0 / B_HEAD_DIM) + NORM_EPS)
            r_hi = r.astype(BF16).astype(F32)
            r_split = jnp.where(hi_lanes, r_hi, r - r_hi).astype(BF16)
            rexp = _dot(r_split, esegt_ref[...])
            store(i, y * rexp * hw)

    @pl.when(kind == 2)
    def _():
        for i in range(seq // pb):
            store(i, tile(i))

    @pl.when(kind == 3)
    def _():
        for i in range(seq // pb):
            store(i, _silu(tile(i)))


def _attn_in_proj(x, norm_w_row, w_bf16, head_w):
    bsz, seq, d = x.shape
    n_out = w_bf16.shape[1]
    tn = d
    dils = tuple(dil for _, dil in DILATED_GROUPS)
    n_heads = d // B_HEAD_DIM
    assert 2 * n_heads <= LANES
    eseg = np.zeros((d, LANES), np.float32)
    eseg[np.arange(d), np.arange(d) // B_HEAD_DIM] = 1.0
    eseg[np.arange(d), n_heads + np.arange(d) // B_HEAD_DIM] = 1.0
    eseg = jnp.asarray(eseg, BF16)
    kern = functools.partial(_bproj_kernel, seq=seq, dils=dils)
    return pl.pallas_call(
        kern,
        grid=(bsz, n_out // tn),
        in_specs=[pl.BlockSpec((1, seq, d), lambda b, j: (b, 0, 0)),
                  pl.BlockSpec((1, d), lambda b, j: (0, 0)),
                  pl.BlockSpec((d, tn), lambda b, j: (0, j)),
                  pl.BlockSpec((1, 1, tn), lambda b, j: (j, 0, 0)),
                  pl.BlockSpec((d, LANES), lambda b, j: (0, 0)),
                  pl.BlockSpec((LANES, d), lambda b, j: (0, 0))],
        out_specs=pl.BlockSpec((None, None, tn // LANES, seq, LANES), lambda b, j: (b, j, 0, 0, 0)),
        out_shape=jax.ShapeDtypeStruct((bsz, n_out // tn, tn // LANES, seq, LANES), BF16),
        scratch_shapes=[pltpu.VMEM((len(dils), seq, d), BF16),
                        pltpu.VMEM((d // LANES, seq, LANES), F32)],
        compiler_params=_params(("arbitrary", "arbitrary")),
        name="attn_in_proj",
    )(x, norm_w_row, w_bf16, head_w, eseg, eseg.T)


def _attn_kernel(slopes_ref, q0, k0, v0, q1, k1, v1, q2, k2, v2, g_ref, o_ref,
                 num_ref, den_ref, max_ref, maxsw_ref, bias_ref, s_ref, p_ref, *, seq, dils, unrolls):
    hp = pl.program_id(1)
    blk = B_BLOCK
    hd = B_HEAD_DIM
    n_heads = (LANES // hd) * pl.num_programs(1)
    qkv = ((q0, k0, v0), (q1, k1, v1), (q2, k2, v2))
    head_a = lax.broadcasted_iota(jnp.int32, (blk, LANES), 1) < hd


    def token_rows(g, tok0):
        return pl.ds(tok0, blk) if dils[g] == 1 else pl.ds(tok0, blk, stride=dils[g])

    def qk_phase(g, specs, has_prev, slot):
        q_ref, k_ref, _ = qkv[g]
        nk = 2 * blk if has_prev else blk
        for w, (row0, _) in enumerate(specs):
            k_lo = row0 - blk if has_prev else row0
            q2d = q_ref[pl.ds(row0, blk), :]
            zero = jnp.zeros_like(q2d)
            qq = jnp.concatenate([jnp.where(head_a, q2d, zero), jnp.where(head_a, zero, q2d)], axis=0)
            s_ref[slot, w, :, 0:nk] = _dot_nt(qq, k_ref[pl.ds(k_lo, nk), :])

    def softmax_phase(g, specs, has_prev, slot):
        nk = 2 * blk if has_prev else blk
        off = 0 if has_prev else blk
        for w, (_, tok0) in enumerate(specs):
            ms = []
            for a in range(2):
                sa = s_ref[slot, w, a * blk:(a + 1) * blk, 0:nk] + bias_ref[a, :, off:off + nk]
                m = jnp.max(sa, axis=-1, keepdims=True)
                p_ref[slot, w, a, :, 0:nk] = jnp.exp2(sa - m).astype(BF16)
                ms.append(m)
            idx = token_rows(g, tok0)
            max_ref[g, idx, :] = jnp.where(head_a, ms[0], ms[1])
            maxsw_ref[g, idx, :] = jnp.where(head_a, ms[1], ms[0])

    def pv_phase(g, specs, has_prev, slot):
        v_ref = qkv[g][2]
        nk = 2 * blk if has_prev else blk
        head_a_k = lax.broadcasted_iota(jnp.int32, (nk, LANES), 1) < hd
        for w, (row0, tok0) in enumerate(specs):
            k_lo = row0 - blk if has_prev else row0
            v2d = v_ref[pl.ds(k_lo, nk), :]
            one = jnp.ones_like(v2d)
            oa = _dot(p_ref[slot, w, 0, :, 0:nk], jnp.where(head_a_k, v2d, one))
            ob = _dot(p_ref[slot, w, 1, :, 0:nk], jnp.where(head_a_k, one, v2d))
            idx = token_rows(g, tok0)
            num_ref[g, idx, :] = jnp.where(head_a, oa, ob)
            den_ref[g, idx, :] = jnp.where(head_a, ob, oa)

    def pipelined(n_iter, make_specs, g, has_prev):
        def step(t):
            static = isinstance(t, int)
            if (not static) or 0 <= t - 2 < n_iter:
                pv_phase(g, make_specs(t - 2), has_prev, t % 2)
            if (not static) or 0 <= t - 1 < n_iter:
                softmax_phase(g, make_specs(t - 1), has_prev, (t - 1) % 2)
            if (not static) or 0 <= t < n_iter:
                qk_phase(g, make_specs(t), has_prev, t % 2)

        for t in range(2):
            step(t)
        if n_iter > 2:
            def body(t, carry):
                step(t)
                return carry

            lax.fori_loop(2, n_iter, body, 0)
        for t in range(max(n_iter, 2), n_iter + 2):
            step(t)

    qi = lax.broadcasted_iota(jnp.int32, (blk, 2 * blk), 0)
    kj = lax.broadcasted_iota(jnp.int32, (blk, 2 * blk), 1)
    dist = blk + qi - kj
    valid = (dist >= 0) & (dist <= blk)

    for g, dil in enumerate(dils):
        sub = seq // dil
        n_blk = sub // blk
        bias_base = jnp.where(valid, -(dist * dil).astype(F32), NEG_BIG)
        for a in range(2):
            slope = slopes_ref[g * n_heads + 2 * hp + a]
            bias_ref[a] = (slope * LOG2E) * bias_base

        u1, u2 = unrolls[g]

        def aligned(row):
            return row if isinstance(row, int) else pl.multiple_of(row, blk)

        def first_specs(i, sub=sub, u1=u1):
            return [(aligned((i * u1 + w) * sub), i * u1 + w) for w in range(u1)]

        def rest_specs(i, dil=dil, sub=sub, n_blk=n_blk, u2=u2):
            specs = []
            for w in range(u2):
                idx = i * u2 + w
                r = idx // (n_blk - 1)
                b = idx % (n_blk - 1) + 1
                specs.append((aligned(r * sub + b * blk), b * blk * dil + r))
            return specs

        pipelined(dil // u1, first_specs, g, False)
        n_rest = dil * (n_blk - 1)
        if n_rest:
            pipelined(n_rest // u2, rest_specs, g, True)

    rb = 256
    n_g = len(dils)

    def merge(i, carry):
        r0 = pl.multiple_of(i * rb, rb)
        ms = [max_ref[g, pl.ds(r0, rb), :] for g in range(n_g)]
        mx = ms[0]
        for g in range(1, n_g):
            mx = jnp.maximum(mx, ms[g])
        ms_sw = [maxsw_ref[g, pl.ds(r0, rb), :] for g in range(n_g)]
        mx_sw = ms_sw[0]
        for g in range(1, n_g):
            mx_sw = jnp.maximum(mx_sw, ms_sw[g])
        num = jnp.zeros((rb, LANES), F32)
        den = jnp.zeros((rb, LANES), F32)
        for g in range(n_g):
            num = num + jnp.exp2(ms[g] - mx) * num_ref[g, pl.ds(r0, rb), :]
            den = den + jnp.exp2(ms_sw[g] - mx_sw) * den_ref[g, pl.ds(r0, rb), :]
        den = pltpu.roll(den, hd, axis=1)
        gate = g_ref[pl.ds(r0, rb), :].astype(F32)
        o_ref[pl.ds(r0, rb), :] = (num / den * gate).astype(BF16)
        return carry

    lax.fori_loop(0, seq // rb, merge, 0)


def _attention(proj, slopes):
    bsz, n_tiles, n_pairs, seq, _ = proj.shape
    assert n_tiles == 3 * B_N_GROUPS + 1
    dils = tuple(dil for _, dil in DILATED_GROUPS)
    unrolls = ((1, 5), (4, 6), (4, 1))
    for (u1, u2), dil in zip(unrolls, dils):
        n_rest = dil * (seq // dil // B_BLOCK - 1)
        assert dil % u1 == 0 and n_rest % u2 == 0

    def col_spec(tile):
        return pl.BlockSpec((None, None, None, seq, LANES), lambda b, h, s, tile=tile: (b, tile, h, 0, 0))

    kern = functools.partial(_attn_kernel, seq=seq, dils=dils, unrolls=unrolls)
    grid_spec = pltpu.PrefetchScalarGridSpec(
        num_scalar_prefetch=1,
        grid=(bsz, n_pairs),
        in_specs=[col_spec(t) for t in range(n_tiles)],
        out_specs=pl.BlockSpec((None, None, seq, LANES), lambda b, h, s: (b, h, 0, 0)),
        scratch_shapes=[pltpu.VMEM((B_N_GROUPS, seq, LANES), F32)] * 4
        + [pltpu.VMEM((2, B_BLOCK, 2 * B_BLOCK), F32),
           pltpu.VMEM((2, max(max(u) for u in unrolls), 2 * B_BLOCK, 2 * B_BLOCK), F32),
           pltpu.VMEM((2, max(max(u) for u in unrolls), 2, B_BLOCK, 2 * B_BLOCK), BF16)],
    )
    return pl.pallas_call(
        kern,
        grid_spec=grid_spec,
        out_shape=jax.ShapeDtypeStruct((bsz, n_pairs, seq, LANES), BF16),
        compiler_params=_params(("arbitrary", "arbitrary")),
        name="dilated_attention",
    )(slopes, *([proj] * n_tiles))


def _gmlp_kernel(x_ref, yprev_ref, wprev_ref, nw_ref, w_ref, vw_ref, vb_ref, ws_ref, bias_ref, wout_ref, o_ref,
                 x1_ref, h_ref, u_ref, v_ref, g_ref, y_ref, *, tm):
    d = x_ref.shape[-1]
    cg = d // C_GROUPS
    rb = 256
    for i in range(tm // rb):
        rows = slice(i * rb, (i + 1) * rb)
        y_prev = jnp.concatenate([yprev_ref[hp, rows, :] for hp in range(yprev_ref.shape[0])], axis=1)
        x1_ref[rows, :] = x_ref[0, rows, :] + _dot(y_prev, wprev_ref[...])
        h_ref[rows, :] = _rms_rows(x1_ref[rows, :], nw_ref[...]).astype(BF16)
    for i in range(tm // rb):
        rows = slice(i * rb, (i + 1) * rb)
        hh = h_ref[rows, :]
        u_ref[rows, :] = _gelu_tanh(_dot(hh, w_ref[:, 0:d]))
        vv = _gelu_tanh(_dot(hh, w_ref[:, d:2 * d]))
        mu = jnp.mean(vv, axis=-1, keepdims=True)
        vc = vv - mu
        var = jnp.mean(vc * vc, axis=-1, keepdims=True)
        v_ref[rows, :] = (vc * lax.rsqrt(var + NORM_EPS) * vw_ref[...] + vb_ref[...]).astype(BF16)
        g_ref[rows, :] = _silu(_dot(hh, w_ref[:, 2 * d:3 * d]))
    ti = lax.broadcasted_iota(jnp.int32, (C_CHUNK, C_CHUNK), 0)
    si = lax.broadcasted_iota(jnp.int32, (C_CHUNK, C_CHUNK), 1)
    n_ch = tm // C_CHUNK
    for grp in range(C_GROUPS):
        cols = slice(grp * cg, (grp + 1) * cg)
        wc = jnp.where(si <= ti, ws_ref[grp], 0.0).astype(BF16)
        rhs = jnp.concatenate([v_ref[n * C_CHUNK:(n + 1) * C_CHUNK, cols] for n in range(n_ch)], axis=1)
        s = _dot(wc, rhs)
        bias = bias_ref[:, cols]
        for n in range(n_ch):
            rows = slice(n * C_CHUNK, (n + 1) * C_CHUNK)
            sn = s[:, n * cg:(n + 1) * cg] + bias
            y_ref[rows, cols] = (u_ref[rows, cols] * sn * g_ref[rows, cols]).astype(BF16)
    for i in range(tm // rb):
        rows = slice(i * rb, (i + 1) * rb)
        o_ref[0, rows, :] = x1_ref[rows, :] + _dot(y_ref[rows, :], wout_ref[...])


def _gmlp_mixer(x, y_prev_pairs, w_prev_bf16, norm_w_row, w_bf16, vw_row, vb_row, w_s, bias_full, w_out_bf16,
                tm=512):
    bsz, seq, d = x.shape
    n_pairs, lanes = y_prev_pairs.shape[1], y_prev_pairs.shape[3]
    kern = functools.partial(_gmlp_kernel, tm=tm)
    return pl.pallas_call(
        kern,
        grid=(bsz, seq // tm),
        in_specs=[pl.BlockSpec((1, tm, d), lambda b, i: (b, i, 0)),
                  pl.BlockSpec((None, n_pairs, tm, lanes), lambda b, i: (b, 0, i, 0)),
                  pl.BlockSpec((d, d), lambda b, i: (0, 0)),
                  pl.BlockSpec((1, d), lambda b, i: (0, 0)),
                  pl.BlockSpec((d, 3 * d), lambda b, i: (0, 0)),
                  pl.BlockSpec((1, d), lambda b, i: (0, 0)),
                  pl.BlockSpec((1, d), lambda b, i: (0, 0)),
                  pl.BlockSpec(w_s.shape, lambda b, i: (0, 0, 0)),
                  pl.BlockSpec((C_CHUNK, d), lambda b, i: (0, 0)),
                  pl.BlockSpec((d, d), lambda b, i: (0, 0))],
        out_specs=pl.BlockSpec((1, tm, d), lambda b, i: (b, i, 0)),
        out_shape=jax.ShapeDtypeStruct((bsz, seq, d), F32),
        scratch_shapes=[pltpu.VMEM((tm, d), F32),
                        pltpu.VMEM((tm, d), BF16),
                        pltpu.VMEM((tm, d), F32),
                        pltpu.VMEM((tm, d), BF16),
                        pltpu.VMEM((tm, d), F32),
                        pltpu.VMEM((tm, d), BF16)],
        compiler_params=_params(("arbitrary", "arbitrary")),
        name="gmlp_mixer",
    )(x, y_prev_pairs, w_prev_bf16, norm_w_row, w_bf16, vw_row, vb_row, w_s, bias_full, w_out_bf16)


def kernel(x, norm_w, a_w_in, a_lower_bounds, a_o_norm_w, a_w_out, b_w_in, b_q_norm_w, b_k_norm_w, b_w_out,
           c_w_in, c_v_norm_w, c_v_norm_b, c_w_s, c_b_s, c_w_out):
    bsz, seq, d = x.shape
    depth = norm_w.shape[0]
    n_mixers = 3
    assert seq == DILATED_GROUPS[-1][0] and seq % A_CHUNK == 0 and d % LANES == 0

    def residual(y, w_out, xin):
        out = _out_proj(y.reshape(bsz * seq, d), w_out.astype(BF16), xin.reshape(bsz * seq, d))
        return out.reshape(bsz, seq, d)

    pending = None
    for layer in range(depth):
        kind, idx = layer % n_mixers, layer // n_mixers
        nw = norm_w[layer][None, :]
        if kind == 0:
            y = _hgrn_mixer(x, nw, a_w_in[idx].astype(BF16), a_lower_bounds, a_o_norm_w[idx][None, :], idx)
            x = residual(y, a_w_out[idx], x)
        elif kind == 1:
            n_heads = d // B_HEAD_DIM
            n_total = B_N_GROUPS * n_heads
            slopes = jnp.exp2(-ALIBI_MAX_EXP * jnp.arange(1, n_total + 1, dtype=F32) / n_total)
            rows = []
            for g in range(B_N_GROUPS):
                rows.append(jnp.tile(b_q_norm_w[idx, g], n_heads) * (B_HEAD_DIM ** -0.5 * LOG2E))
                rows.append(jnp.tile(b_k_norm_w[idx, g], n_heads))
                rows.append(jnp.ones((d,), F32))
            rows.append(jnp.ones((d,), F32))
            head_w = jnp.stack(rows)[:, None, :]
            proj = _attn_in_proj(x, nw, b_w_in[idx].astype(BF16), head_w)
            pending = (_attention(proj, slopes), b_w_out[idx].astype(BF16))
            if layer == depth - 1:
                x = _out_proj_pairs(*pending, x)
        else:
            bias_full = jnp.repeat(c_b_s[idx].T, d // C_GROUPS, axis=1)
            x = _gmlp_mixer(x, *pending, nw, c_w_in[idx].astype(BF16), c_v_norm_w[idx][None, :],
                            c_v_norm_b[idx][None, :], c_w_s[idx], bias_full, c_w_out[idx].astype(BF16))
    return x
```

```python
import functools

import numpy as np
import jax
import jax.numpy as jnp
from jax import lax
from jax.experimental import pallas as pl
from jax.experimental.pallas import tpu as pltpu

F32 = jnp.float32
BF16 = jnp.bfloat16

NORM_EPS = 1e-6
NEG_BIG = -1e30
LB_FLOOR = 1e-30
LOG2E = float(np.log2(np.e))

LANES = 128
SUBLANES = 8
VMEM_LIMIT_BYTES = 60 * 1024 * 1024

A_HEAD_DIM = 128
A_CHUNK = 128
A_CHUNKS_PER_ITER = 2
A_HEADS_PER_STEP = 4
B_HEAD_DIM = 64
DILATED_GROUPS = ((128, 1), (512, 4), (2048, 16))
B_N_GROUPS = len(DILATED_GROUPS)
B_BLOCK = 128
ALIBI_MAX_EXP = 8.0
C_CHUNK = 128
C_GROUPS = 8


def _silu(x):
    hx = 0.5 * x
    return hx + hx * jnp.tanh(hx)


def _gelu_tanh(x):
    c = np.float32(np.sqrt(2.0 / np.pi))
    return 0.5 * x * (1.0 + jnp.tanh(c * (x + 0.044715 * (x * x * x))))


def _rms_rows(xf, w):
    ms = jnp.mean(xf * xf, axis=-1, keepdims=True)
    return xf * lax.rsqrt(ms + NORM_EPS) * w


def _dot(a, b):
    return jnp.dot(a, b, preferred_element_type=F32)


def _dot_nt(a, b):
    return lax.dot_general(a, b, (((1,), (1,)), ((), ())), preferred_element_type=F32)


def _dot_tn(a, b):
    return lax.dot_general(a, b, (((0,), (0,)), ((), ())), preferred_element_type=F32)


def _params(sem):
    return pltpu.CompilerParams(dimension_semantics=sem, vmem_limit_bytes=VMEM_LIMIT_BYTES)


def _out_proj_kernel(y_ref, w_ref, x_ref, o_ref):
    o_ref[...] = x_ref[...] + _dot(y_ref[...], w_ref[...])


def _out_proj(y2d, w_bf16, x2d, tm=2048):
    m, k = y2d.shape
    n = w_bf16.shape[1]
    return pl.pallas_call(
        _out_proj_kernel,
        grid=(m // tm,),
        in_specs=[pl.BlockSpec((tm, k), lambda i: (i, 0)),
                  pl.BlockSpec((k, n), lambda i: (0, 0)),
                  pl.BlockSpec((tm, n), lambda i: (i, 0))],
        out_specs=pl.BlockSpec((tm, n), lambda i: (i, 0)),
        out_shape=jax.ShapeDtypeStruct((m, n), F32),
        compiler_params=_params(("arbitrary",)),
        name="out_proj",
    )(y2d, w_bf16, x2d)


def _out_proj_pairs_kernel(y_ref, w_ref, x_ref, o_ref):
    y = jnp.concatenate([y_ref[hp] for hp in range(y_ref.shape[0])], axis=1)
    o_ref[...] = x_ref[...] + _dot(y, w_ref[...])


def _out_proj_pairs(y_pairs, w_bf16, x, tm=1024):
    bsz, n_pairs, seq, lanes = y_pairs.shape
    d = n_pairs * lanes
    n = w_bf16.shape[1]
    return pl.pallas_call(
        _out_proj_pairs_kernel,
        grid=(bsz, seq // tm),
        in_specs=[pl.BlockSpec((None, n_pairs, tm, lanes), lambda b, i: (b, 0, i, 0)),
                  pl.BlockSpec((d, n), lambda b, i: (0, 0)),
                  pl.BlockSpec((None, tm, n), lambda b, i: (b, i, 0))],
        out_specs=pl.BlockSpec((None, tm, n), lambda b, i: (b, i, 0)),
        out_shape=jax.ShapeDtypeStruct((bsz, seq, n), F32),
        compiler_params=_params(("arbitrary", "arbitrary")),
        name="out_proj_pairs",
    )(y_pairs, w_bf16, x)


A_BAND = 4


def _hgrn_levels(chunk):
    return [1 << j for j in range(int(np.log2(chunk))) if (2 << j) > A_BAND]


def _hgrn_level_exponents(b, b_view, sgn_ref, chunk):
    out = {}
    n_vregs = chunk // SUBLANES
    width = b.shape[-1]

    def row(r, n):
        one_group = jnp.broadcast_to(b_view[pl.ds(r, 1), :], (SUBLANES, width))
        return jnp.concatenate([one_group] * (n // SUBLANES), axis=0)

    for m in _hgrn_levels(chunk):
        if m >= SUBLANES:
            parts = []
            for j in range(chunk // (2 * m)):
                base = j * 2 * m
                ref_row = row(base + m - 1, m)
                parts.append(ref_row - b[base:base + m, :])
                parts.append(b[base + m:base + 2 * m, :] - ref_row)
            out[m] = jnp.concatenate(parts, axis=0)
        else:
            assert m == 4
            ref = jnp.concatenate([row(SUBLANES * i + 3, SUBLANES) for i in range(n_vregs)], axis=0)
            out[m] = (b - ref) * sgn_ref[...]
    return out


def _hgrn_kernel(x_ref, nw_ref, wq_ref, wf_ref, wi_ref, wg_ref, lbraw_ref, onw_ref, tri_ref, y_ref,
                 h_ref, act_ref, v16_ref, b_ref, st_ref, lvl_ref, band_ref, sgn_ref, qd_ref, kd_ref, a16_ref,
                 dec_ref,
                 *, layer_idx, seq, chunk, hps, cpi):
    step = pl.program_id(1)
    dh = A_HEAD_DIM
    rb = 256

    @pl.when(step == 0)
    def _():
        for i in range(seq // rb):
            xs = x_ref[0, i * rb:(i + 1) * rb, :]
            h_ref[i * rb:(i + 1) * rb, :] = _rms_rows(xs, nw_ref[...]).astype(BF16)

    raw = lbraw_ref[...]
    n_layers = raw.shape[0]
    mx = raw[0:1, :]
    for i in range(1, n_layers):
        mx = jnp.maximum(mx, raw[i:i + 1, :])
    es = [jnp.exp(raw[i:i + 1, :] - mx) for i in range(n_layers)]
    z = es[0]
    for i in range(1, n_layers):
        z = z + es[i]
    soft = [e / z for e in es]
    csum = soft[0]
    for i in range(1, layer_idx + 1):
        csum = csum + soft[i]
    lb = csum - soft[0]
    log_lb = jnp.log(jnp.maximum(lb, LB_FLOOR))
    log1m_lb = jnp.log1p(-lb)
    one_m_lb = 1.0 - lb
    heads = range(hps)

    pb = 512
    for i in range(seq // pb):
        rows = slice(i * pb, (i + 1) * pb)
        hh = h_ref[rows, :]
        res_q, res_f, res_i, res_g = (_dot(hh, w[...]) for w in (wq_ref, wf_ref, wi_ref, wg_ref))
        v16_ref[rows, :] = res_i.astype(BF16)
        for a in heads:
            o0 = a * 4 * dh
            lbs = slice(a * dh, (a + 1) * dh)
            pf = res_f[:, lbs]
            log_sig = jnp.minimum(pf, 0.0) - jnp.log(1.0 + jnp.exp(-jnp.abs(pf)))
            bb = log1m_lb[:, lbs] + log_sig
            ll = log_lb[:, lbs]
            log_f = jnp.maximum(ll, bb) + jnp.log(1.0 + jnp.exp(-jnp.abs(ll - bb)))
            act_ref[rows, o0:o0 + dh] = _silu(res_q[:, lbs])
            act_ref[rows, o0 + dh:o0 + 2 * dh] = one_m_lb[:, lbs] * jnp.exp(log_sig - pf)
            act_ref[rows, o0 + 2 * dh:o0 + 3 * dh] = _silu(res_g[:, lbs])
            act_ref[rows, o0 + 3 * dh:o0 + 4 * dh] = log_f * LOG2E

    ti = lax.broadcasted_iota(jnp.int32, (chunk, chunk), 0)
    si = lax.broadcasted_iota(jnp.int32, (chunk, chunk), 1)
    lvl_ref[...] = jnp.where(si < ti, ti ^ si, jnp.where(si == ti, 0, -1))
    band_ref[...] = jnp.where((si <= ti) & ((ti ^ si) < A_BAND), ti - si, -1)
    tr = lax.broadcasted_iota(jnp.int32, (chunk, dh), 0)
    sgn_ref[...] = jnp.where((tr & 4) != 0, 1.0, -1.0)
    st_ref[...] = jnp.zeros_like(st_ref)
    onw = onw_ref[...]
    levels = _hgrn_levels(chunk)

    def row_start(c):
        return c * chunk if isinstance(c, int) else pl.multiple_of(c * chunk, chunk)

    def front(chunks):
        q, k, log_f = [], [], []
        for c in chunks:
            for a in heads:
                o0 = a * 4 * dh
                q.append(act_ref[pl.ds(row_start(c), chunk), o0:o0 + dh])
                k.append(act_ref[pl.ds(row_start(c), chunk), o0 + dh:o0 + 2 * dh])
                log_f.append(act_ref[pl.ds(row_start(c), chunk), o0 + 3 * dh:o0 + 4 * dh])
        slots = range(len(q))

        b = []
        for a in slots:
            lf_hi = log_f[a].astype(BF16)
            lf_lo = (log_f[a] - lf_hi.astype(F32)).astype(BF16)
            b.append(_dot(tri_ref[...], lf_hi) + _dot(tri_ref[...], lf_lo))
            b_ref[a] = b[a]

        expo, acc = [], []
        for a in slots:
            expo.append(_hgrn_level_exponents(b[a], b_ref.at[a], sgn_ref, chunk))
            qd_ref[a] = (q[a] * jnp.exp2(b[a])).astype(BF16)
            b_last = b_ref[a, chunk - 1:chunk, :]
            kd_ref[a] = (k[a] * jnp.exp2(b_last - b[a])).astype(BF16)
            dec_ref[a] = jnp.exp2(b_last)
        n_grp = chunk // SUBLANES

        def shifted(x, dist):
            return pltpu.roll(x.reshape(n_grp, SUBLANES, dh), dist, axis=1).reshape(chunk, dh)

        for a in slots:
            acc.append(jnp.where(band_ref[...] == 0, jnp.sum(q[a] * k[a], axis=-1, keepdims=True), 0.0))
        for dist in range(1, A_BAND):
            for a in slots:
                decay = jnp.exp2(b[a] - shifted(b[a], dist))
                c = jnp.sum(q[a] * shifted(k[a], dist) * decay, axis=-1, keepdims=True)
                acc[a] = jnp.where(band_ref[...] == dist, c, acc[a])
        for m in levels:
            if m >= SUBLANES:
                break
            is_query_row = sgn_ref[...] > 0.0
            for a in slots:
                w = (jnp.where(is_query_row, q[a], k[a]) * jnp.exp2(expo[a][m])).astype(BF16)
                acc[a] = jnp.where(lvl_ref[...] >= m, _dot_nt(w, w), acc[a])
        n_grp = chunk // SUBLANES
        acc = [[acc_a[SUBLANES * i:SUBLANES * (i + 1), :] for i in range(n_grp)] for acc_a in acc]
        for m in levels:
            if m < SUBLANES:
                continue
            is_query = [(SUBLANES * i) % (2 * m) >= m for i in range(n_grp)]
            for a in slots:
                roles = jnp.concatenate([(q[a] if is_query[i] else k[a])[SUBLANES * i:SUBLANES * (i + 1), :]
                                         for i in range(n_grp)], axis=0)
                w = roles * jnp.exp2(expo[a][m])
                q_m = jnp.concatenate([w[SUBLANES * i:SUBLANES * (i + 1), :]
                                       for i in range(n_grp) if is_query[i]], axis=0)
                a_m = _dot_nt(q_m.astype(BF16), w.astype(BF16))
                for n, i in enumerate([i for i in range(n_grp) if is_query[i]]):
                    rows = slice(SUBLANES * i, SUBLANES * (i + 1))
                    acc[a][i] = jnp.where(lvl_ref[rows, :] >= m,
                                          a_m[SUBLANES * n:SUBLANES * (n + 1), :], acc[a][i])
        for a in slots:
            a16_ref[a] = jnp.concatenate(acc[a], axis=0).astype(BF16)

    def back(chunks):
        for n, c in enumerate(chunks):
            r0 = row_start(c)
            for a in heads:
                o0 = a * 4 * dh
                slot = n * hps + a
                v16 = v16_ref[pl.ds(r0, chunk), a * dh:(a + 1) * dh]
                st = st_ref[a]
                oa = _dot_nt(qd_ref[slot], st.astype(BF16)) + _dot(a16_ref[slot], v16)
                st_ref[a] = st * dec_ref[slot] + _dot_tn(v16, kd_ref[slot])
                y = _rms_rows(oa, onw) * act_ref[pl.ds(r0, chunk), o0 + 2 * dh:o0 + 3 * dh]
                y_ref[0, pl.ds(r0, chunk), a * dh:(a + 1) * dh] = y.astype(BF16)

    n_iter = seq // (chunk * cpi)
    front([n for n in range(cpi)])

    def body(i, carry):
        back([(i - 1) * cpi + n for n in range(cpi)])
        front([i * cpi + n for n in range(cpi)])
        return carry

    lax.fori_loop(1, n_iter, body, 0)
    back([(n_iter - 1) * cpi + n for n in range(cpi)])


def _hgrn_mixer(x, norm_w_row, w_in, lb_raw, o_norm_w_row, layer_idx):
    bsz, seq, d = x.shape
    dh = A_HEAD_DIM
    hps = A_HEADS_PER_STEP
    n_steps = d // (dh * hps)
    chunk = A_CHUNK
    tri = jnp.asarray(np.tril(np.ones((chunk, chunk), np.float32)), BF16)
    cpi = A_CHUNKS_PER_ITER
    n_slots = hps * cpi
    assert seq % (chunk * cpi) == 0
    kern = functools.partial(_hgrn_kernel, layer_idx=layer_idx, seq=seq, chunk=chunk, hps=hps, cpi=cpi)

    def w_spec(part):
        return pl.BlockSpec((d, hps * dh), lambda b, h, part=part: (0, part * n_steps + h))

    return pl.pallas_call(
        kern,
        grid=(bsz, n_steps),
        in_specs=[pl.BlockSpec((1, seq, d), lambda b, h: (b, 0, 0)),
                  pl.BlockSpec((1, d), lambda b, h: (0, 0)),
                  w_spec(0), w_spec(1), w_spec(2), w_spec(3),
                  pl.BlockSpec((lb_raw.shape[0], hps * dh), lambda b, h: (0, h)),
                  pl.BlockSpec((1, dh), lambda b, h: (0, 0)),
                  pl.BlockSpec((chunk, chunk), lambda b, h: (0, 0))],
        out_specs=pl.BlockSpec((1, seq, hps * dh), lambda b, h: (b, 0, h)),
        out_shape=jax.ShapeDtypeStruct((bsz, seq, d), BF16),
        scratch_shapes=[pltpu.VMEM((seq, d), BF16),
                        pltpu.VMEM((seq, hps * 4 * dh), F32),
                        pltpu.VMEM((seq, hps * dh), BF16),
                        pltpu.VMEM((n_slots, chunk, dh), F32),
                        pltpu.VMEM((hps, dh, dh), F32),
                        pltpu.VMEM((chunk, chunk), jnp.int32),
                        pltpu.VMEM((chunk, chunk), jnp.int32),
                        pltpu.VMEM((chunk, dh), F32),
                        pltpu.VMEM((n_slots, chunk, dh), BF16),
                        pltpu.VMEM((n_slots, chunk, dh), BF16),
                        pltpu.VMEM((n_slots, chunk, chunk), BF16),
                        pltpu.VMEM((n_slots, 1, dh), F32)],
        compiler_params=_params(("arbitrary", "arbitrary")),
        name="hgrn_mixer",
    )(x, norm_w_row, w_in, w_in, w_in, w_in, lb_raw, o_norm_w_row, tri)


def _bproj_kernel(x_ref, nw_ref, w_ref, hw_ref, eseg_ref, esegt_ref, o_ref, h_ref, hn_ref, *, seq, dils):
    j = pl.program_id(1)
    n_qkv_tiles = 3 * len(dils)

    @pl.when(j == 0)
    def _():
        d = x_ref.shape[-1]
        n_planes = d // LANES
        rb = 256
        for i in range(seq // rb):
            hn = _rms_rows(x_ref[0, i * rb:(i + 1) * rb, :], nw_ref[...])
            for c in range(n_planes):
                hn_ref[c, i * rb:(i + 1) * rb, :] = hn[:, c * LANES:(c + 1) * LANES]
        for g, dil in enumerate(dils):
            sub = seq // dil
            for r in range(dil):
                for c0 in range(0, sub, rb):
                    rows = min(rb, sub - c0)
                    if dil == 1:
                        idx = pl.ds(c0, rows)
                    else:
                        idx = pl.ds(r + c0 * dil, rows, stride=dil)
                    for c in range(n_planes):
                        h_ref[g, r * sub + c0:r * sub + c0 + rows, c * LANES:(c + 1) * LANES] = (
                            hn_ref[c, idx, :].astype(BF16))

    grp = jnp.where(j < n_qkv_tiles, j // 3, 0)
    kind = jnp.where(j < n_qkv_tiles, j % 3, 3)
    pb = 512

    def tile(i):
        return _dot(h_ref[grp, i * pb:(i + 1) * pb, :], w_ref[...])

    def store(i, val):
        for hp in range(val.shape[-1] // LANES):
            o_ref[hp, i * pb:(i + 1) * pb, :] = val[:, hp * LANES:(hp + 1) * LANES].astype(BF16)

    @pl.when(kind < 2)
    def _():
        hw = hw_ref[0]
        n_heads = x_ref.shape[-1] // B_HEAD_DIM
        hi_lanes = lax.broadcasted_iota(jnp.int32, (pb, LANES), 1) < n_heads
        for i in range(seq // pb):
            y = tile(i)
            ssq = _dot((y * y).astype(BF16), eseg_ref[...])
            r = lax.rsqrt(ssq * (1.0 / B_HEAD_DIM) + NORM_EPS)
            r_hi = r.astype(BF16).astype(F32)
            r_split = jnp.where(hi_lanes, r_hi, r - r_hi).astype(BF16)
            rexp = _dot(r_split, esegt_ref[...])
            store(i, y * rexp * hw)

    @pl.when(kind == 2)
    def _():
        for i in range(seq // pb):
            store(i, tile(i))

    @pl.when(kind == 3)
    def _():
        for i in range(seq // pb):
            store(i, _silu(tile(i)))


def _attn_in_proj(x, norm_w_row, w_bf16, head_w):
    bsz, seq, d = x.shape
    n_out = w_bf16.shape[1]
    tn = d
    dils = tuple(dil for _, dil in DILATED_GROUPS)
    n_heads = d // B_HEAD_DIM
    assert 2 * n_heads <= LANES
    eseg = np.zeros((d, LANES), np.float32)
    eseg[np.arange(d), np.arange(d) // B_HEAD_DIM] = 1.0
    eseg[np.arange(d), n_heads + np.arange(d) // B_HEAD_DIM] = 1.0
    eseg = jnp.asarray(eseg, BF16)
    kern = functools.partial(_bproj_kernel, seq=seq, dils=dils)
    return pl.pallas_call(
        kern,
        grid=(bsz, n_out // tn),
        in_specs=[pl.BlockSpec((1, seq, d), lambda b, j: (b, 0, 0)),
                  pl.BlockSpec((1, d), lambda b, j: (0, 0)),
                  pl.BlockSpec((d, tn), lambda b, j: (0, j)),
                  pl.BlockSpec((1, 1, tn), lambda b, j: (j, 0, 0)),
                  pl.BlockSpec((d, LANES), lambda b, j: (0, 0)),
                  pl.BlockSpec((LANES, d), lambda b, j: (0, 0))],
        out_specs=pl.BlockSpec((None, None, tn // LANES, seq, LANES), lambda b, j: (b, j, 0, 0, 0)),
        out_shape=jax.ShapeDtypeStruct((bsz, n_out // tn, tn // LANES, seq, LANES), BF16),
        scratch_shapes=[pltpu.VMEM((len(dils), seq, d), BF16),
                        pltpu.VMEM((d // LANES, seq, LANES), F32)],
        compiler_params=_params(("arbitrary", "arbitrary")),
        name="attn_in_proj",
    )(x, norm_w_row, w_bf16, head_w, eseg, eseg.T)


def _attn_kernel(slopes_ref, q0, k0, v0, q1, k1, v1, q2, k2, v2, g_ref, o_ref,
                 num_ref, den_ref, max_ref, maxsw_ref, bias_ref, s_ref, p_ref, *, seq, dils, unrolls):
    hp = pl.program_id(1)
    blk = B_BLOCK
    hd = B_HEAD_DIM
    n_heads = (LANES // hd) * pl.num_programs(1)
    qkv = ((q0, k0, v0), (q1, k1, v1), (q2, k2, v2))
    head_a = lax.broadcasted_iota(jnp.int32, (blk, LANES), 1) < hd


    def token_rows(g, tok0):
        return pl.ds(tok0, blk) if dils[g] == 1 else pl.ds(tok0, blk, stride=dils[g])

    def qk_phase(g, specs, has_prev, slot):
        q_ref, k_ref, _ = qkv[g]
        nk = 2 * blk if has_prev else blk
        for w, (row0, _) in enumerate(specs):
            k_lo = row0 - blk if has_prev else row0
            q2d = q_ref[pl.ds(row0, blk), :]
            zero = jnp.zeros_like(q2d)
            qq = jnp.concatenate([jnp.where(head_a, q2d, zero), jnp.where(head_a, zero, q2d)], axis=0)
            s_ref[slot, w, :, 0:nk] = _dot_nt(qq, k_ref[pl.ds(k_lo, nk), :])

    def softmax_phase(g, specs, has_prev, slot):
        nk = 2 * blk if has_prev else blk
        off = 0 if has_prev else blk
        for w, (_, tok0) in enumerate(specs):
            ms = []
            for a in range(2):
                sa = s_ref[slot, w, a * blk:(a + 1) * blk, 0:nk] + bias_ref[a, :, off:off + nk]
                m = jnp.max(sa, axis=-1, keepdims=True)
                p_ref[slot, w, a, :, 0:nk] = jnp.exp2(sa - m).astype(BF16)
                ms.append(m)
            idx = token_rows(g, tok0)
            max_ref[g, idx, :] = jnp.where(head_a, ms[0], ms[1])
            maxsw_ref[g, idx, :] = jnp.where(head_a, ms[1], ms[0])

    def pv_phase(g, specs, has_prev, slot):
        v_ref = qkv[g][2]
        nk = 2 * blk if has_prev else blk
        head_a_k = lax.broadcasted_iota(jnp.int32, (nk, LANES), 1) < hd
        for w, (row0, tok0) in enumerate(specs):
            k_lo = row0 - blk if has_prev else row0
            v2d = v_ref[pl.ds(k_lo, nk), :]
            one = jnp.ones_like(v2d)
            oa = _dot(p_ref[slot, w, 0, :, 0:nk], jnp.where(head_a_k, v2d, one))
            ob = _dot(p_ref[slot, w, 1, :, 0:nk], jnp.where(head_a_k, one, v2d))
            idx = token_rows(g, tok0)
            num_ref[g, idx, :] = jnp.where(head_a, oa, ob)
            den_ref[g, idx, :] = jnp.where(head_a, ob, oa)

    def pipelined(n_iter, make_specs, g, has_prev):
        def step(t):
            static = isinstance(t, int)
            if (not static) or 0 <= t - 2 < n_iter:
                pv_phase(g, make_specs(t - 2), has_prev, t % 2)
            if (not static) or 0 <= t - 1 < n_iter:
                softmax_phase(g, make_specs(t - 1), has_prev, (t - 1) % 2)
            if (not static) or 0 <= t < n_iter:
                qk_phase(g, make_specs(t), has_prev, t % 2)

        for t in range(2):
            step(t)
        if n_iter > 2:
            def body(t, carry):
                step(t)
                return carry

            lax.fori_loop(2, n_iter, body, 0)
        for t in range(max(n_iter, 2), n_iter + 2):
            step(t)

    qi = lax.broadcasted_iota(jnp.int32, (blk, 2 * blk), 0)
    kj = lax.broadcasted_iota(jnp.int32, (blk, 2 * blk), 1)
    dist = blk + qi - kj
    valid = (dist >= 0) & (dist <= blk)

    for g, dil in enumerate(dils):
        sub = seq // dil
        n_blk = sub // blk
        bias_base = jnp.where(valid, -(dist * dil).astype(F32), NEG_BIG)
        for a in range(2):
            slope = slopes_ref[g * n_heads + 2 * hp + a]
            bias_ref[a] = (slope * LOG2E) * bias_base

        u1, u2 = unrolls[g]

        def aligned(row):
            return row if isinstance(row, int) else pl.multiple_of(row, blk)

        def first_specs(i, sub=sub, u1=u1):
            return [(aligned((i * u1 + w) * sub), i * u1 + w) for w in range(u1)]

        def rest_specs(i, dil=dil, sub=sub, n_blk=n_blk, u2=u2):
            specs = []
            for w in range(u2):
                idx = i * u2 + w
                r = idx // (n_blk - 1)
                b = idx % (n_blk - 1) + 1
                specs.append((aligned(r * sub + b * blk), b * blk * dil + r))
            return specs

        pipelined(dil // u1, first_specs, g, False)
        n_rest = dil * (n_blk - 1)
        if n_rest:
            pipelined(n_rest // u2, rest_specs, g, True)

    rb = 256
    n_g = len(dils)

    def merge(i, carry):
        r0 = pl.multiple_of(i * rb, rb)
        ms = [max_ref[g, pl.ds(r0, rb), :] for g in range(n_g)]
        mx = ms[0]
        for g in range(1, n_g):
            mx = jnp.maximum(mx, ms[g])
        ms_sw = [maxsw_ref[g, pl.ds(r0, rb), :] for g in range(n_g)]
        mx_sw = ms_sw[0]
        for g in range(1, n_g):
            mx_sw = jnp.maximum(mx_sw, ms_sw[g])
        num = jnp.zeros((rb, LANES), F32)
        den = jnp.zeros((rb, LANES), F32)
        for g in range(n_g):
            num = num + jnp.exp2(ms[g] - mx) * num_ref[g, pl.ds(r0, rb), :]
            den = den + jnp.exp2(ms_sw[g] - mx_sw) * den_ref[g, pl.ds(r0, rb), :]
        den = pltpu.roll(den, hd, axis=1)
        gate = g_ref[pl.ds(r0, rb), :].astype(F32)
        o_ref[pl.ds(r0, rb), :] = (num / den * gate).astype(BF16)
        return carry

    lax.fori_loop(0, seq // rb, merge, 0)


def _attention(proj, slopes):
    bsz, n_tiles, n_pairs, seq, _ = proj.shape
    assert n_tiles == 3 * B_N_GROUPS + 1
    dils = tuple(dil for _, dil in DILATED_GROUPS)
    unrolls = ((1, 5), (4, 6), (4, 1))
    for (u1, u2), dil in zip(unrolls, dils):
        n_rest = dil * (seq // dil // B_BLOCK - 1)
        assert dil % u1 == 0 and n_rest % u2 == 0

    def col_spec(tile):
        return pl.BlockSpec((None, None, None, seq, LANES), lambda b, h, s, tile=tile: (b, tile, h, 0, 0))

    kern = functools.partial(_attn_kernel, seq=seq, dils=dils, unrolls=unrolls)
    grid_spec = pltpu.PrefetchScalarGridSpec(
        num_scalar_prefetch=1,
        grid=(bsz, n_pairs),
        in_specs=[col_spec(t) for t in range(n_tiles)],
        out_specs=pl.BlockSpec((None, None, seq, LANES), lambda b, h, s: (b, h, 0, 0)),
        scratch_shapes=[pltpu.VMEM((B_N_GROUPS, seq, LANES), F32)] * 4
        + [pltpu.VMEM((2, B_BLOCK, 2 * B_BLOCK), F32),
           pltpu.VMEM((2, max(max(u) for u in unrolls), 2 * B_BLOCK, 2 * B_BLOCK), F32),
           pltpu.VMEM((2, max(max(u) for u in unrolls), 2, B_BLOCK, 2 * B_BLOCK), BF16)],
    )
    return pl.pallas_call(
        kern,
        grid_spec=grid_spec,
        out_shape=jax.ShapeDtypeStruct((bsz, n_pairs, seq, LANES), BF16),
        compiler_params=_params(("arbitrary", "arbitrary")),
        name="dilated_attention",
    )(slopes, *([proj] * n_tiles))


def _gmlp_kernel(x_ref, yprev_ref, wprev_ref, nw_ref, w_ref, vw_ref, vb_ref, ws_ref, bias_ref, wout_ref, o_ref,
                 x1_ref, h_ref, u_ref, v_ref, g_ref, y_ref, *, tm):
    d = x_ref.shape[-1]
    cg = d // C_GROUPS
    rb = 256
    for i in range(tm // rb):
        rows = slice(i * rb, (i + 1) * rb)
        y_prev = jnp.concatenate([yprev_ref[hp, rows, :] for hp in range(yprev_ref.shape[0])], axis=1)
        x1_ref[rows, :] = x_ref[0, rows, :] + _dot(y_prev, wprev_ref[...])
        h_ref[rows, :] = _rms_rows(x1_ref[rows, :], nw_ref[...]).astype(BF16)
    for i in range(tm // rb):
        rows = slice(i * rb, (i + 1) * rb)
        hh = h_ref[rows, :]
        u_ref[rows, :] = _gelu_tanh(_dot(hh, w_ref[:, 0:d]))
        vv = _gelu_tanh(_dot(hh, w_ref[:, d:2 * d]))
        mu = jnp.mean(vv, axis=-1, keepdims=True)
        vc = vv - mu
        var = jnp.mean(vc * vc, axis=-1, keepdims=True)
        v_ref[rows, :] = (vc * lax.rsqrt(var + NORM_EPS) * vw_ref[...] + vb_ref[...]).astype(BF16)
        g_ref[rows, :] = _silu(_dot(hh, w_ref[:, 2 * d:3 * d]))
    ti = lax.broadcasted_iota(jnp.int32, (C_CHUNK, C_CHUNK), 0)
    si = lax.broadcasted_iota(jnp.int32, (C_CHUNK, C_CHUNK), 1)
    n_ch = tm // C_CHUNK
    for grp in range(C_GROUPS):
        cols = slice(grp * cg, (grp + 1) * cg)
        wc = jnp.where(si <= ti, ws_ref[grp], 0.0).astype(BF16)
        rhs = jnp.concatenate([v_ref[n * C_CHUNK:(n + 1) * C_CHUNK, cols] for n in range(n_ch)], axis=1)
        s = _dot(wc, rhs)
        bias = bias_ref[:, cols]
        for n in range(n_ch):
            rows = slice(n * C_CHUNK, (n + 1) * C_CHUNK)
            sn = s[:, n * cg:(n + 1) * cg] + bias
            y_ref[rows, cols] = (u_ref[rows, cols] * sn * g_ref[rows, cols]).astype(BF16)
    for i in range(tm // rb):
        rows = slice(i * rb, (i + 1) * rb)
        o_ref[0, rows, :] = x1_ref[rows, :] + _dot(y_ref[rows, :], wout_ref[...])


def _gmlp_mixer(x, y_prev_pairs, w_prev_bf16, norm_w_row, w_bf16, vw_row, vb_row, w_s, bias_full, w_out_bf16,
                tm=512):
    bsz, seq, d = x.shape
    n_pairs, lanes = y_prev_pairs.shape[1], y_prev_pairs.shape[3]
    kern = functools.partial(_gmlp_kernel, tm=tm)
    return pl.pallas_call(
        kern,
        grid=(bsz, seq // tm),
        in_specs=[pl.BlockSpec((1, tm, d), lambda b, i: (b, i, 0)),
                  pl.BlockSpec((None, n_pairs, tm, lanes), lambda b, i: (b, 0, i, 0)),
                  pl.BlockSpec((d, d), lambda b, i: (0, 0)),
                  pl.BlockSpec((1, d), lambda b, i: (0, 0)),
                  pl.BlockSpec((d, 3 * d), lambda b, i: (0, 0)),
                  pl.BlockSpec((1, d), lambda b, i: (0, 0)),
                  pl.BlockSpec((1, d), lambda b, i: (0, 0)),
                  pl.BlockSpec(w_s.shape, lambda b, i: (0, 0, 0)),
                  pl.BlockSpec((C_CHUNK, d), lambda b, i: (0, 0)),
                  pl.BlockSpec((d, d), lambda b, i: (0, 0))],
        out_specs=pl.BlockSpec((1, tm, d), lambda b, i: (b, i, 0)),
        out_shape=jax.ShapeDtypeStruct((bsz, seq, d), F32),
        scratch_shapes=[pltpu.VMEM((tm, d), F32),
                        pltpu.VMEM((tm, d), BF16),
                        pltpu.VMEM((tm, d), F32),
                        pltpu.VMEM((tm, d), BF16),
                        pltpu.VMEM((tm, d), F32),
                        pltpu.VMEM((tm, d), BF16)],
        compiler_params=_params(("arbitrary", "arbitrary")),
        name="gmlp_mixer",
    )(x, y_prev_pairs, w_prev_bf16, norm_w_row, w_bf16, vw_row, vb_row, w_s, bias_full, w_out_bf16)


def kernel(x, norm_w, a_w_in, a_lower_bounds, a_o_norm_w, a_w_out, b_w_in, b_q_norm_w, b_k_norm_w, b_w_out,
           c_w_in, c_v_norm_w, c_v_norm_b, c_w_s, c_b_s, c_w_out):
    bsz, seq, d = x.shape
    depth = norm_w.shape[0]
    n_mixers = 3
    assert seq == DILATED_GROUPS[-1][0] and seq % A_CHUNK == 0 and d % LANES == 0

    def residual(y, w_out, xin):
        out = _out_proj(y.reshape(bsz * seq, d), w_out.astype(BF16), xin.reshape(bsz * seq, d))
        return out.reshape(bsz, seq, d)

    pending = None
    for layer in range(depth):
        kind, idx = layer % n_mixers, layer // n_mixers
        nw = norm_w[layer][None, :]
        if kind == 0:
            y = _hgrn_mixer(x, nw, a_w_in[idx].astype(BF16), a_lower_bounds, a_o_norm_w[idx][None, :], idx)
            x = residual(y, a_w_out[idx], x)
        elif kind == 1:
            n_heads = d // B_HEAD_DIM
            n_total = B_N_GROUPS * n_heads
            slopes = jnp.exp2(-ALIBI_MAX_EXP * jnp.arange(1, n_total + 1, dtype=F32) / n_total)
            rows = []
            for g in range(B_N_GROUPS):
                rows.append(jnp.tile(b_q_norm_w[idx, g], n_heads) * (B_HEAD_DIM ** -0.5 * LOG2E))
                rows.append(jnp.tile(b_k_norm_w[idx, g], n_heads))
                rows.append(jnp.ones((d,), F32))
            rows.append(jnp.ones((d,), F32))
            head_w = jnp.stack(rows)[:, None, :]
            proj = _attn_in_proj(x, nw, b_w_in[idx].astype(BF16), head_w)
            pending = (_attention(proj, slopes), b_w_out[idx].astype(BF16))
            if layer == depth - 1:
                x = _out_proj_pairs(*pending, x)
        else:
            bias_full = jnp.repeat(c_b_s[idx].T, d // C_GROUPS, axis=1)
            x = _gmlp_mixer(x, *pending, nw, c_w_in[idx].astype(BF16), c_v_norm_w[idx][None, :],
                            c_v_norm_b[idx][None, :], c_w_s[idx], bias_full, c_w_out[idx].astype(BF16))
    return x
```

```python
import functools

import numpy as np
import jax
import jax.numpy as jnp
from jax import lax
from jax.experimental import pallas as pl
from jax.experimental.pallas import tpu as pltpu

F32 = jnp.float32
BF16 = jnp.bfloat16

NORM_EPS = 1e-6
NEG_BIG = -1e30
LB_FLOOR = 1e-30
LOG2E = float(np.log2(np.e))

LANES = 128
SUBLANES = 8
VMEM_LIMIT_BYTES = 60 * 1024 * 1024

A_HEAD_DIM = 128
A_CHUNK = 128
A_CHUNKS_PER_ITER = 2
A_HEADS_PER_STEP = 4
B_HEAD_DIM = 64
DILATED_GROUPS = ((128, 1), (512, 4), (2048, 16))
B_N_GROUPS = len(DILATED_GROUPS)
B_BLOCK = 128
ALIBI_MAX_EXP = 8.0
C_CHUNK = 128
C_GROUPS = 8


def _silu(x):
    hx = 0.5 * x
    return hx + hx * jnp.tanh(hx)


def _gelu_tanh(x):
    c = np.float32(np.sqrt(2.0 / np.pi))
    return 0.5 * x * (1.0 + jnp.tanh(c * (x + 0.044715 * (x * x * x))))


def _rms_rows(xf, w):
    ms = jnp.mean(xf * xf, axis=-1, keepdims=True)
    return xf * lax.rsqrt(ms + NORM_EPS) * w


def _dot(a, b):
    return jnp.dot(a, b, preferred_element_type=F32)


def _dot_nt(a, b):
    return lax.dot_general(a, b, (((1,), (1,)), ((), ())), preferred_element_type=F32)


def _dot_tn(a, b):
    return lax.dot_general(a, b, (((0,), (0,)), ((), ())), preferred_element_type=F32)


def _params(sem):
    return pltpu.CompilerParams(dimension_semantics=sem, vmem_limit_bytes=VMEM_LIMIT_BYTES)


def _out_proj_kernel(y_ref, w_ref, x_ref, o_ref):
    o_ref[...] = x_ref[...] + _dot(y_ref[...], w_ref[...])


def _out_proj(y2d, w_bf16, x2d, tm=2048):
    m, k = y2d.shape
    n = w_bf16.shape[1]
    return pl.pallas_call(
        _out_proj_kernel,
        grid=(m // tm,),
        in_specs=[pl.BlockSpec((tm, k), lambda i: (i, 0)),
                  pl.BlockSpec((k, n), lambda i: (0, 0)),
                  pl.BlockSpec((tm, n), lambda i: (i, 0))],
        out_specs=pl.BlockSpec((tm, n), lambda i: (i, 0)),
        out_shape=jax.ShapeDtypeStruct((m, n), F32),
        compiler_params=_params(("arbitrary",)),
        name="out_proj",
    )(y2d, w_bf16, x2d)


def _out_proj_pairs_kernel(y_ref, w_ref, x_ref, o_ref):
    y = jnp.concatenate([y_ref[hp] for hp in range(y_ref.shape[0])], axis=1)
    o_ref[...] = x_ref[...] + _dot(y, w_ref[...])


def _out_proj_pairs(y_pairs, w_bf16, x, tm=1024):
    bsz, n_pairs, seq, lanes = y_pairs.shape
    d = n_pairs * lanes
    n = w_bf16.shape[1]
    return pl.pallas_call(
        _out_proj_pairs_kernel,
        grid=(bsz, seq // tm),
        in_specs=[pl.BlockSpec((None, n_pairs, tm, lanes), lambda b, i: (b, 0, i, 0)),
                  pl.BlockSpec((d, n), lambda b, i: (0, 0)),
                  pl.BlockSpec((None, tm, n), lambda b, i: (b, i, 0))],
        out_specs=pl.BlockSpec((None, tm, n), lambda b, i: (b, i, 0)),
        out_shape=jax.ShapeDtypeStruct((bsz, seq, n), F32),
        compiler_params=_params(("arbitrary", "arbitrary")),
        name="out_proj_pairs",
    )(y_pairs, w_bf16, x)


A_BAND = 4


def _hgrn_levels(chunk):
    return [1 << j for j in range(int(np.log2(chunk))) if (2 << j) > A_BAND]


def _hgrn_level_exponents(b, b_view, sgn_ref, chunk):
    out = {}
    n_vregs = chunk // SUBLANES
    width = b.shape[-1]

    def row(r, n):
        one_group = jnp.broadcast_to(b_view[pl.ds(r, 1), :], (SUBLANES, width))
        return jnp.concatenate([one_group] * (n // SUBLANES), axis=0)

    for m in _hgrn_levels(chunk):
        if m >= SUBLANES:
            parts = []
            for j in range(chunk // (2 * m)):
                base = j * 2 * m
                ref_row = row(base + m - 1, m)
                parts.append(ref_row - b[base:base + m, :])
                parts.append(b[base + m:base + 2 * m, :] - ref_row)
            out[m] = jnp.concatenate(parts, axis=0)
        else:
            assert m == 4
            ref = jnp.concatenate([row(SUBLANES * i + 3, SUBLANES) for i in range(n_vregs)], axis=0)
            out[m] = (b - ref) * sgn_ref[...]
    return out


def _hgrn_kernel(x_ref, nw_ref, wq_ref, wf_ref, wi_ref, wg_ref, lbraw_ref, onw_ref, tri_ref, y_ref,
                 h_ref, act_ref, v16_ref, b_ref, st_ref, lvl_ref, band_ref, sgn_ref, qd_ref, kd_ref, a16_ref,
                 dec_ref,
                 *, layer_idx, seq, chunk, hps, cpi):
    step = pl.program_id(1)
    dh = A_HEAD_DIM
    rb = 256

    @pl.when(step == 0)
    def _():
        for i in range(seq // rb):
            xs = x_ref[0, i * rb:(i + 1) * rb, :]
            h_ref[i * rb:(i + 1) * rb, :] = _rms_rows(xs, nw_ref[...]).astype(BF16)

    raw = lbraw_ref[...]
    n_layers = raw.shape[0]
    mx = raw[0:1, :]
    for i in range(1, n_layers):
        mx = jnp.maximum(mx, raw[i:i + 1, :])
    es = [jnp.exp(raw[i:i + 1, :] - mx) for i in range(n_layers)]
    z = es[0]
    for i in range(1, n_layers):
        z = z + es[i]
    soft = [e / z for e in es]
    csum = soft[0]
    for i in range(1, layer_idx + 1):
        csum = csum + soft[i]
    lb = csum - soft[0]
    log_lb = jnp.log(jnp.maximum(lb, LB_FLOOR))
    log1m_lb = jnp.log1p(-lb)
    one_m_lb = 1.0 - lb
    heads = range(hps)

    pb = 512
    for i in range(seq // pb):
        rows = slice(i * pb, (i + 1) * pb)
        hh = h_ref[rows, :]
        res_q, res_f, res_i, res_g = (_dot(hh, w[...]) for w in (wq_ref, wf_ref, wi_ref, wg_ref))
        v16_ref[rows, :] = res_i.astype(BF16)
        for a in heads:
            o0 = a * 4 * dh
            lbs = slice(a * dh, (a + 1) * dh)
            pf = res_f[:, lbs]
            log_sig = jnp.minimum(pf, 0.0) - jnp.log(1.0 + jnp.exp(-jnp.abs(pf)))
            bb = log1m_lb[:, lbs] + log_sig
            ll = log_lb[:, lbs]
            log_f = jnp.maximum(ll, bb) + jnp.log(1.0 + jnp.exp(-jnp.abs(ll - bb)))
            act_ref[rows, o0:o0 + dh] = _silu(res_q[:, lbs])
            act_ref[rows, o0 + dh:o0 + 2 * dh] = one_m_lb[:, lbs] * jnp.exp(log_sig - pf)
            act_ref[rows, o0 + 2 * dh:o0 + 3 * dh] = _silu(res_g[:, lbs])
            act_ref[rows, o0 + 3 * dh:o0 + 4 * dh] = log_f * LOG2E

    ti = lax.broadcasted_iota(jnp.int32, (chunk, chunk), 0)
    si = lax.broadcasted_iota(jnp.int32, (chunk, chunk), 1)
    lvl_ref[...] = jnp.where(si < ti, ti ^ si, jnp.where(si == ti, 0, -1))
    band_ref[...] = jnp.where((si <= ti) & ((ti ^ si) < A_BAND), ti - si, -1)
    tr = lax.broadcasted_iota(jnp.int32, (chunk, dh), 0)
    sgn_ref[...] = jnp.where((tr & 4) != 0, 1.0, -1.0)
    st_ref[...] = jnp.zeros_like(st_ref)
    onw = onw_ref[...]
    levels = _hgrn_levels(chunk)

    def row_start(c):
        return c * chunk if isinstance(c, int) else pl.multiple_of(c * chunk, chunk)

    def front(chunks):
        q, k, log_f = [], [], []
        for c in chunks:
            for a in heads:
                o0 = a * 4 * dh
                q.append(act_ref[pl.ds(row_start(c), chunk), o0:o0 + dh])
                k.append(act_ref[pl.ds(row_start(c), chunk), o0 + dh:o0 + 2 * dh])
                log_f.append(act_ref[pl.ds(row_start(c), chunk), o0 + 3 * dh:o0 + 4 * dh])
        slots = range(len(q))

        b = []
        for a in slots:
            lf_hi = log_f[a].astype(BF16)
            lf_lo = (log_f[a] - lf_hi.astype(F32)).astype(BF16)
            b.append(_dot(tri_ref[...], lf_hi) + _dot(tri_ref[...], lf_lo))
            b_ref[a] = b[a]

        expo, acc = [], []
        for a in slots:
            expo.append(_hgrn_level_exponents(b[a], b_ref.at[a], sgn_ref, chunk))
            qd_ref[a] = (q[a] * jnp.exp2(b[a])).astype(BF16)
            b_last = b_ref[a, chunk - 1:chunk, :]
            kd_ref[a] = (k[a] * jnp.exp2(b_last - b[a])).astype(BF16)
            dec_ref[a] = jnp.exp2(b_last)
        n_grp = chunk // SUBLANES

        def shifted(x, dist):
            return pltpu.roll(x.reshape(n_grp, SUBLANES, dh), dist, axis=1).reshape(chunk, dh)

        for a in slots:
            acc.append(jnp.where(band_ref[...] == 0, jnp.sum(q[a] * k[a], axis=-1, keepdims=True), 0.0))
        for dist in range(1, A_BAND):
            for a in slots:
                decay = jnp.exp2(b[a] - shifted(b[a], dist))
                c = jnp.sum(q[a] * shifted(k[a], dist) * decay, axis=-1, keepdims=True)
                acc[a] = jnp.where(band_ref[...] == dist, c, acc[a])
        for m in levels:
            if m >= SUBLANES:
                break
            is_query_row = sgn_ref[...] > 0.0
            for a in slots:
                w = (jnp.where(is_query_row, q[a], k[a]) * jnp.exp2(expo[a][m])).astype(BF16)
                acc[a] = jnp.where(lvl_ref[...] >= m, _dot_nt(w, w), acc[a])
        n_grp = chunk // SUBLANES
        acc = [[acc_a[SUBLANES * i:SUBLANES * (i + 1), :] for i in range(n_grp)] for acc_a in acc]
        for m in levels:
            if m < SUBLANES:
                continue
            is_query = [(SUBLANES * i) % (2 * m) >= m for i in range(n_grp)]
            for a in slots:
                roles = jnp.concatenate([(q[a] if is_query[i] else k[a])[SUBLANES * i:SUBLANES * (i + 1), :]
                                         for i in range(n_grp)], axis=0)
                w = roles * jnp.exp2(expo[a][m])
                q_m = jnp.concatenate([w[SUBLANES * i:SUBLANES * (i + 1), :]
                                       for i in range(n_grp) if is_query[i]], axis=0)
                a_m = _dot_nt(q_m.astype(BF16), w.astype(BF16))
                for n, i in enumerate([i for i in range(n_grp) if is_query[i]]):
                    rows = slice(SUBLANES * i, SUBLANES * (i + 1))
                    acc[a][i] = jnp.where(lvl_ref[rows, :] >= m,
                                          a_m[SUBLANES * n:SUBLANES * (n + 1), :], acc[a][i])
        for a in slots:
            a16_ref[a] = jnp.concatenate(acc[a], axis=0).astype(BF16)

    def back(chunks):
        for n, c in enumerate(chunks):
            r0 = row_start(c)
            for a in heads:
                o0 = a * 4 * dh
                slot = n * hps + a
                v16 = v16_ref[pl.ds(r0, chunk), a * dh:(a + 1) * dh]
                st = st_ref[a]
                oa = _dot_nt(qd_ref[slot], st.astype(BF16)) + _dot(a16_ref[slot], v16)
                st_ref[a] = st * dec_ref[slot] + _dot_tn(v16, kd_ref[slot])
                y = _rms_rows(oa, onw) * act_ref[pl.ds(r0, chunk), o0 + 2 * dh:o0 + 3 * dh]
                y_ref[0, pl.ds(r0, chunk), a * dh:(a + 1) * dh] = y.astype(BF16)

    n_iter = seq // (chunk * cpi)
    front([n for n in range(cpi)])

    def body(i, carry):
        back([(i - 1) * cpi + n for n in range(cpi)])
        front([i * cpi + n for n in range(cpi)])
        return carry

    lax.fori_loop(1, n_iter, body, 0)
    back([(n_iter - 1) * cpi + n for n in range(cpi)])


def _hgrn_mixer(x, norm_w_row, w_in, lb_raw, o_norm_w_row, layer_idx):
    bsz, seq, d = x.shape
    dh = A_HEAD_DIM
    hps = A_HEADS_PER_STEP
    n_steps = d // (dh * hps)
    chunk = A_CHUNK
    tri = jnp.asarray(np.tril(np.ones((chunk, chunk), np.float32)), BF16)
    cpi = A_CHUNKS_PER_ITER
    n_slots = hps * cpi
    assert seq % (chunk * cpi) == 0
    kern = functools.partial(_hgrn_kernel, layer_idx=layer_idx, seq=seq, chunk=chunk, hps=hps, cpi=cpi)

    def w_spec(part):
        return pl.BlockSpec((d, hps * dh), lambda b, h, part=part: (0, part * n_steps + h))

    return pl.pallas_call(
        kern,
        grid=(bsz, n_steps),
        in_specs=[pl.BlockSpec((1, seq, d), lambda b, h: (b, 0, 0)),
                  pl.BlockSpec((1, d), lambda b, h: (0, 0)),
                  w_spec(0), w_spec(1), w_spec(2), w_spec(3),
                  pl.BlockSpec((lb_raw.shape[0], hps * dh), lambda b, h: (0, h)),
                  pl.BlockSpec((1, dh), lambda b, h: (0, 0)),
                  pl.BlockSpec((chunk, chunk), lambda b, h: (0, 0))],
        out_specs=pl.BlockSpec((1, seq, hps * dh), lambda b, h: (b, 0, h)),
        out_shape=jax.ShapeDtypeStruct((bsz, seq, d), BF16),
        scratch_shapes=[pltpu.VMEM((seq, d), BF16),
                        pltpu.VMEM((seq, hps * 4 * dh), F32),
                        pltpu.VMEM((seq, hps * dh), BF16),
                        pltpu.VMEM((n_slots, chunk, dh), F32),
                        pltpu.VMEM((hps, dh, dh), F32),
                        pltpu.VMEM((chunk, chunk), jnp.int32),
                        pltpu.VMEM((chunk, chunk), jnp.int32),
                        pltpu.VMEM((chunk, dh), F32),
                        pltpu.VMEM((n_slots, chunk, dh), BF16),
                        pltpu.VMEM((n_slots, chunk, dh), BF16),
                        pltpu.VMEM((n_slots, chunk, chunk), BF16),
                        pltpu.VMEM((n_slots, 1, dh), F32)],
        compiler_params=_params(("arbitrary", "arbitrary")),
        name="hgrn_mixer",
    )(x, norm_w_row, w_in, w_in, w_in, w_in, lb_raw, o_norm_w_row, tri)


def _bproj_kernel(x_ref, nw_ref, w_ref, hw_ref, eseg_ref, esegt_ref, o_ref, h_ref, hn_ref, *, seq, dils):
    j = pl.program_id(1)
    n_qkv_tiles = 3 * len(dils)

    @pl.when(j == 0)
    def _():
        d = x_ref.shape[-1]
        n_planes = d // LANES
        rb = 256
        for i in range(seq // rb):
            hn = _rms_rows(x_ref[0, i * rb:(i + 1) * rb, :], nw_ref[...])
            for c in range(n_planes):
                hn_ref[c, i * rb:(i + 1) * rb, :] = hn[:, c * LANES:(c + 1) * LANES]
        for g, dil in enumerate(dils):
            sub = seq // dil
            for r in range(dil):
                for c0 in range(0, sub, rb):
                    rows = min(rb, sub - c0)
                    if dil == 1:
                        idx = pl.ds(c0, rows)
                    else:
                        idx = pl.ds(r + c0 * dil, rows, stride=dil)
                    for c in range(n_planes):
                        h_ref[g, r * sub + c0:r * sub + c0 + rows, c * LANES:(c + 1) * LANES] = (
                            hn_ref[c, idx, :].astype(BF16))

    grp = jnp.where(j < n_qkv_tiles, j // 3, 0)
    kind = jnp.where(j < n_qkv_tiles, j % 3, 3)
    pb = 512

    def tile(i):
        return _dot(h_ref[grp, i * pb:(i + 1) * pb, :], w_ref[...])

    def store(i, val):
        for hp in range(val.shape[-1] // LANES):
            o_ref[hp, i * pb:(i + 1) * pb, :] = val[:, hp * LANES:(hp + 1) * LANES].astype(BF16)

    @pl.when(kind < 2)
    def _():
        hw = hw_ref[0]
        n_heads = x_ref.shape[-1] // B_HEAD_DIM
        hi_lanes = lax.broadcasted_iota(jnp.int32, (pb, LANES), 1) < n_heads
        for i in range(seq // pb):
            y = tile(i)
            ssq = _dot((y * y).astype(BF16), eseg_ref[...])
            r = lax.rsqrt(ssq * (1.0 / B_HEAD_DIM) + NORM_EPS)
            r_hi = r.astype(BF16).astype(F32)
            r_split = jnp.where(hi_lanes, r_hi, r - r_hi).astype(BF16)
            rexp = _dot(r_split, esegt_ref[...])
            store(i, y * rexp * hw)

    @pl.when(kind == 2)
    def _():
        for i in range(seq // pb):
            store(i, tile(i))

    @pl.when(kind == 3)
    def _():
        for i in range(seq // pb):
            store(i, _silu(tile(i)))


def _attn_in_proj(x, norm_w_row, w_bf16, head_w):
    bsz, seq, d = x.shape
    n_out = w_bf16.shape[1]
    tn = d
    dils = tuple(dil for _, dil in DILATED_GROUPS)
    n_heads = d // B_HEAD_DIM
    assert 2 * n_heads <= LANES
    eseg = np.zeros((d, LANES), np.float32)
    eseg[np.arange(d), np.arange(d) // B_HEAD_DIM] = 1.0
    eseg[np.arange(d), n_heads + np.arange(d) // B_HEAD_DIM] = 1.0
    eseg = jnp.asarray(eseg, BF16)
    kern = functools.partial(_bproj_kernel, seq=seq, dils=dils)
    return pl.pallas_call(
        kern,
        grid=(bsz, n_out // tn),
        in_specs=[pl.BlockSpec((1, seq, d), lambda b, j: (b, 0, 0)),
                  pl.BlockSpec((1, d), lambda b, j: (0, 0)),
                  pl.BlockSpec((d, tn), lambda b, j: (0, j)),
                  pl.BlockSpec((1, 1, tn), lambda b, j: (j, 0, 0)),
                  pl.BlockSpec((d, LANES), lambda b, j: (0, 0)),
                  pl.BlockSpec((LANES, d), lambda b, j: (0, 0))],
        out_specs=pl.BlockSpec((None, None, tn // LANES, seq, LANES), lambda b, j: (b, j, 0, 0, 0)),
        out_shape=jax.ShapeDtypeStruct((bsz, n_out // tn, tn // LANES, seq, LANES), BF16),
        scratch_shapes=[pltpu.VMEM((len(dils), seq, d), BF16),
                        pltpu.VMEM((d // LANES, seq, LANES), F32)],
        compiler_params=_params(("arbitrary", "arbitrary")),
        name="attn_in_proj",
    )(x, norm_w_row, w_bf16, head_w, eseg, eseg.T)


def _attn_kernel(slopes_ref, q0, k0, v0, q1, k1, v1, q2, k2, v2, g_ref, o_ref,
                 num_ref, den_ref, max_ref, bias_ref, s_ref, p_ref, *, seq, dils, unrolls):
    hp = pl.program_id(1)
    blk = B_BLOCK
    hd = B_HEAD_DIM
    n_heads = (LANES // hd) * pl.num_programs(1)
    qkv = ((q0, k0, v0), (q1, k1, v1), (q2, k2, v2))
    head_a = lax.broadcasted_iota(jnp.int32, (blk, LANES), 1) < hd


    def token_rows(g, tok0):
        return pl.ds(tok0, blk) if dils[g] == 1 else pl.ds(tok0, blk, stride=dils[g])

    def qk_phase(g, specs, has_prev, slot):
        q_ref, k_ref, _ = qkv[g]
        nk = 2 * blk if has_prev else blk
        for w, (row0, _) in enumerate(specs):
            k_lo = row0 - blk if has_prev else row0
            q2d = q_ref[pl.ds(row0, blk), :]
            zero = jnp.zeros_like(q2d)
            qq = jnp.concatenate([jnp.where(head_a, q2d, zero), jnp.where(head_a, zero, q2d)], axis=0)
            s_ref[slot, w, :, 0:nk] = _dot_nt(qq, k_ref[pl.ds(k_lo, nk), :])

    def softmax_phase(g, specs, has_prev, slot):
        nk = 2 * blk if has_prev else blk
        off = 0 if has_prev else blk
        for w, (_, tok0) in enumerate(specs):
            ms, ls = [], []
            for a in range(2):
                sa = s_ref[slot, w, a * blk:(a + 1) * blk, 0:nk] + bias_ref[a, :, off:off + nk]
                m = jnp.max(sa, axis=-1, keepdims=True)
                e = jnp.exp2(sa - m)
                p_ref[slot, w, a * blk:(a + 1) * blk, 0:nk] = e.astype(BF16)
                ms.append(m)
                ls.append(jnp.sum(e, axis=-1, keepdims=True))
            idx = token_rows(g, tok0)
            max_ref[g, idx, :] = jnp.where(head_a, ms[0], ms[1])
            den_ref[g, idx, :] = jnp.where(head_a, ls[0], ls[1])

    def pv_phase(g, specs, has_prev, slot):
        v_ref = qkv[g][2]
        nk = 2 * blk if has_prev else blk
        for w, (row0, tok0) in enumerate(specs):
            k_lo = row0 - blk if has_prev else row0
            o = _dot(p_ref[slot, w, :, 0:nk], v_ref[pl.ds(k_lo, nk), :])
            num_ref[g, token_rows(g, tok0), :] = jnp.where(head_a, o[0:blk, :], o[blk:2 * blk, :])

    def pipelined(n_iter, make_specs, g, has_prev):
        def step(t):
            static = isinstance(t, int)
            if (not static) or 0 <= t - 2 < n_iter:
                pv_phase(g, make_specs(t - 2), has_prev, t % 2)
            if (not static) or 0 <= t - 1 < n_iter:
                softmax_phase(g, make_specs(t - 1), has_prev, (t - 1) % 2)
            if (not static) or 0 <= t < n_iter:
                qk_phase(g, make_specs(t), has_prev, t % 2)

        for t in range(2):
            step(t)
        if n_iter > 2:
            def body(t, carry):
                step(t)
                return carry

            lax.fori_loop(2, n_iter, body, 0)
        for t in range(max(n_iter, 2), n_iter + 2):
            step(t)

    qi = lax.broadcasted_iota(jnp.int32, (blk, 2 * blk), 0)
    kj = lax.broadcasted_iota(jnp.int32, (blk, 2 * blk), 1)
    dist = blk + qi - kj
    valid = (dist >= 0) & (dist <= blk)

    for g, dil in enumerate(dils):
        sub = seq // dil
        n_blk = sub // blk
        bias_base = jnp.where(valid, -(dist * dil).astype(F32), NEG_BIG)
        for a in range(2):
            slope = slopes_ref[g * n_heads + 2 * hp + a]
            bias_ref[a] = (slope * LOG2E) * bias_base

        u1, u2 = unrolls[g]

        def aligned(row):
            return row if isinstance(row, int) else pl.multiple_of(row, blk)

        def first_specs(i, sub=sub, u1=u1):
            return [(aligned((i * u1 + w) * sub), i * u1 + w) for w in range(u1)]

        def rest_specs(i, dil=dil, sub=sub, n_blk=n_blk, u2=u2):
            specs = []
            for w in range(u2):
                idx = i * u2 + w
                r = idx // (n_blk - 1)
                b = idx % (n_blk - 1) + 1
                specs.append((aligned(r * sub + b * blk), b * blk * dil + r))
            return specs

        pipelined(dil // u1, first_specs, g, False)
        n_rest = dil * (n_blk - 1)
        if n_rest:
            pipelined(n_rest // u2, rest_specs, g, True)

    rb = 256
    n_g = len(dils)

    def merge(i, carry):
        r0 = pl.multiple_of(i * rb, rb)
        ms = [max_ref[g, pl.ds(r0, rb), :] for g in range(n_g)]
        mx = ms[0]
        for g in range(1, n_g):
            mx = jnp.maximum(mx, ms[g])
        num = jnp.zeros((rb, LANES), F32)
        den = jnp.zeros((rb, LANES), F32)
        for g in range(n_g):
            w = jnp.exp2(ms[g] - mx)
            num = num + w * num_ref[g, pl.ds(r0, rb), :]
            den = den + w * den_ref[g, pl.ds(r0, rb), :]
        gate = g_ref[pl.ds(r0, rb), :].astype(F32)
        o_ref[pl.ds(r0, rb), :] = (num / den * gate).astype(BF16)
        return carry

    lax.fori_loop(0, seq // rb, merge, 0)


def _attention(proj, slopes):
    bsz, n_tiles, n_pairs, seq, _ = proj.shape
    assert n_tiles == 3 * B_N_GROUPS + 1
    dils = tuple(dil for _, dil in DILATED_GROUPS)
    unrolls = ((1, 5), (4, 6), (4, 1))
    for (u1, u2), dil in zip(unrolls, dils):
        n_rest = dil * (seq // dil // B_BLOCK - 1)
        assert dil % u1 == 0 and n_rest % u2 == 0

    def col_spec(tile):
        return pl.BlockSpec((None, None, None, seq, LANES), lambda b, h, s, tile=tile: (b, tile, h, 0, 0))

    kern = functools.partial(_attn_kernel, seq=seq, dils=dils, unrolls=unrolls)
    grid_spec = pltpu.PrefetchScalarGridSpec(
        num_scalar_prefetch=1,
        grid=(bsz, n_pairs),
        in_specs=[col_spec(t) for t in range(n_tiles)],
        out_specs=pl.BlockSpec((None, None, seq, LANES), lambda b, h, s: (b, h, 0, 0)),
        scratch_shapes=[pltpu.VMEM((B_N_GROUPS, seq, LANES), F32)] * 3
        + [pltpu.VMEM((2, B_BLOCK, 2 * B_BLOCK), F32),
           pltpu.VMEM((2, max(max(u) for u in unrolls), 2 * B_BLOCK, 2 * B_BLOCK), F32),
           pltpu.VMEM((2, max(max(u) for u in unrolls), 2 * B_BLOCK, 2 * B_BLOCK), BF16)],
    )
    return pl.pallas_call(
        kern,
        grid_spec=grid_spec,
        out_shape=jax.ShapeDtypeStruct((bsz, n_pairs, seq, LANES), BF16),
        compiler_params=_params(("arbitrary", "arbitrary")),
        name="dilated_attention",
    )(slopes, *([proj] * n_tiles))


def _gmlp_kernel(x_ref, yprev_ref, wprev_ref, nw_ref, w_ref, vw_ref, vb_ref, ws_ref, bias_ref, wout_ref, o_ref,
                 x1_ref, h_ref, u_ref, v_ref, g_ref, y_ref, *, tm):
    d = x_ref.shape[-1]
    cg = d // C_GROUPS
    rb = 256
    for i in range(tm // rb):
        rows = slice(i * rb, (i + 1) * rb)
        y_prev = jnp.concatenate([yprev_ref[hp, rows, :] for hp in range(yprev_ref.shape[0])], axis=1)
        x1_ref[rows, :] = x_ref[0, rows, :] + _dot(y_prev, wprev_ref[...])
        h_ref[rows, :] = _rms_rows(x1_ref[rows, :], nw_ref[...]).astype(BF16)
    for i in range(tm // rb):
        rows = slice(i * rb, (i + 1) * rb)
        hh = h_ref[rows, :]
        u_ref[rows, :] = _gelu_tanh(_dot(hh, w_ref[:, 0:d]))
        vv = _gelu_tanh(_dot(hh, w_ref[:, d:2 * d]))
        mu = jnp.mean(vv, axis=-1, keepdims=True)
        vc = vv - mu
        var = jnp.mean(vc * vc, axis=-1, keepdims=True)
        v_ref[rows, :] = (vc * lax.rsqrt(var + NORM_EPS) * vw_ref[...] + vb_ref[...]).astype(BF16)
        g_ref[rows, :] = _silu(_dot(hh, w_ref[:, 2 * d:3 * d]))
    ti = lax.broadcasted_iota(jnp.int32, (C_CHUNK, C_CHUNK), 0)
    si = lax.broadcasted_iota(jnp.int32, (C_CHUNK, C_CHUNK), 1)
    n_ch = tm // C_CHUNK
    for grp in range(C_GROUPS):
        cols = slice(grp * cg, (grp + 1) * cg)
        wc = jnp.where(si <= ti, ws_ref[grp], 0.0).astype(BF16)
        rhs = jnp.concatenate([v_ref[n * C_CHUNK:(n + 1) * C_CHUNK, cols] for n in range(n_ch)], axis=1)
        s = _dot(wc, rhs)
        bias = bias_ref[:, cols]
        for n in range(n_ch):
            rows = slice(n * C_CHUNK, (n + 1) * C_CHUNK)
            sn = s[:, n * cg:(n + 1) * cg] + bias
            y_ref[rows, cols] = (u_ref[rows, cols] * sn * g_ref[rows, cols]).astype(BF16)
    for i in range(tm // rb):
        rows = slice(i * rb, (i + 1) * rb)
        o_ref[0, rows, :] = x1_ref[rows, :] + _dot(y_ref[rows, :], wout_ref[...])


def _gmlp_mixer(x, y_prev_pairs, w_prev_bf16, norm_w_row, w_bf16, vw_row, vb_row, w_s, bias_full, w_out_bf16,
                tm=512):
    bsz, seq, d = x.shape
    n_pairs, lanes = y_prev_pairs.shape[1], y_prev_pairs.shape[3]
    kern = functools.partial(_gmlp_kernel, tm=tm)
    return pl.pallas_call(
        kern,
        grid=(bsz, seq // tm),
        in_specs=[pl.BlockSpec((1, tm, d), lambda b, i: (b, i, 0)),
                  pl.BlockSpec((None, n_pairs, tm, lanes), lambda b, i: (b, 0, i, 0)),
                  pl.BlockSpec((d, d), lambda b, i: (0, 0)),
                  pl.BlockSpec((1, d), lambda b, i: (0, 0)),
                  pl.BlockSpec((d, 3 * d), lambda b, i: (0, 0)),
                  pl.BlockSpec((1, d), lambda b, i: (0, 0)),
                  pl.BlockSpec((1, d), lambda b, i: (0, 0)),
                  pl.BlockSpec(w_s.shape, lambda b, i: (0, 0, 0)),
                  pl.BlockSpec((C_CHUNK, d), lambda b, i: (0, 0)),
                  pl.BlockSpec((d, d), lambda b, i: (0, 0))],
        out_specs=pl.BlockSpec((1, tm, d), lambda b, i: (b, i, 0)),
        out_shape=jax.ShapeDtypeStruct((bsz, seq, d), F32),
        scratch_shapes=[pltpu.VMEM((tm, d), F32),
                        pltpu.VMEM((tm, d), BF16),
                        pltpu.VMEM((tm, d), F32),
                        pltpu.VMEM((tm, d), BF16),
                        pltpu.VMEM((tm, d), F32),
                        pltpu.VMEM((tm, d), BF16)],
        compiler_params=_params(("arbitrary", "arbitrary")),
        name="gmlp_mixer",
    )(x, y_prev_pairs, w_prev_bf16, norm_w_row, w_bf16, vw_row, vb_row, w_s, bias_full, w_out_bf16)


def kernel(x, norm_w, a_w_in, a_lower_bounds, a_o_norm_w, a_w_out, b_w_in, b_q_norm_w, b_k_norm_w, b_w_out,
           c_w_in, c_v_norm_w, c_v_norm_b, c_w_s, c_b_s, c_w_out):
    bsz, seq, d = x.shape
    depth = norm_w.shape[0]
    n_mixers = 3
    assert seq == DILATED_GROUPS[-1][0] and seq % A_CHUNK == 0 and d % LANES == 0

    def residual(y, w_out, xin):
        out = _out_proj(y.reshape(bsz * seq, d), w_out.astype(BF16), xin.reshape(bsz * seq, d))
        return out.reshape(bsz, seq, d)

    pending = None
    for layer in range(depth):
        kind, idx = layer % n_mixers, layer // n_mixers
        nw = norm_w[layer][None, :]
        if kind == 0:
            y = _hgrn_mixer(x, nw, a_w_in[idx].astype(BF16), a_lower_bounds, a_o_norm_w[idx][None, :], idx)
            x = residual(y, a_w_out[idx], x)
        elif kind == 1:
            n_heads = d // B_HEAD_DIM
            n_total = B_N_GROUPS * n_heads
            slopes = jnp.exp2(-ALIBI_MAX_EXP * jnp.arange(1, n_total + 1, dtype=F32) / n_total)
            rows = []
            for g in range(B_N_GROUPS):
                rows.append(jnp.tile(b_q_norm_w[idx, g], n_heads) * (B_HEAD_DIM ** -0.5 * LOG2E))
                rows.append(jnp.tile(b_k_norm_w[idx, g], n_heads))
                rows.append(jnp.ones((d,), F32))
            rows.append(jnp.ones((d,), F32))
            head_w = jnp.stack(rows)[:, None, :]
            proj = _attn_in_proj(x, nw, b_w_in[idx].astype(BF16), head_w)
            pending = (_attention(proj, slopes), b_w_out[idx].astype(BF16))
            if layer == depth - 1:
                x = _out_proj_pairs(*pending, x)
        else:
            bias_full = jnp.repeat(c_b_s[idx].T, d // C_GROUPS, axis=1)
            x = _gmlp_mixer(x, *pending, nw, c_w_in[idx].astype(BF16), c_v_norm_w[idx][None, :],
                            c_v_norm_b[idx][None, :], c_w_s[idx], bias_full, c_w_out[idx].astype(BF16))
    return x
```

```python
import functools

import numpy as np
import jax
import jax.numpy as jnp
from jax import lax
from jax.experimental import pallas as pl
from jax.experimental.pallas import tpu as pltpu

F32 = jnp.float32
BF16 = jnp.bfloat16

NORM_EPS = 1e-6
NEG_BIG = -1e30
LB_FLOOR = 1e-30
LOG2E = float(np.log2(np.e))

LANES = 128
SUBLANES = 8
VMEM_LIMIT_BYTES = 60 * 1024 * 1024

A_HEAD_DIM = 128
A_CHUNK = 128
A_CHUNKS_PER_ITER = 2
A_HEADS_PER_STEP = 4
B_HEAD_DIM = 64
DILATED_GROUPS = ((128, 1), (512, 4), (2048, 16))
B_N_GROUPS = len(DILATED_GROUPS)
B_BLOCK = 128
ALIBI_MAX_EXP = 8.0
C_CHUNK = 128
C_GROUPS = 8


def _silu(x):
    hx = 0.5 * x
    return hx + hx * jnp.tanh(hx)


def _gelu_tanh(x):
    c = np.float32(np.sqrt(2.0 / np.pi))
    return 0.5 * x * (1.0 + jnp.tanh(c * (x + 0.044715 * (x * x * x))))


def _rms_rows(xf, w):
    ms = jnp.mean(xf * xf, axis=-1, keepdims=True)
    return xf * lax.rsqrt(ms + NORM_EPS) * w


def _dot(a, b):
    return jnp.dot(a, b, preferred_element_type=F32)


def _dot_nt(a, b):
    return lax.dot_general(a, b, (((1,), (1,)), ((), ())), preferred_element_type=F32)


def _dot_tn(a, b):
    return lax.dot_general(a, b, (((0,), (0,)), ((), ())), preferred_element_type=F32)


def _params(sem):
    return pltpu.CompilerParams(dimension_semantics=sem, vmem_limit_bytes=VMEM_LIMIT_BYTES)


def _out_proj_kernel(y_ref, w_ref, x_ref, o_ref):
    o_ref[...] = x_ref[...] + _dot(y_ref[...], w_ref[...])


def _out_proj(y2d, w_bf16, x2d, tm=2048):
    m, k = y2d.shape
    n = w_bf16.shape[1]
    return pl.pallas_call(
        _out_proj_kernel,
        grid=(m // tm,),
        in_specs=[pl.BlockSpec((tm, k), lambda i: (i, 0)),
                  pl.BlockSpec((k, n), lambda i: (0, 0)),
                  pl.BlockSpec((tm, n), lambda i: (i, 0))],
        out_specs=pl.BlockSpec((tm, n), lambda i: (i, 0)),
        out_shape=jax.ShapeDtypeStruct((m, n), F32),
        compiler_params=_params(("arbitrary",)),
        name="out_proj",
    )(y2d, w_bf16, x2d)


def _out_proj_pairs_kernel(y_ref, w_ref, x_ref, o_ref):
    y = jnp.concatenate([y_ref[hp] for hp in range(y_ref.shape[0])], axis=1)
    o_ref[...] = x_ref[...] + _dot(y, w_ref[...])


def _out_proj_pairs(y_pairs, w_bf16, x, tm=1024):
    bsz, n_pairs, seq, lanes = y_pairs.shape
    d = n_pairs * lanes
    n = w_bf16.shape[1]
    return pl.pallas_call(
        _out_proj_pairs_kernel,
        grid=(bsz, seq // tm),
        in_specs=[pl.BlockSpec((None, n_pairs, tm, lanes), lambda b, i: (b, 0, i, 0)),
                  pl.BlockSpec((d, n), lambda b, i: (0, 0)),
                  pl.BlockSpec((None, tm, n), lambda b, i: (b, i, 0))],
        out_specs=pl.BlockSpec((None, tm, n), lambda b, i: (b, i, 0)),
        out_shape=jax.ShapeDtypeStruct((bsz, seq, n), F32),
        compiler_params=_params(("arbitrary", "arbitrary")),
        name="out_proj_pairs",
    )(y_pairs, w_bf16, x)


A_BAND = 4


def _hgrn_levels(chunk):
    return [1 << j for j in range(int(np.log2(chunk))) if (2 << j) > A_BAND]


def _hgrn_level_exponents(b, b_view, sgn_ref, chunk):
    out = {}
    n_vregs = chunk // SUBLANES
    width = b.shape[-1]

    def row(r, n):
        one_group = jnp.broadcast_to(b_view[pl.ds(r, 1), :], (SUBLANES, width))
        return jnp.concatenate([one_group] * (n // SUBLANES), axis=0)

    for m in _hgrn_levels(chunk):
        if m >= SUBLANES:
            parts = []
            for j in range(chunk // (2 * m)):
                base = j * 2 * m
                ref_row = row(base + m - 1, m)
                parts.append(ref_row - b[base:base + m, :])
                parts.append(b[base + m:base + 2 * m, :] - ref_row)
            out[m] = jnp.concatenate(parts, axis=0)
        else:
            assert m == 4
            ref = jnp.concatenate([row(SUBLANES * i + 3, SUBLANES) for i in range(n_vregs)], axis=0)
            out[m] = (b - ref) * sgn_ref[...]
    return out


def _hgrn_kernel(x_ref, nw_ref, wq_ref, wf_ref, wi_ref, wg_ref, lbraw_ref, onw_ref, tri_ref, y_ref,
                 h_ref, act_ref, v16_ref, b_ref, st_ref, lvl_ref, band_ref, sgn_ref, qd_ref, kd_ref, a16_ref,
                 dec_ref,
                 *, layer_idx, seq, chunk, hps, cpi):
    step = pl.program_id(1)
    dh = A_HEAD_DIM
    rb = 256

    @pl.when(step == 0)
    def _():
        for i in range(seq // rb):
            xs = x_ref[0, i * rb:(i + 1) * rb, :]
            h_ref[i * rb:(i + 1) * rb, :] = _rms_rows(xs, nw_ref[...]).astype(BF16)

    raw = lbraw_ref[...]
    n_layers = raw.shape[0]
    mx = raw[0:1, :]
    for i in range(1, n_layers):
        mx = jnp.maximum(mx, raw[i:i + 1, :])
    es = [jnp.exp(raw[i:i + 1, :] - mx) for i in range(n_layers)]
    z = es[0]
    for i in range(1, n_layers):
        z = z + es[i]
    soft = [e / z for e in es]
    csum = soft[0]
    for i in range(1, layer_idx + 1):
        csum = csum + soft[i]
    lb = csum - soft[0]
    log_lb = jnp.log(jnp.maximum(lb, LB_FLOOR))
    log1m_lb = jnp.log1p(-lb)
    one_m_lb = 1.0 - lb
    heads = range(hps)

    pb = 512
    for i in range(seq // pb):
        rows = slice(i * pb, (i + 1) * pb)
        hh = h_ref[rows, :]
        res_q, res_f, res_i, res_g = (_dot(hh, w[...]) for w in (wq_ref, wf_ref, wi_ref, wg_ref))
        v16_ref[rows, :] = res_i.astype(BF16)
        for a in heads:
            o0 = a * 4 * dh
            lbs = slice(a * dh, (a + 1) * dh)
            pf = res_f[:, lbs]
            log_sig = jnp.minimum(pf, 0.0) - jnp.log(1.0 + jnp.exp(-jnp.abs(pf)))
            bb = log1m_lb[:, lbs] + log_sig
            ll = log_lb[:, lbs]
            log_f = jnp.maximum(ll, bb) + jnp.log(1.0 + jnp.exp(-jnp.abs(ll - bb)))
            act_ref[rows, o0:o0 + dh] = _silu(res_q[:, lbs])
            act_ref[rows, o0 + dh:o0 + 2 * dh] = one_m_lb[:, lbs] * jnp.exp(log_sig - pf)
            act_ref[rows, o0 + 2 * dh:o0 + 3 * dh] = _silu(res_g[:, lbs])
            act_ref[rows, o0 + 3 * dh:o0 + 4 * dh] = log_f * LOG2E

    ti = lax.broadcasted_iota(jnp.int32, (chunk, chunk), 0)
    si = lax.broadcasted_iota(jnp.int32, (chunk, chunk), 1)
    lvl_ref[...] = jnp.where(si < ti, ti ^ si, jnp.where(si == ti, 0, -1))
    band_ref[...] = jnp.where((si <= ti) & ((ti ^ si) < A_BAND), ti - si, -1)
    tr = lax.broadcasted_iota(jnp.int32, (chunk, dh), 0)
    sgn_ref[...] = jnp.where((tr & 4) != 0, 1.0, -1.0)
    st_ref[...] = jnp.zeros_like(st_ref)
    onw = onw_ref[...]
    levels = _hgrn_levels(chunk)

    def row_start(c):
        return c * chunk if isinstance(c, int) else pl.multiple_of(c * chunk, chunk)

    def front(chunks):
        q, k, log_f = [], [], []
        for c in chunks:
            for a in heads:
                o0 = a * 4 * dh
                q.append(act_ref[pl.ds(row_start(c), chunk), o0:o0 + dh])
                k.append(act_ref[pl.ds(row_start(c), chunk), o0 + dh:o0 + 2 * dh])
                log_f.append(act_ref[pl.ds(row_start(c), chunk), o0 + 3 * dh:o0 + 4 * dh])
        slots = range(len(q))

        b = []
        for a in slots:
            lf_hi = log_f[a].astype(BF16)
            lf_lo = (log_f[a] - lf_hi.astype(F32)).astype(BF16)
            b.append(_dot(tri_ref[...], lf_hi) + _dot(tri_ref[...], lf_lo))
            b_ref[a] = b[a]

        expo, acc = [], []
        for a in slots:
            expo.append(_hgrn_level_exponents(b[a], b_ref.at[a], sgn_ref, chunk))
            qd_ref[a] = (q[a] * jnp.exp2(b[a])).astype(BF16)
            b_last = b_ref[a, chunk - 1:chunk, :]
            kd_ref[a] = (k[a] * jnp.exp2(b_last - b[a])).astype(BF16)
            dec_ref[a] = jnp.exp2(b_last)
        n_grp = chunk // SUBLANES

        def shifted(x, dist):
            return pltpu.roll(x.reshape(n_grp, SUBLANES, dh), dist, axis=1).reshape(chunk, dh)

        for a in slots:
            acc.append(jnp.where(band_ref[...] == 0, jnp.sum(q[a] * k[a], axis=-1, keepdims=True), 0.0))
        for dist in range(1, A_BAND):
            for a in slots:
                decay = jnp.exp2(b[a] - shifted(b[a], dist))
                c = jnp.sum(q[a] * shifted(k[a], dist) * decay, axis=-1, keepdims=True)
                acc[a] = jnp.where(band_ref[...] == dist, c, acc[a])
        for m in levels:
            if m >= SUBLANES:
                break
            is_query_row = sgn_ref[...] > 0.0
            for a in slots:
                w = (jnp.where(is_query_row, q[a], k[a]) * jnp.exp2(expo[a][m])).astype(BF16)
                acc[a] = jnp.where(lvl_ref[...] >= m, _dot_nt(w, w), acc[a])
        n_grp = chunk // SUBLANES
        acc = [[acc_a[SUBLANES * i:SUBLANES * (i + 1), :] for i in range(n_grp)] for acc_a in acc]
        for m in levels:
            if m < SUBLANES:
                continue
            is_query = [(SUBLANES * i) % (2 * m) >= m for i in range(n_grp)]
            for a in slots:
                roles = jnp.concatenate([(q[a] if is_query[i] else k[a])[SUBLANES * i:SUBLANES * (i + 1), :]
                                         for i in range(n_grp)], axis=0)
                w = roles * jnp.exp2(expo[a][m])
                q_m = jnp.concatenate([w[SUBLANES * i:SUBLANES * (i + 1), :]
                                       for i in range(n_grp) if is_query[i]], axis=0)
                a_m = _dot_nt(q_m.astype(BF16), w.astype(BF16))
                for n, i in enumerate([i for i in range(n_grp) if is_query[i]]):
                    rows = slice(SUBLANES * i, SUBLANES * (i + 1))
                    acc[a][i] = jnp.where(lvl_ref[rows, :] >= m,
                                          a_m[SUBLANES * n:SUBLANES * (n + 1), :], acc[a][i])
        for a in slots:
            a16_ref[a] = jnp.concatenate(acc[a], axis=0).astype(BF16)

    def back(chunks):
        for n, c in enumerate(chunks):
            r0 = row_start(c)
            for a in heads:
                o0 = a * 4 * dh
                slot = n * hps + a
                v16 = v16_ref[pl.ds(r0, chunk), a * dh:(a + 1) * dh]
                st = st_ref[a]
                oa = _dot_nt(qd_ref[slot], st.astype(BF16)) + _dot(a16_ref[slot], v16)
                st_ref[a] = st * dec_ref[slot] + _dot_tn(v16, kd_ref[slot])
                y = _rms_rows(oa, onw) * act_ref[pl.ds(r0, chunk), o0 + 2 * dh:o0 + 3 * dh]
                y_ref[0, pl.ds(r0, chunk), a * dh:(a + 1) * dh] = y.astype(BF16)

    n_iter = seq // (chunk * cpi)
    front([n for n in range(cpi)])

    def body(i, carry):
        back([(i - 1) * cpi + n for n in range(cpi)])
        front([i * cpi + n for n in range(cpi)])
        return carry

    lax.fori_loop(1, n_iter, body, 0)
    back([(n_iter - 1) * cpi + n for n in range(cpi)])


def _hgrn_mixer(x, norm_w_row, w_in, lb_raw, o_norm_w_row, layer_idx):
    bsz, seq, d = x.shape
    dh = A_HEAD_DIM
    hps = A_HEADS_PER_STEP
    n_steps = d // (dh * hps)
    chunk = A_CHUNK
    tri = jnp.asarray(np.tril(np.ones((chunk, chunk), np.float32)), BF16)
    cpi = A_CHUNKS_PER_ITER
    n_slots = hps * cpi
    assert seq % (chunk * cpi) == 0
    kern = functools.partial(_hgrn_kernel, layer_idx=layer_idx, seq=seq, chunk=chunk, hps=hps, cpi=cpi)

    def w_spec(part):
        return pl.BlockSpec((d, hps * dh), lambda b, h, part=part: (0, part * n_steps + h))

    return pl.pallas_call(
        kern,
        grid=(bsz, n_steps),
        in_specs=[pl.BlockSpec((1, seq, d), lambda b, h: (b, 0, 0)),
                  pl.BlockSpec((1, d), lambda b, h: (0, 0)),
                  w_spec(0), w_spec(1), w_spec(2), w_spec(3),
                  pl.BlockSpec((lb_raw.shape[0], hps * dh), lambda b, h: (0, h)),
                  pl.BlockSpec((1, dh), lambda b, h: (0, 0)),
                  pl.BlockSpec((chunk, chunk), lambda b, h: (0, 0))],
        out_specs=pl.BlockSpec((1, seq, hps * dh), lambda b, h: (b, 0, h)),
        out_shape=jax.ShapeDtypeStruct((bsz, seq, d), BF16),
        scratch_shapes=[pltpu.VMEM((seq, d), BF16),
                        pltpu.VMEM((seq, hps * 4 * dh), F32),
                        pltpu.VMEM((seq, hps * dh), BF16),
                        pltpu.VMEM((n_slots, chunk, dh), F32),
                        pltpu.VMEM((hps, dh, dh), F32),
                        pltpu.VMEM((chunk, chunk), jnp.int32),
                        pltpu.VMEM((chunk, chunk), jnp.int32),
                        pltpu.VMEM((chunk, dh), F32),
                        pltpu.VMEM((n_slots, chunk, dh), BF16),
                        pltpu.VMEM((n_slots, chunk, dh), BF16),
                        pltpu.VMEM((n_slots, chunk, chunk), BF16),
                        pltpu.VMEM((n_slots, 1, dh), F32)],
        compiler_params=_params(("arbitrary", "arbitrary")),
        name="hgrn_mixer",
    )(x, norm_w_row, w_in, w_in, w_in, w_in, lb_raw, o_norm_w_row, tri)


def _bproj_kernel(x_ref, nw_ref, w_ref, wkt_ref, hw_ref, eseg_ref, esegt_ref, o_ref, h_ref, hn_ref,
                  *, seq, dils):
    j = pl.program_id(1)
    n_qkv_tiles = 3 * len(dils)

    @pl.when(j == 0)
    def _():
        d = x_ref.shape[-1]
        n_planes = d // LANES
        rb = 256
        for i in range(seq // rb):
            hn = _rms_rows(x_ref[0, i * rb:(i + 1) * rb, :], nw_ref[...])
            for c in range(n_planes):
                hn_ref[c, i * rb:(i + 1) * rb, :] = hn[:, c * LANES:(c + 1) * LANES]
        for g, dil in enumerate(dils):
            sub = seq // dil
            for r in range(dil):
                for c0 in range(0, sub, rb):
                    rows = min(rb, sub - c0)
                    if dil == 1:
                        idx = pl.ds(c0, rows)
                    else:
                        idx = pl.ds(r + c0 * dil, rows, stride=dil)
                    for c in range(n_planes):
                        h_ref[g, r * sub + c0:r * sub + c0 + rows, c * LANES:(c + 1) * LANES] = (
                            hn_ref[c, idx, :].astype(BF16))

    grp = jnp.where(j < n_qkv_tiles, j // 3, 0)
    kind = jnp.where(j < n_qkv_tiles, j % 3, 3)
    pb = 512

    def tile(i):
        return _dot(h_ref[grp, i * pb:(i + 1) * pb, :], w_ref[...])

    def store(i, val):
        for hp in range(val.shape[-1] // LANES):
            o_ref[hp, i * pb:(i + 1) * pb, :] = val[:, hp * LANES:(hp + 1) * LANES].astype(BF16)

    @pl.when(kind == 1)
    def _():
        d = x_ref.shape[-1]
        hd = B_HEAD_DIM
        for i in range(seq // pb):
            yt = _dot_nt(wkt_ref[...], h_ref[grp, i * pb:(i + 1) * pb, :])
            for h in range(d // hd):
                yh = yt[h * hd:(h + 1) * hd, :]
                r = lax.rsqrt(jnp.mean(yh * yh, axis=0, keepdims=True) + NORM_EPS)
                yh = (yh * r).astype(BF16)
                hp, half = divmod(h * hd, LANES)
                for kb in range(pb // LANES):
                    row0 = (i * (pb // LANES) + kb) * LANES + half
                    o_ref[hp, row0:row0 + hd, :] = yh[:, kb * LANES:(kb + 1) * LANES]

    @pl.when(kind == 0)
    def _():
        hw = hw_ref[0]
        n_heads = x_ref.shape[-1] // B_HEAD_DIM
        hi_lanes = lax.broadcasted_iota(jnp.int32, (pb, LANES), 1) < n_heads
        for i in range(seq // pb):
            y = tile(i)
            ssq = _dot((y * y).astype(BF16), eseg_ref[...])
            r = lax.rsqrt(ssq * (1.0 / B_HEAD_DIM) + NORM_EPS)
            r_hi = r.astype(BF16).astype(F32)
            r_split = jnp.where(hi_lanes, r_hi, r - r_hi).astype(BF16)
            rexp = _dot(r_split, esegt_ref[...])
            store(i, y * rexp * hw)

    @pl.when(kind == 2)
    def _():
        for i in range(seq // pb):
            store(i, tile(i))

    @pl.when(kind == 3)
    def _():
        for i in range(seq // pb):
            store(i, _silu(tile(i)))


def _attn_in_proj(x, norm_w_row, w_bf16, head_w):
    bsz, seq, d = x.shape
    n_out = w_bf16.shape[1]
    tn = d
    dils = tuple(dil for _, dil in DILATED_GROUPS)
    n_grp = len(dils)
    w_k_t = jnp.stack([w_bf16[:, (3 * g + 1) * tn:(3 * g + 2) * tn].T for g in range(n_grp)])
    n_heads = d // B_HEAD_DIM
    assert 2 * n_heads <= LANES
    eseg = np.zeros((d, LANES), np.float32)
    eseg[np.arange(d), np.arange(d) // B_HEAD_DIM] = 1.0
    eseg[np.arange(d), n_heads + np.arange(d) // B_HEAD_DIM] = 1.0
    eseg = jnp.asarray(eseg, BF16)
    kern = functools.partial(_bproj_kernel, seq=seq, dils=dils)
    return pl.pallas_call(
        kern,
        grid=(bsz, n_out // tn),
        in_specs=[pl.BlockSpec((1, seq, d), lambda b, j: (b, 0, 0)),
                  pl.BlockSpec((1, d), lambda b, j: (0, 0)),
                  pl.BlockSpec((d, tn), lambda b, j: (0, j)),
                  pl.BlockSpec((None, tn, d), lambda b, j: (jnp.minimum(j // 3, n_grp - 1), 0, 0)),
                  pl.BlockSpec((1, 1, tn), lambda b, j: (j, 0, 0)),
                  pl.BlockSpec((d, LANES), lambda b, j: (0, 0)),
                  pl.BlockSpec((LANES, d), lambda b, j: (0, 0))],
        out_specs=pl.BlockSpec((None, None, tn // LANES, seq, LANES), lambda b, j: (b, j, 0, 0, 0)),
        out_shape=jax.ShapeDtypeStruct((bsz, n_out // tn, tn // LANES, seq, LANES), BF16),
        scratch_shapes=[pltpu.VMEM((len(dils), seq, d), BF16),
                        pltpu.VMEM((d // LANES, seq, LANES), F32)],
        compiler_params=_params(("arbitrary", "arbitrary")),
        name="attn_in_proj",
    )(x, norm_w_row, w_bf16, w_k_t, head_w, eseg, eseg.T)


def _attn_kernel(slopes_ref, q0, k0, v0, q1, k1, v1, q2, k2, v2, g_ref, o_ref,
                 num_ref, den_ref, max_ref, bias_ref, s_ref, p_ref, *, seq, dils, unrolls):
    hp = pl.program_id(1)
    blk = B_BLOCK
    hd = B_HEAD_DIM
    n_heads = (LANES // hd) * pl.num_programs(1)
    qkv = ((q0, k0, v0), (q1, k1, v1), (q2, k2, v2))
    head_a = lax.broadcasted_iota(jnp.int32, (blk, LANES), 1) < hd


    def token_rows(g, tok0):
        return pl.ds(tok0, blk) if dils[g] == 1 else pl.ds(tok0, blk, stride=dils[g])

    def qk_phase(g, specs, has_prev, slot):
        q_ref, kt_ref, _ = qkv[g]
        nk = 2 * blk if has_prev else blk
        for w, (row0, _) in enumerate(specs):
            q2d = q_ref[pl.ds(row0, blk), :]
            zero = jnp.zeros_like(q2d)
            qq = jnp.concatenate([jnp.where(head_a, q2d, zero), jnp.where(head_a, zero, q2d)], axis=0)
            kt = kt_ref[pl.ds(row0, blk), :]
            if has_prev:
                kt = jnp.concatenate([kt_ref[pl.ds(row0 - blk, blk), :], kt], axis=1)
            s_ref[slot, w, :, 0:nk] = _dot(qq, kt)

    def softmax_phase(g, specs, has_prev, slot):
        nk = 2 * blk if has_prev else blk
        off = 0 if has_prev else blk
        for w, (_, tok0) in enumerate(specs):
            ms, ls = [], []
            for a in range(2):
                sa = s_ref[slot, w, a * blk:(a + 1) * blk, 0:nk] + bias_ref[a, :, off:off + nk]
                m = jnp.max(sa, axis=-1, keepdims=True)
                e = jnp.exp2(sa - m)
                p_ref[slot, w, a * blk:(a + 1) * blk, 0:nk] = e.astype(BF16)
                ms.append(m)
                ls.append(jnp.sum(e, axis=-1, keepdims=True))
            idx = token_rows(g, tok0)
            max_ref[g, idx, :] = jnp.where(head_a, ms[0], ms[1])
            den_ref[g, idx, :] = jnp.where(head_a, ls[0], ls[1])

    def pv_phase(g, specs, has_prev, slot):
        v_ref = qkv[g][2]
        nk = 2 * blk if has_prev else blk
        for w, (row0, tok0) in enumerate(specs):
            k_lo = row0 - blk if has_prev else row0
            o = _dot(p_ref[slot, w, :, 0:nk], v_ref[pl.ds(k_lo, nk), :])
            num_ref[g, token_rows(g, tok0), :] = jnp.where(head_a, o[0:blk, :], o[blk:2 * blk, :])

    def pipelined(n_iter, make_specs, g, has_prev):
        def step(t):
            static = isinstance(t, int)
            if (not static) or 0 <= t - 2 < n_iter:
                pv_phase(g, make_specs(t - 2), has_prev, t % 2)
            if (not static) or 0 <= t - 1 < n_iter:
                softmax_phase(g, make_specs(t - 1), has_prev, (t - 1) % 2)
            if (not static) or 0 <= t < n_iter:
                qk_phase(g, make_specs(t), has_prev, t % 2)

        for t in range(2):
            step(t)
        if n_iter > 2:
            def body(t, carry):
                step(t)
                return carry

            lax.fori_loop(2, n_iter, body, 0)
        for t in range(max(n_iter, 2), n_iter + 2):
            step(t)

    qi = lax.broadcasted_iota(jnp.int32, (blk, 2 * blk), 0)
    kj = lax.broadcasted_iota(jnp.int32, (blk, 2 * blk), 1)
    dist = blk + qi - kj
    valid = (dist >= 0) & (dist <= blk)

    for g, dil in enumerate(dils):
        sub = seq // dil
        n_blk = sub // blk
        bias_base = jnp.where(valid, -(dist * dil).astype(F32), NEG_BIG)
        for a in range(2):
            slope = slopes_ref[g * n_heads + 2 * hp + a]
            bias_ref[a] = (slope * LOG2E) * bias_base

        u1, u2 = unrolls[g]

        def aligned(row):
            return row if isinstance(row, int) else pl.multiple_of(row, blk)

        def first_specs(i, sub=sub, u1=u1):
            return [(aligned((i * u1 + w) * sub), i * u1 + w) for w in range(u1)]

        def rest_specs(i, dil=dil, sub=sub, n_blk=n_blk, u2=u2):
            specs = []
            for w in range(u2):
                idx = i * u2 + w
                r = idx // (n_blk - 1)
                b = idx % (n_blk - 1) + 1
                specs.append((aligned(r * sub + b * blk), b * blk * dil + r))
            return specs

        pipelined(dil // u1, first_specs, g, False)
        n_rest = dil * (n_blk - 1)
        if n_rest:
            pipelined(n_rest // u2, rest_specs, g, True)

    rb = 256
    n_g = len(dils)

    def merge(i, carry):
        r0 = pl.multiple_of(i * rb, rb)
        ms = [max_ref[g, pl.ds(r0, rb), :] for g in range(n_g)]
        mx = ms[0]
        for g in range(1, n_g):
            mx = jnp.maximum(mx, ms[g])
        num = jnp.zeros((rb, LANES), F32)
        den = jnp.zeros((rb, LANES), F32)
        for g in range(n_g):
            w = jnp.exp2(ms[g] - mx)
            num = num + w * num_ref[g, pl.ds(r0, rb), :]
            den = den + w * den_ref[g, pl.ds(r0, rb), :]
        gate = g_ref[pl.ds(r0, rb), :].astype(F32)
        o_ref[pl.ds(r0, rb), :] = (num / den * gate).astype(BF16)
        return carry

    lax.fori_loop(0, seq // rb, merge, 0)


def _attention(proj, slopes):
    bsz, n_tiles, n_pairs, seq, _ = proj.shape
    assert n_tiles == 3 * B_N_GROUPS + 1
    dils = tuple(dil for _, dil in DILATED_GROUPS)
    unrolls = ((1, 5), (4, 6), (4, 1))
    for (u1, u2), dil in zip(unrolls, dils):
        n_rest = dil * (seq // dil // B_BLOCK - 1)
        assert dil % u1 == 0 and n_rest % u2 == 0

    def col_spec(tile):
        return pl.BlockSpec((None, None, None, seq, LANES), lambda b, h, s, tile=tile: (b, tile, h, 0, 0))

    kern = functools.partial(_attn_kernel, seq=seq, dils=dils, unrolls=unrolls)
    grid_spec = pltpu.PrefetchScalarGridSpec(
        num_scalar_prefetch=1,
        grid=(bsz, n_pairs),
        in_specs=[col_spec(t) for t in range(n_tiles)],
        out_specs=pl.BlockSpec((None, None, seq, LANES), lambda b, h, s: (b, h, 0, 0)),
        scratch_shapes=[pltpu.VMEM((B_N_GROUPS, seq, LANES), F32)] * 3
        + [pltpu.VMEM((2, B_BLOCK, 2 * B_BLOCK), F32),
           pltpu.VMEM((2, max(max(u) for u in unrolls), 2 * B_BLOCK, 2 * B_BLOCK), F32),
           pltpu.VMEM((2, max(max(u) for u in unrolls), 2 * B_BLOCK, 2 * B_BLOCK), BF16)],
    )
    return pl.pallas_call(
        kern,
        grid_spec=grid_spec,
        out_shape=jax.ShapeDtypeStruct((bsz, n_pairs, seq, LANES), BF16),
        compiler_params=_params(("arbitrary", "arbitrary")),
        name="dilated_attention",
    )(slopes, *([proj] * n_tiles))


def _gmlp_kernel(x_ref, yprev_ref, wprev_ref, nw_ref, w_ref, vw_ref, vb_ref, ws_ref, bias_ref, wout_ref, o_ref,
                 x1_ref, h_ref, u_ref, v_ref, g_ref, y_ref, *, tm):
    d = x_ref.shape[-1]
    cg = d // C_GROUPS
    rb = 256
    for i in range(tm // rb):
        rows = slice(i * rb, (i + 1) * rb)
        y_prev = jnp.concatenate([yprev_ref[hp, rows, :] for hp in range(yprev_ref.shape[0])], axis=1)
        x1_ref[rows, :] = x_ref[0, rows, :] + _dot(y_prev, wprev_ref[...])
        h_ref[rows, :] = _rms_rows(x1_ref[rows, :], nw_ref[...]).astype(BF16)
    for i in range(tm // rb):
        rows = slice(i * rb, (i + 1) * rb)
        hh = h_ref[rows, :]
        u_ref[rows, :] = _gelu_tanh(_dot(hh, w_ref[:, 0:d]))
        vv = _gelu_tanh(_dot(hh, w_ref[:, d:2 * d]))
        mu = jnp.mean(vv, axis=-1, keepdims=True)
        vc = vv - mu
        var = jnp.mean(vc * vc, axis=-1, keepdims=True)
        v_ref[rows, :] = (vc * lax.rsqrt(var + NORM_EPS) * vw_ref[...] + vb_ref[...]).astype(BF16)
        g_ref[rows, :] = _silu(_dot(hh, w_ref[:, 2 * d:3 * d]))
    ti = lax.broadcasted_iota(jnp.int32, (C_CHUNK, C_CHUNK), 0)
    si = lax.broadcasted_iota(jnp.int32, (C_CHUNK, C_CHUNK), 1)
    n_ch = tm // C_CHUNK
    for grp in range(C_GROUPS):
        cols = slice(grp * cg, (grp + 1) * cg)
        wc = jnp.where(si <= ti, ws_ref[grp], 0.0).astype(BF16)
        rhs = jnp.concatenate([v_ref[n * C_CHUNK:(n + 1) * C_CHUNK, cols] for n in range(n_ch)], axis=1)
        s = _dot(wc, rhs)
        bias = bias_ref[:, cols]
        for n in range(n_ch):
            rows = slice(n * C_CHUNK, (n + 1) * C_CHUNK)
            sn = s[:, n * cg:(n + 1) * cg] + bias
            y_ref[rows, cols] = (u_ref[rows, cols] * sn * g_ref[rows, cols]).astype(BF16)
    for i in range(tm // rb):
        rows = slice(i * rb, (i + 1) * rb)
        o_ref[0, rows, :] = x1_ref[rows, :] + _dot(y_ref[rows, :], wout_ref[...])


def _gmlp_mixer(x, y_prev_pairs, w_prev_bf16, norm_w_row, w_bf16, vw_row, vb_row, w_s, bias_full, w_out_bf16,
                tm=512):
    bsz, seq, d = x.shape
    n_pairs, lanes = y_prev_pairs.shape[1], y_prev_pairs.shape[3]
    kern = functools.partial(_gmlp_kernel, tm=tm)
    return pl.pallas_call(
        kern,
        grid=(bsz, seq // tm),
        in_specs=[pl.BlockSpec((1, tm, d), lambda b, i: (b, i, 0)),
                  pl.BlockSpec((None, n_pairs, tm, lanes), lambda b, i: (b, 0, i, 0)),
                  pl.BlockSpec((d, d), lambda b, i: (0, 0)),
                  pl.BlockSpec((1, d), lambda b, i: (0, 0)),
                  pl.BlockSpec((d, 3 * d), lambda b, i: (0, 0)),
                  pl.BlockSpec((1, d), lambda b, i: (0, 0)),
                  pl.BlockSpec((1, d), lambda b, i: (0, 0)),
                  pl.BlockSpec(w_s.shape, lambda b, i: (0, 0, 0)),
                  pl.BlockSpec((C_CHUNK, d), lambda b, i: (0, 0)),
                  pl.BlockSpec((d, d), lambda b, i: (0, 0))],
        out_specs=pl.BlockSpec((1, tm, d), lambda b, i: (b, i, 0)),
        out_shape=jax.ShapeDtypeStruct((bsz, seq, d), F32),
        scratch_shapes=[pltpu.VMEM((tm, d), F32),
                        pltpu.VMEM((tm, d), BF16),
                        pltpu.VMEM((tm, d), F32),
                        pltpu.VMEM((tm, d), BF16),
                        pltpu.VMEM((tm, d), F32),
                        pltpu.VMEM((tm, d), BF16)],
        compiler_params=_params(("arbitrary", "arbitrary")),
        name="gmlp_mixer",
    )(x, y_prev_pairs, w_prev_bf16, norm_w_row, w_bf16, vw_row, vb_row, w_s, bias_full, w_out_bf16)


def kernel(x, norm_w, a_w_in, a_lower_bounds, a_o_norm_w, a_w_out, b_w_in, b_q_norm_w, b_k_norm_w, b_w_out,
           c_w_in, c_v_norm_w, c_v_norm_b, c_w_s, c_b_s, c_w_out):
    bsz, seq, d = x.shape
    depth = norm_w.shape[0]
    n_mixers = 3
    assert seq == DILATED_GROUPS[-1][0] and seq % A_CHUNK == 0 and d % LANES == 0

    def residual(y, w_out, xin):
        out = _out_proj(y.reshape(bsz * seq, d), w_out.astype(BF16), xin.reshape(bsz * seq, d))
        return out.reshape(bsz, seq, d)

    pending = None
    for layer in range(depth):
        kind, idx = layer % n_mixers, layer // n_mixers
        nw = norm_w[layer][None, :]
        if kind == 0:
            y = _hgrn_mixer(x, nw, a_w_in[idx].astype(BF16), a_lower_bounds, a_o_norm_w[idx][None, :], idx)
            x = residual(y, a_w_out[idx], x)
        elif kind == 1:
            n_heads = d // B_HEAD_DIM
            n_total = B_N_GROUPS * n_heads
            slopes = jnp.exp2(-ALIBI_MAX_EXP * jnp.arange(1, n_total + 1, dtype=F32) / n_total)
            rows = []
            for g in range(B_N_GROUPS):
                rows.append(jnp.tile(b_q_norm_w[idx, g] * b_k_norm_w[idx, g], n_heads)
                            * (B_HEAD_DIM ** -0.5 * LOG2E))
                rows.append(jnp.ones((d,), F32))
                rows.append(jnp.ones((d,), F32))
            rows.append(jnp.ones((d,), F32))
            head_w = jnp.stack(rows)[:, None, :]
            proj = _attn_in_proj(x, nw, b_w_in[idx].astype(BF16), head_w)
            pending = (_attention(proj, slopes), b_w_out[idx].astype(BF16))
            if layer == depth - 1:
                x = _out_proj_pairs(*pending, x)
        else:
            bias_full = jnp.repeat(c_b_s[idx].T, d // C_GROUPS, axis=1)
            x = _gmlp_mixer(x, *pending, nw, c_w_in[idx].astype(BF16), c_v_norm_w[idx][None, :],
                            c_v_norm_b[idx][None, :], c_w_s[idx], bias_full, c_w_out[idx].astype(BF16))
    return x
```

```python
import functools

import numpy as np
import jax
import jax.numpy as jnp
from jax import lax
from jax.experimental import pallas as pl
from jax.experimental.pallas import tpu as pltpu

F32 = jnp.float32
BF16 = jnp.bfloat16

NORM_EPS = 1e-6
NEG_BIG = -1e30
LB_FLOOR = 1e-30
LOG2E = float(np.log2(np.e))

LANES = 128
SUBLANES = 8
VMEM_LIMIT_BYTES = 60 * 1024 * 1024

A_HEAD_DIM = 128
A_CHUNK = 128
A_CHUNKS_PER_ITER = 2
A_HEADS_PER_STEP = 4
B_HEAD_DIM = 64
DILATED_GROUPS = ((128, 1), (512, 4), (2048, 16))
B_N_GROUPS = len(DILATED_GROUPS)
B_BLOCK = 128
ALIBI_MAX_EXP = 8.0
C_CHUNK = 128
C_GROUPS = 8


def _silu(x):
    hx = 0.5 * x
    return hx + hx * jnp.tanh(hx)


def _gelu_tanh(x):
    c = np.float32(np.sqrt(2.0 / np.pi))
    return 0.5 * x * (1.0 + jnp.tanh(c * (x + 0.044715 * (x * x * x))))


def _rms_rows(xf, w):
    ms = jnp.mean(xf * xf, axis=-1, keepdims=True)
    return xf * lax.rsqrt(ms + NORM_EPS) * w


def _dot(a, b):
    return jnp.dot(a, b, preferred_element_type=F32)


def _dot_nt(a, b):
    return lax.dot_general(a, b, (((1,), (1,)), ((), ())), preferred_element_type=F32)


def _dot_tn(a, b):
    return lax.dot_general(a, b, (((0,), (0,)), ((), ())), preferred_element_type=F32)


def _params(sem):
    return pltpu.CompilerParams(dimension_semantics=sem, vmem_limit_bytes=VMEM_LIMIT_BYTES)


def _out_proj_kernel(y_ref, w_ref, x_ref, o_ref):
    o_ref[...] = x_ref[...] + _dot(y_ref[...], w_ref[...])


def _out_proj(y2d, w_bf16, x2d, tm=2048):
    m, k = y2d.shape
    n = w_bf16.shape[1]
    return pl.pallas_call(
        _out_proj_kernel,
        grid=(m // tm,),
        in_specs=[pl.BlockSpec((tm, k), lambda i: (i, 0)),
                  pl.BlockSpec((k, n), lambda i: (0, 0)),
                  pl.BlockSpec((tm, n), lambda i: (i, 0))],
        out_specs=pl.BlockSpec((tm, n), lambda i: (i, 0)),
        out_shape=jax.ShapeDtypeStruct((m, n), F32),
        compiler_params=_params(("arbitrary",)),
        name="out_proj",
    )(y2d, w_bf16, x2d)


def _out_proj_pairs_kernel(y_ref, w_ref, x_ref, o_ref):
    y = jnp.concatenate([y_ref[hp] for hp in range(y_ref.shape[0])], axis=1)
    o_ref[...] = x_ref[...] + _dot(y, w_ref[...])


def _out_proj_pairs(y_pairs, w_bf16, x, tm=1024):
    bsz, n_pairs, seq, lanes = y_pairs.shape
    d = n_pairs * lanes
    n = w_bf16.shape[1]
    return pl.pallas_call(
        _out_proj_pairs_kernel,
        grid=(bsz, seq // tm),
        in_specs=[pl.BlockSpec((None, n_pairs, tm, lanes), lambda b, i: (b, 0, i, 0)),
                  pl.BlockSpec((d, n), lambda b, i: (0, 0)),
                  pl.BlockSpec((None, tm, n), lambda b, i: (b, i, 0))],
        out_specs=pl.BlockSpec((None, tm, n), lambda b, i: (b, i, 0)),
        out_shape=jax.ShapeDtypeStruct((bsz, seq, n), F32),
        compiler_params=_params(("arbitrary", "arbitrary")),
        name="out_proj_pairs",
    )(y_pairs, w_bf16, x)


A_BAND = 4


def _hgrn_levels(chunk):
    return [1 << j for j in range(int(np.log2(chunk))) if (2 << j) > A_BAND]


def _hgrn_level_exponents(b, b_view, sgn_ref, chunk):
    out = {}
    n_vregs = chunk // SUBLANES
    width = b.shape[-1]

    def row(r, n):
        one_group = jnp.broadcast_to(b_view[pl.ds(r, 1), :], (SUBLANES, width))
        return jnp.concatenate([one_group] * (n // SUBLANES), axis=0)

    for m in _hgrn_levels(chunk):
        if m >= SUBLANES:
            parts = []
            for j in range(chunk // (2 * m)):
                base = j * 2 * m
                ref_row = row(base + m - 1, m)
                parts.append(ref_row - b[base:base + m, :])
                parts.append(b[base + m:base + 2 * m, :] - ref_row)
            out[m] = jnp.concatenate(parts, axis=0)
        else:
            assert m == 4
            ref = jnp.concatenate([row(SUBLANES * i + 3, SUBLANES) for i in range(n_vregs)], axis=0)
            out[m] = (b - ref) * sgn_ref[...]
    return out


def _hgrn_kernel(x_ref, nw_ref, wq_ref, wf_ref, wi_ref, wg_ref, lbraw_ref, onw_ref, tri_ref, y_ref,
                 h_ref, act_ref, v16_ref, b_ref, st_ref, lvl_ref, band_ref, sgn_ref, qd_ref, kd_ref, a16_ref,
                 dec_ref,
                 *, layer_idx, seq, chunk, hps, cpi):
    step = pl.program_id(1)
    dh = A_HEAD_DIM
    rb = 256

    @pl.when(step == 0)
    def _():
        for i in range(seq // rb):
            xs = x_ref[0, i * rb:(i + 1) * rb, :]
            h_ref[i * rb:(i + 1) * rb, :] = _rms_rows(xs, nw_ref[...]).astype(BF16)

    raw = lbraw_ref[...]
    n_layers = raw.shape[0]
    mx = raw[0:1, :]
    for i in range(1, n_layers):
        mx = jnp.maximum(mx, raw[i:i + 1, :])
    es = [jnp.exp(raw[i:i + 1, :] - mx) for i in range(n_layers)]
    z = es[0]
    for i in range(1, n_layers):
        z = z + es[i]
    soft = [e / z for e in es]
    csum = soft[0]
    for i in range(1, layer_idx + 1):
        csum = csum + soft[i]
    lb = csum - soft[0]
    log_lb = jnp.log(jnp.maximum(lb, LB_FLOOR))
    log1m_lb = jnp.log1p(-lb)
    one_m_lb = 1.0 - lb
    heads = range(hps)

    pb = 512
    for i in range(seq // pb):
        rows = slice(i * pb, (i + 1) * pb)
        hh = h_ref[rows, :]
        res_q, res_f, res_i, res_g = (_dot(hh, w[...]) for w in (wq_ref, wf_ref, wi_ref, wg_ref))
        v16_ref[rows, :] = res_i.astype(BF16)
        for a in heads:
            o0 = a * 4 * dh
            lbs = slice(a * dh, (a + 1) * dh)
            pf = res_f[:, lbs]
            log_sig = jnp.minimum(pf, 0.0) - jnp.log(1.0 + jnp.exp(-jnp.abs(pf)))
            bb = log1m_lb[:, lbs] + log_sig
            ll = log_lb[:, lbs]
            log_f = jnp.maximum(ll, bb) + jnp.log(1.0 + jnp.exp(-jnp.abs(ll - bb)))
            act_ref[rows, o0:o0 + dh] = _silu(res_q[:, lbs])
            act_ref[rows, o0 + dh:o0 + 2 * dh] = one_m_lb[:, lbs] * jnp.exp(log_sig - pf)
            act_ref[rows, o0 + 2 * dh:o0 + 3 * dh] = _silu(res_g[:, lbs])
            act_ref[rows, o0 + 3 * dh:o0 + 4 * dh] = log_f * LOG2E

    ti = lax.broadcasted_iota(jnp.int32, (chunk, chunk), 0)
    si = lax.broadcasted_iota(jnp.int32, (chunk, chunk), 1)
    lvl_ref[...] = jnp.where(si < ti, ti ^ si, jnp.where(si == ti, 0, -1))
    band_ref[...] = jnp.where((si <= ti) & ((ti ^ si) < A_BAND), ti - si, -1)
    tr = lax.broadcasted_iota(jnp.int32, (chunk, dh), 0)
    sgn_ref[...] = jnp.where((tr & 4) != 0, 1.0, -1.0)
    st_ref[...] = jnp.zeros_like(st_ref)
    onw = onw_ref[...]
    levels = _hgrn_levels(chunk)

    def row_start(c):
        return c * chunk if isinstance(c, int) else pl.multiple_of(c * chunk, chunk)

    def front(chunks):
        q, k, log_f = [], [], []
        for c in chunks:
            for a in heads:
                o0 = a * 4 * dh
                q.append(act_ref[pl.ds(row_start(c), chunk), o0:o0 + dh])
                k.append(act_ref[pl.ds(row_start(c), chunk), o0 + dh:o0 + 2 * dh])
                log_f.append(act_ref[pl.ds(row_start(c), chunk), o0 + 3 * dh:o0 + 4 * dh])
        slots = range(len(q))

        b = []
        for a in slots:
            lf_hi = log_f[a].astype(BF16)
            lf_lo = (log_f[a] - lf_hi.astype(F32)).astype(BF16)
            b.append(_dot(tri_ref[...], lf_hi) + _dot(tri_ref[...], lf_lo))
            b_ref[a] = b[a]

        expo, acc = [], []
        for a in slots:
            expo.append(_hgrn_level_exponents(b[a], b_ref.at[a], sgn_ref, chunk))
            qd_ref[a] = (q[a] * jnp.exp2(b[a])).astype(BF16)
            b_last = b_ref[a, chunk - 1:chunk, :]
            kd_ref[a] = (k[a] * jnp.exp2(b_last - b[a])).astype(BF16)
            dec_ref[a] = jnp.exp2(b_last)
        n_grp = chunk // SUBLANES

        def shifted(x, dist):
            return pltpu.roll(x.reshape(n_grp, SUBLANES, dh), dist, axis=1).reshape(chunk, dh)

        for a in slots:
            acc.append(jnp.where(band_ref[...] == 0, jnp.sum(q[a] * k[a], axis=-1, keepdims=True), 0.0))
        for dist in range(1, A_BAND):
            for a in slots:
                decay = jnp.exp2(b[a] - shifted(b[a], dist))
                c = jnp.sum(q[a] * shifted(k[a], dist) * decay, axis=-1, keepdims=True)
                acc[a] = jnp.where(band_ref[...] == dist, c, acc[a])
        for m in levels:
            if m >= SUBLANES:
                break
            is_query_row = sgn_ref[...] > 0.0
            for a in slots:
                w = (jnp.where(is_query_row, q[a], k[a]) * jnp.exp2(expo[a][m])).astype(BF16)
                acc[a] = jnp.where(lvl_ref[...] >= m, _dot_nt(w, w), acc[a])
        n_grp = chunk // SUBLANES
        acc = [[acc_a[SUBLANES * i:SUBLANES * (i + 1), :] for i in range(n_grp)] for acc_a in acc]
        for m in levels:
            if m < SUBLANES:
                continue
            is_query = [(SUBLANES * i) % (2 * m) >= m for i in range(n_grp)]
            for a in slots:
                roles = jnp.concatenate([(q[a] if is_query[i] else k[a])[SUBLANES * i:SUBLANES * (i + 1), :]
                                         for i in range(n_grp)], axis=0)
                w = roles * jnp.exp2(expo[a][m])
                q_m = jnp.concatenate([w[SUBLANES * i:SUBLANES * (i + 1), :]
                                       for i in range(n_grp) if is_query[i]], axis=0)
                a_m = _dot_nt(q_m.astype(BF16), w.astype(BF16))
                for n, i in enumerate([i for i in range(n_grp) if is_query[i]]):
                    rows = slice(SUBLANES * i, SUBLANES * (i + 1))
                    acc[a][i] = jnp.where(lvl_ref[rows, :] >= m,
                                          a_m[SUBLANES * n:SUBLANES * (n + 1), :], acc[a][i])
        for a in slots:
            a16_ref[a] = jnp.concatenate(acc[a], axis=0).astype(BF16)

    def back(chunks):
        for n, c in enumerate(chunks):
            r0 = row_start(c)
            for a in heads:
                o0 = a * 4 * dh
                slot = n * hps + a
                v16 = v16_ref[pl.ds(r0, chunk), a * dh:(a + 1) * dh]
                st = st_ref[a]
                oa = _dot_nt(qd_ref[slot], st.astype(BF16)) + _dot(a16_ref[slot], v16)
                st_ref[a] = st * dec_ref[slot] + _dot_tn(v16, kd_ref[slot])
                y = _rms_rows(oa, onw) * act_ref[pl.ds(r0, chunk), o0 + 2 * dh:o0 + 3 * dh]
                y_ref[0, pl.ds(r0, chunk), a * dh:(a + 1) * dh] = y.astype(BF16)

    n_iter = seq // (chunk * cpi)
    front([n for n in range(cpi)])

    def body(i, carry):
        back([(i - 1) * cpi + n for n in range(cpi)])
        front([i * cpi + n for n in range(cpi)])
        return carry

    lax.fori_loop(1, n_iter, body, 0)
    back([(n_iter - 1) * cpi + n for n in range(cpi)])


def _hgrn_mixer(x, norm_w_row, w_in, lb_raw, o_norm_w_row, layer_idx):
    bsz, seq, d = x.shape
    dh = A_HEAD_DIM
    hps = A_HEADS_PER_STEP
    n_steps = d // (dh * hps)
    chunk = A_CHUNK
    tri = jnp.asarray(np.tril(np.ones((chunk, chunk), np.float32)), BF16)
    cpi = A_CHUNKS_PER_ITER
    n_slots = hps * cpi
    assert seq % (chunk * cpi) == 0
    kern = functools.partial(_hgrn_kernel, layer_idx=layer_idx, seq=seq, chunk=chunk, hps=hps, cpi=cpi)

    def w_spec(part):
        return pl.BlockSpec((d, hps * dh), lambda b, h, part=part: (0, part * n_steps + h))

    return pl.pallas_call(
        kern,
        grid=(bsz, n_steps),
        in_specs=[pl.BlockSpec((1, seq, d), lambda b, h: (b, 0, 0)),
                  pl.BlockSpec((1, d), lambda b, h: (0, 0)),
                  w_spec(0), w_spec(1), w_spec(2), w_spec(3),
                  pl.BlockSpec((lb_raw.shape[0], hps * dh), lambda b, h: (0, h)),
                  pl.BlockSpec((1, dh), lambda b, h: (0, 0)),
                  pl.BlockSpec((chunk, chunk), lambda b, h: (0, 0))],
        out_specs=pl.BlockSpec((1, seq, hps * dh), lambda b, h: (b, 0, h)),
        out_shape=jax.ShapeDtypeStruct((bsz, seq, d), BF16),
        scratch_shapes=[pltpu.VMEM((seq, d), BF16),
                        pltpu.VMEM((seq, hps * 4 * dh), F32),
                        pltpu.VMEM((seq, hps * dh), BF16),
                        pltpu.VMEM((n_slots, chunk, dh), F32),
                        pltpu.VMEM((hps, dh, dh), F32),
                        pltpu.VMEM((chunk, chunk), jnp.int32),
                        pltpu.VMEM((chunk, chunk), jnp.int32),
                        pltpu.VMEM((chunk, dh), F32),
                        pltpu.VMEM((n_slots, chunk, dh), BF16),
                        pltpu.VMEM((n_slots, chunk, dh), BF16),
                        pltpu.VMEM((n_slots, chunk, chunk), BF16),
                        pltpu.VMEM((n_slots, 1, dh), F32)],
        compiler_params=_params(("arbitrary", "arbitrary")),
        name="hgrn_mixer",
    )(x, norm_w_row, w_in, w_in, w_in, w_in, lb_raw, o_norm_w_row, tri)


def _bproj_kernel(x_ref, nw_ref, w_ref, wkt_ref, hw_ref, eseg_ref, esegt_ref, o_ref, h_ref, hn_ref,
                  *, seq, dils):
    j = pl.program_id(1)
    n_qkv_tiles = 3 * len(dils)

    @pl.when(j == 0)
    def _():
        d = x_ref.shape[-1]
        n_planes = d // LANES
        rb = 256
        for i in range(seq // rb):
            hn = _rms_rows(x_ref[0, i * rb:(i + 1) * rb, :], nw_ref[...])
            for c in range(n_planes):
                hn_ref[c, i * rb:(i + 1) * rb, :] = hn[:, c * LANES:(c + 1) * LANES]
        for g, dil in enumerate(dils):
            sub = seq // dil
            for r in range(dil):
                for c0 in range(0, sub, rb):
                    rows = min(rb, sub - c0)
                    if dil == 1:
                        idx = pl.ds(c0, rows)
                    else:
                        idx = pl.ds(r + c0 * dil, rows, stride=dil)
                    for c in range(n_planes):
                        h_ref[g, r * sub + c0:r * sub + c0 + rows, c * LANES:(c + 1) * LANES] = (
                            hn_ref[c, idx, :].astype(BF16))

    grp = jnp.where(j < n_qkv_tiles, j // 3, 0)
    kind = jnp.where(j < n_qkv_tiles, j % 3, 3)
    pb = 512

    def tile(i):
        return _dot(h_ref[grp, i * pb:(i + 1) * pb, :], w_ref[...])

    def store(i, val):
        for hp in range(val.shape[-1] // LANES):
            o_ref[hp, i * pb:(i + 1) * pb, :] = val[:, hp * LANES:(hp + 1) * LANES].astype(BF16)

    @pl.when(kind == 1)
    def _():
        d = x_ref.shape[-1]
        hd = B_HEAD_DIM
        for i in range(seq // pb):
            yt = _dot_nt(wkt_ref[...], h_ref[grp, i * pb:(i + 1) * pb, :])
            for h in range(d // hd):
                yh = yt[h * hd:(h + 1) * hd, :]
                r = lax.rsqrt(jnp.mean(yh * yh, axis=0, keepdims=True) + NORM_EPS)
                yh = (yh * r).astype(BF16)
                hp, half = divmod(h * hd, LANES)
                for kb in range(pb // LANES):
                    row0 = (i * (pb // LANES) + kb) * LANES + half
                    o_ref[hp, row0:row0 + hd, :] = yh[:, kb * LANES:(kb + 1) * LANES]

    @pl.when(kind == 0)
    def _():
        hw = hw_ref[0]
        n_heads = x_ref.shape[-1] // B_HEAD_DIM
        hi_lanes = lax.broadcasted_iota(jnp.int32, (pb, LANES), 1) < n_heads
        for i in range(seq // pb):
            y = tile(i)
            ssq = _dot((y * y).astype(BF16), eseg_ref[...])
            r = lax.rsqrt(ssq * (1.0 / B_HEAD_DIM) + NORM_EPS)
            r_hi = r.astype(BF16).astype(F32)
            r_split = jnp.where(hi_lanes, r_hi, r - r_hi).astype(BF16)
            rexp = _dot(r_split, esegt_ref[...])
            store(i, y * rexp * hw)

    @pl.when(kind == 2)
    def _():
        for i in range(seq // pb):
            store(i, tile(i))

    @pl.when(kind == 3)
    def _():
        for i in range(seq // pb):
            store(i, _silu(tile(i)))


def _attn_in_proj(x, norm_w_row, w_bf16, head_w):
    bsz, seq, d = x.shape
    n_out = w_bf16.shape[1]
    tn = d
    dils = tuple(dil for _, dil in DILATED_GROUPS)
    n_grp = len(dils)
    w_k_t = jnp.transpose(w_bf16.reshape(d, n_out // tn, tn)[:, 1:3 * n_grp:3, :], (1, 2, 0))
    n_heads = d // B_HEAD_DIM
    assert 2 * n_heads <= LANES
    eseg = np.zeros((d, LANES), np.float32)
    eseg[np.arange(d), np.arange(d) // B_HEAD_DIM] = 1.0
    eseg[np.arange(d), n_heads + np.arange(d) // B_HEAD_DIM] = 1.0
    eseg = jnp.asarray(eseg, BF16)
    kern = functools.partial(_bproj_kernel, seq=seq, dils=dils)
    return pl.pallas_call(
        kern,
        grid=(bsz, n_out // tn),
        in_specs=[pl.BlockSpec((1, seq, d), lambda b, j: (b, 0, 0)),
                  pl.BlockSpec((1, d), lambda b, j: (0, 0)),
                  pl.BlockSpec((d, tn), lambda b, j: (0, j)),
                  pl.BlockSpec((None, tn, d), lambda b, j: (jnp.minimum(j // 3, n_grp - 1), 0, 0)),
                  pl.BlockSpec((1, 1, tn), lambda b, j: (j, 0, 0)),
                  pl.BlockSpec((d, LANES), lambda b, j: (0, 0)),
                  pl.BlockSpec((LANES, d), lambda b, j: (0, 0))],
        out_specs=pl.BlockSpec((None, None, tn // LANES, seq, LANES), lambda b, j: (b, j, 0, 0, 0)),
        out_shape=jax.ShapeDtypeStruct((bsz, n_out // tn, tn // LANES, seq, LANES), BF16),
        scratch_shapes=[pltpu.VMEM((len(dils), seq, d), BF16),
                        pltpu.VMEM((d // LANES, seq, LANES), F32)],
        compiler_params=_params(("arbitrary", "arbitrary")),
        name="attn_in_proj",
    )(x, norm_w_row, w_bf16, w_k_t, head_w, eseg, eseg.T)


def _attn_kernel(slopes_ref, q0, k0, v0, q1, k1, v1, q2, k2, v2, g_ref, o_ref,
                 num_ref, den_ref, max_ref, bias_ref, s_ref, p_ref, *, seq, dils, unrolls):
    hp = pl.program_id(1)
    blk = B_BLOCK
    hd = B_HEAD_DIM
    n_heads = (LANES // hd) * pl.num_programs(1)
    qkv = ((q0, k0, v0), (q1, k1, v1), (q2, k2, v2))
    head_a = lax.broadcasted_iota(jnp.int32, (blk, LANES), 1) < hd


    def token_rows(g, tok0):
        return pl.ds(tok0, blk) if dils[g] == 1 else pl.ds(tok0, blk, stride=dils[g])

    def qk_phase(g, specs, has_prev, slot):
        q_ref, kt_ref, _ = qkv[g]
        nk = 2 * blk if has_prev else blk
        for w, (row0, _) in enumerate(specs):
            q2d = q_ref[pl.ds(row0, blk), :]
            zero = jnp.zeros_like(q2d)
            qq = jnp.concatenate([jnp.where(head_a, q2d, zero), jnp.where(head_a, zero, q2d)], axis=0)
            kt = kt_ref[pl.ds(row0, blk), :]
            if has_prev:
                kt = jnp.concatenate([kt_ref[pl.ds(row0 - blk, blk), :], kt], axis=1)
            s_ref[slot, w, :, 0:nk] = _dot(qq, kt)

    def softmax_phase(g, specs, has_prev, slot):
        nk = 2 * blk if has_prev else blk
        off = 0 if has_prev else blk
        for w, (_, tok0) in enumerate(specs):
            ms, ls = [], []
            for a in range(2):
                sa = s_ref[slot, w, a * blk:(a + 1) * blk, 0:nk] + bias_ref[a, :, off:off + nk]
                m = jnp.max(sa, axis=-1, keepdims=True)
                e = jnp.exp2(sa - m)
                p_ref[slot, w, a * blk:(a + 1) * blk, 0:nk] = e.astype(BF16)
                ms.append(m)
                ls.append(jnp.sum(e, axis=-1, keepdims=True))
            idx = token_rows(g, tok0)
            max_ref[g, idx, :] = jnp.where(head_a, ms[0], ms[1])
            den_ref[g, idx, :] = jnp.where(head_a, ls[0], ls[1])

    def pv_phase(g, specs, has_prev, slot):
        v_ref = qkv[g][2]
        nk = 2 * blk if has_prev else blk
        for w, (row0, tok0) in enumerate(specs):
            k_lo = row0 - blk if has_prev else row0
            o = _dot(p_ref[slot, w, :, 0:nk], v_ref[pl.ds(k_lo, nk), :])
            num_ref[g, token_rows(g, tok0), :] = jnp.where(head_a, o[0:blk, :], o[blk:2 * blk, :])

    def pipelined(n_iter, make_specs, g, has_prev):
        def step(t):
            static = isinstance(t, int)
            if (not static) or 0 <= t - 2 < n_iter:
                pv_phase(g, make_specs(t - 2), has_prev, t % 2)
            if (not static) or 0 <= t - 1 < n_iter:
                softmax_phase(g, make_specs(t - 1), has_prev, (t - 1) % 2)
            if (not static) or 0 <= t < n_iter:
                qk_phase(g, make_specs(t), has_prev, t % 2)

        for t in range(2):
            step(t)
        if n_iter > 2:
            def body(t, carry):
                step(t)
                return carry

            lax.fori_loop(2, n_iter, body, 0)
        for t in range(max(n_iter, 2), n_iter + 2):
            step(t)

    qi = lax.broadcasted_iota(jnp.int32, (blk, 2 * blk), 0)
    kj = lax.broadcasted_iota(jnp.int32, (blk, 2 * blk), 1)
    dist = blk + qi - kj
    valid = (dist >= 0) & (dist <= blk)

    for g, dil in enumerate(dils):
        sub = seq // dil
        n_blk = sub // blk
        bias_base = jnp.where(valid, -(dist * dil).astype(F32), NEG_BIG)
        for a in range(2):
            slope = slopes_ref[g * n_heads + 2 * hp + a]
            bias_ref[a] = (slope * LOG2E) * bias_base

        u1, u2 = unrolls[g]

        def aligned(row):
            return row if isinstance(row, int) else pl.multiple_of(row, blk)

        def first_specs(i, sub=sub, u1=u1):
            return [(aligned((i * u1 + w) * sub), i * u1 + w) for w in range(u1)]

        def rest_specs(i, dil=dil, sub=sub, n_blk=n_blk, u2=u2):
            specs = []
            for w in range(u2):
                idx = i * u2 + w
                r = idx // (n_blk - 1)
                b = idx % (n_blk - 1) + 1
                specs.append((aligned(r * sub + b * blk), b * blk * dil + r))
            return specs

        pipelined(dil // u1, first_specs, g, False)
        n_rest = dil * (n_blk - 1)
        if n_rest:
            pipelined(n_rest // u2, rest_specs, g, True)

    rb = 256
    n_g = len(dils)

    def merge(i, carry):
        r0 = pl.multiple_of(i * rb, rb)
        ms = [max_ref[g, pl.ds(r0, rb), :] for g in range(n_g)]
        mx = ms[0]
        for g in range(1, n_g):
            mx = jnp.maximum(mx, ms[g])
        num = jnp.zeros((rb, LANES), F32)
        den = jnp.zeros((rb, LANES), F32)
        for g in range(n_g):
            w = jnp.exp2(ms[g] - mx)
            num = num + w * num_ref[g, pl.ds(r0, rb), :]
            den = den + w * den_ref[g, pl.ds(r0, rb), :]
        gate = g_ref[pl.ds(r0, rb), :].astype(F32)
        o_ref[pl.ds(r0, rb), :] = (num / den * gate).astype(BF16)
        return carry

    lax.fori_loop(0, seq // rb, merge, 0)


def _attention(proj, slopes):
    bsz, n_tiles, n_pairs, seq, _ = proj.shape
    assert n_tiles == 3 * B_N_GROUPS + 1
    dils = tuple(dil for _, dil in DILATED_GROUPS)
    unrolls = ((1, 5), (4, 6), (4, 1))
    for (u1, u2), dil in zip(unrolls, dils):
        n_rest = dil * (seq // dil // B_BLOCK - 1)
        assert dil % u1 == 0 and n_rest % u2 == 0

    def col_spec(tile):
        return pl.BlockSpec((None, None, None, seq, LANES), lambda b, h, s, tile=tile: (b, tile, h, 0, 0))

    kern = functools.partial(_attn_kernel, seq=seq, dils=dils, unrolls=unrolls)
    grid_spec = pltpu.PrefetchScalarGridSpec(
        num_scalar_prefetch=1,
        grid=(bsz, n_pairs),
        in_specs=[col_spec(t) for t in range(n_tiles)],
        out_specs=pl.BlockSpec((None, None, seq, LANES), lambda b, h, s: (b, h, 0, 0)),
        scratch_shapes=[pltpu.VMEM((B_N_GROUPS, seq, LANES), F32)] * 3
        + [pltpu.VMEM((2, B_BLOCK, 2 * B_BLOCK), F32),
           pltpu.VMEM((2, max(max(u) for u in unrolls), 2 * B_BLOCK, 2 * B_BLOCK), F32),
           pltpu.VMEM((2, max(max(u) for u in unrolls), 2 * B_BLOCK, 2 * B_BLOCK), BF16)],
    )
    return pl.pallas_call(
        kern,
        grid_spec=grid_spec,
        out_shape=jax.ShapeDtypeStruct((bsz, n_pairs, seq, LANES), BF16),
        compiler_params=_params(("arbitrary", "arbitrary")),
        name="dilated_attention",
    )(slopes, *([proj] * n_tiles))


def _gmlp_kernel(x_ref, yprev_ref, wprev_ref, nw_ref, w_ref, vw_ref, vb_ref, ws_ref, bias_ref, wout_ref, o_ref,
                 x1_ref, h_ref, u_ref, v_ref, g_ref, y_ref, *, tm):
    d = x_ref.shape[-1]
    cg = d // C_GROUPS
    rb = 256
    for i in range(tm // rb):
        rows = slice(i * rb, (i + 1) * rb)
        y_prev = jnp.concatenate([yprev_ref[hp, rows, :] for hp in range(yprev_ref.shape[0])], axis=1)
        x1_ref[rows, :] = x_ref[0, rows, :] + _dot(y_prev, wprev_ref[...])
        h_ref[rows, :] = _rms_rows(x1_ref[rows, :], nw_ref[...]).astype(BF16)
    for i in range(tm // rb):
        rows = slice(i * rb, (i + 1) * rb)
        hh = h_ref[rows, :]
        u_ref[rows, :] = _gelu_tanh(_dot(hh, w_ref[:, 0:d]))
        vv = _gelu_tanh(_dot(hh, w_ref[:, d:2 * d]))
        mu = jnp.mean(vv, axis=-1, keepdims=True)
        vc = vv - mu
        var = jnp.mean(vc * vc, axis=-1, keepdims=True)
        v_ref[rows, :] = (vc * lax.rsqrt(var + NORM_EPS) * vw_ref[...] + vb_ref[...]).astype(BF16)
        g_ref[rows, :] = _silu(_dot(hh, w_ref[:, 2 * d:3 * d]))
    ti = lax.broadcasted_iota(jnp.int32, (C_CHUNK, C_CHUNK), 0)
    si = lax.broadcasted_iota(jnp.int32, (C_CHUNK, C_CHUNK), 1)
    n_ch = tm // C_CHUNK
    for grp in range(C_GROUPS):
        cols = slice(grp * cg, (grp + 1) * cg)
        wc = jnp.where(si <= ti, ws_ref[grp], 0.0).astype(BF16)
        rhs = jnp.concatenate([v_ref[n * C_CHUNK:(n + 1) * C_CHUNK, cols] for n in range(n_ch)], axis=1)
        s = _dot(wc, rhs)
        bias = bias_ref[:, cols]
        for n in range(n_ch):
            rows = slice(n * C_CHUNK, (n + 1) * C_CHUNK)
            sn = s[:, n * cg:(n + 1) * cg] + bias
            y_ref[rows, cols] = (u_ref[rows, cols] * sn * g_ref[rows, cols]).astype(BF16)
    for i in range(tm // rb):
        rows = slice(i * rb, (i + 1) * rb)
        o_ref[0, rows, :] = x1_ref[rows, :] + _dot(y_ref[rows, :], wout_ref[...])


def _gmlp_mixer(x, y_prev_pairs, w_prev_bf16, norm_w_row, w_bf16, vw_row, vb_row, w_s, bias_full, w_out_bf16,
                tm=512):
    bsz, seq, d = x.shape
    n_pairs, lanes = y_prev_pairs.shape[1], y_prev_pairs.shape[3]
    kern = functools.partial(_gmlp_kernel, tm=tm)
    return pl.pallas_call(
        kern,
        grid=(bsz, seq // tm),
        in_specs=[pl.BlockSpec((1, tm, d), lambda b, i: (b, i, 0)),
                  pl.BlockSpec((None, n_pairs, tm, lanes), lambda b, i: (b, 0, i, 0)),
                  pl.BlockSpec((d, d), lambda b, i: (0, 0)),
                  pl.BlockSpec((1, d), lambda b, i: (0, 0)),
                  pl.BlockSpec((d, 3 * d), lambda b, i: (0, 0)),
                  pl.BlockSpec((1, d), lambda b, i: (0, 0)),
                  pl.BlockSpec((1, d), lambda b, i: (0, 0)),
                  pl.BlockSpec(w_s.shape, lambda b, i: (0, 0, 0)),
                  pl.BlockSpec((C_CHUNK, d), lambda b, i: (0, 0)),
                  pl.BlockSpec((d, d), lambda b, i: (0, 0))],
        out_specs=pl.BlockSpec((1, tm, d), lambda b, i: (b, i, 0)),
        out_shape=jax.ShapeDtypeStruct((bsz, seq, d), F32),
        scratch_shapes=[pltpu.VMEM((tm, d), F32),
                        pltpu.VMEM((tm, d), BF16),
                        pltpu.VMEM((tm, d), F32),
                        pltpu.VMEM((tm, d), BF16),
                        pltpu.VMEM((tm, d), F32),
                        pltpu.VMEM((tm, d), BF16)],
        compiler_params=_params(("arbitrary", "arbitrary")),
        name="gmlp_mixer",
    )(x, y_prev_pairs, w_prev_bf16, norm_w_row, w_bf16, vw_row, vb_row, w_s, bias_full, w_out_bf16)


def kernel(x, norm_w, a_w_in, a_lower_bounds, a_o_norm_w, a_w_out, b_w_in, b_q_norm_w, b_k_norm_w, b_w_out,
           c_w_in, c_v_norm_w, c_v_norm_b, c_w_s, c_b_s, c_w_out):
    bsz, seq, d = x.shape
    depth = norm_w.shape[0]
    n_mixers = 3
    assert seq == DILATED_GROUPS[-1][0] and seq % A_CHUNK == 0 and d % LANES == 0

    def residual(y, w_out, xin):
        out = _out_proj(y.reshape(bsz * seq, d), w_out.astype(BF16), xin.reshape(bsz * seq, d))
        return out.reshape(bsz, seq, d)

    pending = None
    for layer in range(depth):
        kind, idx = layer % n_mixers, layer // n_mixers
        nw = norm_w[layer][None, :]
        if kind == 0:
            y = _hgrn_mixer(x, nw, a_w_in[idx].astype(BF16), a_lower_bounds, a_o_norm_w[idx][None, :], idx)
            x = residual(y, a_w_out[idx], x)
        elif kind == 1:
            n_heads = d // B_HEAD_DIM
            n_total = B_N_GROUPS * n_heads
            slopes = jnp.exp2(-ALIBI_MAX_EXP * jnp.arange(1, n_total + 1, dtype=F32) / n_total)
            rows = []
            for g in range(B_N_GROUPS):
                rows.append(jnp.tile(b_q_norm_w[idx, g] * b_k_norm_w[idx, g], n_heads)
                            * (B_HEAD_DIM ** -0.5 * LOG2E))
                rows.append(jnp.ones((d,), F32))
                rows.append(jnp.ones((d,), F32))
            rows.append(jnp.ones((d,), F32))
            head_w = jnp.stack(rows)[:, None, :]
            proj = _attn_in_proj(x, nw, b_w_in[idx].astype(BF16), head_w)
            pending = (_attention(proj, slopes), b_w_out[idx].astype(BF16))
            if layer == depth - 1:
                x = _out_proj_pairs(*pending, x)
        else:
            bias_full = jnp.repeat(c_b_s[idx].T, d // C_GROUPS, axis=1)
            x = _gmlp_mixer(x, *pending, nw, c_w_in[idx].astype(BF16), c_v_norm_w[idx][None, :],
                            c_v_norm_b[idx][None, :], c_w_s[idx], bias_full, c_w_out[idx].astype(BF16))
    return x
```

```python
import functools

import numpy as np
import jax
import jax.numpy as jnp
from jax import lax
from jax.experimental import pallas as pl
from jax.experimental.pallas import tpu as pltpu

F32 = jnp.float32
BF16 = jnp.bfloat16

NORM_EPS = 1e-6
NEG_BIG = -1e30
LB_FLOOR = 1e-30
LOG2E = float(np.log2(np.e))

LANES = 128
SUBLANES = 8
VMEM_LIMIT_BYTES = 60 * 1024 * 1024

A_HEAD_DIM = 128
A_CHUNK = 128
A_CHUNKS_PER_ITER = 2
A_HEADS_PER_STEP = 4
B_HEAD_DIM = 64
DILATED_GROUPS = ((128, 1), (512, 4), (2048, 16))
B_N_GROUPS = len(DILATED_GROUPS)
B_BLOCK = 128
ALIBI_MAX_EXP = 8.0
C_CHUNK = 128
C_GROUPS = 8


def _silu(x):
    hx = 0.5 * x
    return hx + hx * jnp.tanh(hx)


def _gelu_tanh(x):
    c = np.float32(np.sqrt(2.0 / np.pi))
    return 0.5 * x * (1.0 + jnp.tanh(c * (x + 0.044715 * (x * x * x))))


def _rms_rows(xf, w):
    ms = jnp.mean(xf * xf, axis=-1, keepdims=True)
    return xf * lax.rsqrt(ms + NORM_EPS) * w


def _dot(a, b):
    return jnp.dot(a, b, preferred_element_type=F32)


def _dot_nt(a, b):
    return lax.dot_general(a, b, (((1,), (1,)), ((), ())), preferred_element_type=F32)


def _dot_tn(a, b):
    return lax.dot_general(a, b, (((0,), (0,)), ((), ())), preferred_element_type=F32)


def _params(sem):
    return pltpu.CompilerParams(dimension_semantics=sem, vmem_limit_bytes=VMEM_LIMIT_BYTES)


def _out_proj_kernel(y_ref, w_ref, x_ref, o_ref):
    o_ref[...] = x_ref[...] + _dot(y_ref[...], w_ref[...])


def _out_proj(y2d, w_bf16, x2d, tm=2048):
    m, k = y2d.shape
    n = w_bf16.shape[1]
    return pl.pallas_call(
        _out_proj_kernel,
        grid=(m // tm,),
        in_specs=[pl.BlockSpec((tm, k), lambda i: (i, 0)),
                  pl.BlockSpec((k, n), lambda i: (0, 0)),
                  pl.BlockSpec((tm, n), lambda i: (i, 0))],
        out_specs=pl.BlockSpec((tm, n), lambda i: (i, 0)),
        out_shape=jax.ShapeDtypeStruct((m, n), F32),
        compiler_params=_params(("arbitrary",)),
        name="out_proj",
    )(y2d, w_bf16, x2d)


def _out_proj_pairs_kernel(y_ref, w_ref, x_ref, o_ref):
    y = jnp.concatenate([y_ref[hp] for hp in range(y_ref.shape[0])], axis=1)
    o_ref[...] = x_ref[...] + _dot(y, w_ref[...])


def _out_proj_pairs(y_pairs, w_bf16, x, tm=1024):
    bsz, n_pairs, seq, lanes = y_pairs.shape
    d = n_pairs * lanes
    n = w_bf16.shape[1]
    return pl.pallas_call(
        _out_proj_pairs_kernel,
        grid=(bsz, seq // tm),
        in_specs=[pl.BlockSpec((None, n_pairs, tm, lanes), lambda b, i: (b, 0, i, 0)),
                  pl.BlockSpec((d, n), lambda b, i: (0, 0)),
                  pl.BlockSpec((None, tm, n), lambda b, i: (b, i, 0))],
        out_specs=pl.BlockSpec((None, tm, n), lambda b, i: (b, i, 0)),
        out_shape=jax.ShapeDtypeStruct((bsz, seq, n), F32),
        compiler_params=_params(("arbitrary", "arbitrary")),
        name="out_proj_pairs",
    )(y_pairs, w_bf16, x)


A_BAND = 4


def _hgrn_levels(chunk):
    return [1 << j for j in range(int(np.log2(chunk))) if (2 << j) > A_BAND]


def _hgrn_level_exponents(b, b_view, sgn_ref, chunk):
    out = {}
    n_vregs = chunk // SUBLANES
    width = b.shape[-1]

    def row(r, n):
        one_group = jnp.broadcast_to(b_view[pl.ds(r, 1), :], (SUBLANES, width))
        return jnp.concatenate([one_group] * (n // SUBLANES), axis=0)

    for m in _hgrn_levels(chunk):
        if m >= SUBLANES:
            parts = []
            for j in range(chunk // (2 * m)):
                base = j * 2 * m
                ref_row = row(base + m - 1, m)
                parts.append(ref_row - b[base:base + m, :])
                parts.append(b[base + m:base + 2 * m, :] - ref_row)
            out[m] = jnp.concatenate(parts, axis=0)
        else:
            assert m == 4
            ref = jnp.concatenate([row(SUBLANES * i + 3, SUBLANES) for i in range(n_vregs)], axis=0)
            out[m] = (b - ref) * sgn_ref[...]
    return out


def _hgrn_kernel(x_ref, nw_ref, wq_ref, wf_ref, wi_ref, wg_ref, lbraw_ref, onw_ref, tri_ref, y_ref,
                 h_ref, act_ref, v16_ref, b_ref, st_ref, lvl_ref, band_ref, sgn_ref, qd_ref, kd_ref, a16_ref,
                 dec_ref,
                 *, layer_idx, seq, chunk, hps, cpi):
    step = pl.program_id(1)
    dh = A_HEAD_DIM
    rb = 256

    @pl.when(step == 0)
    def _():
        for i in range(seq // rb):
            xs = x_ref[0, i * rb:(i + 1) * rb, :]
            h_ref[i * rb:(i + 1) * rb, :] = _rms_rows(xs, nw_ref[...]).astype(BF16)

    raw = lbraw_ref[...]
    n_layers = raw.shape[0]
    mx = raw[0:1, :]
    for i in range(1, n_layers):
        mx = jnp.maximum(mx, raw[i:i + 1, :])
    es = [jnp.exp(raw[i:i + 1, :] - mx) for i in range(n_layers)]
    z = es[0]
    for i in range(1, n_layers):
        z = z + es[i]
    soft = [e / z for e in es]
    csum = soft[0]
    for i in range(1, layer_idx + 1):
        csum = csum + soft[i]
    lb = csum - soft[0]
    log_lb = jnp.log(jnp.maximum(lb, LB_FLOOR))
    log1m_lb = jnp.log1p(-lb)
    one_m_lb = 1.0 - lb
    heads = range(hps)

    pb = 512
    for i in range(seq // pb):
        rows = slice(i * pb, (i + 1) * pb)
        hh = h_ref[rows, :]
        res_q, res_f, res_i, res_g = (_dot(hh, w[...]) for w in (wq_ref, wf_ref, wi_ref, wg_ref))
        v16_ref[rows, :] = res_i.astype(BF16)
        for a in heads:
            o0 = a * 4 * dh
            lbs = slice(a * dh, (a + 1) * dh)
            pf = res_f[:, lbs]
            log_sig = jnp.minimum(pf, 0.0) - jnp.log(1.0 + jnp.exp(-jnp.abs(pf)))
            bb = log1m_lb[:, lbs] + log_sig
            ll = log_lb[:, lbs]
            log_f = jnp.maximum(ll, bb) + jnp.log(1.0 + jnp.exp(-jnp.abs(ll - bb)))
            act_ref[rows, o0:o0 + dh] = _silu(res_q[:, lbs])
            act_ref[rows, o0 + dh:o0 + 2 * dh] = one_m_lb[:, lbs] * jnp.exp(log_sig - pf)
            act_ref[rows, o0 + 2 * dh:o0 + 3 * dh] = _silu(res_g[:, lbs])
            act_ref[rows, o0 + 3 * dh:o0 + 4 * dh] = log_f * LOG2E

    ti = lax.broadcasted_iota(jnp.int32, (chunk, chunk), 0)
    si = lax.broadcasted_iota(jnp.int32, (chunk, chunk), 1)
    lvl_ref[...] = jnp.where(si < ti, ti ^ si, jnp.where(si == ti, 0, -1))
    band_ref[...] = jnp.where((si <= ti) & ((ti ^ si) < A_BAND), ti - si, -1)
    tr = lax.broadcasted_iota(jnp.int32, (chunk, dh), 0)
    sgn_ref[...] = jnp.where((tr & 4) != 0, 1.0, -1.0)
    st_ref[...] = jnp.zeros_like(st_ref)
    onw = onw_ref[...]
    levels = _hgrn_levels(chunk)

    def row_start(c):
        return c * chunk if isinstance(c, int) else pl.multiple_of(c * chunk, chunk)

    def front(chunks):
        q, k, log_f = [], [], []
        for c in chunks:
            for a in heads:
                o0 = a * 4 * dh
                q.append(act_ref[pl.ds(row_start(c), chunk), o0:o0 + dh])
                k.append(act_ref[pl.ds(row_start(c), chunk), o0 + dh:o0 + 2 * dh])
                log_f.append(act_ref[pl.ds(row_start(c), chunk), o0 + 3 * dh:o0 + 4 * dh])
        slots = range(len(q))

        b = []
        for a in slots:
            lf_hi = log_f[a].astype(BF16)
            lf_lo = (log_f[a] - lf_hi.astype(F32)).astype(BF16)
            b.append(_dot(tri_ref[...], lf_hi) + _dot(tri_ref[...], lf_lo))
            b_ref[a] = b[a]

        expo, acc = [], []
        for a in slots:
            expo.append(_hgrn_level_exponents(b[a], b_ref.at[a], sgn_ref, chunk))
            qd_ref[a] = (q[a] * jnp.exp2(b[a])).astype(BF16)
            b_last = b_ref[a, chunk - 1:chunk, :]
            kd_ref[a] = (k[a] * jnp.exp2(b_last - b[a])).astype(BF16)
            dec_ref[a] = jnp.exp2(b_last)
        n_grp = chunk // SUBLANES

        def shifted(x, dist):
            return pltpu.roll(x.reshape(n_grp, SUBLANES, dh), dist, axis=1).reshape(chunk, dh)

        for a in slots:
            acc.append(jnp.where(band_ref[...] == 0, jnp.sum(q[a] * k[a], axis=-1, keepdims=True), 0.0))
        for dist in range(1, A_BAND):
            for a in slots:
                decay = jnp.exp2(b[a] - shifted(b[a], dist))
                c = jnp.sum(q[a] * shifted(k[a], dist) * decay, axis=-1, keepdims=True)
                acc[a] = jnp.where(band_ref[...] == dist, c, acc[a])
        for m in levels:
            if m >= SUBLANES:
                break
            is_query_row = sgn_ref[...] > 0.0
            for a in slots:
                w = (jnp.where(is_query_row, q[a], k[a]) * jnp.exp2(expo[a][m])).astype(BF16)
                acc[a] = jnp.where(lvl_ref[...] >= m, _dot_nt(w, w), acc[a])
        n_grp = chunk // SUBLANES
        acc = [[acc_a[SUBLANES * i:SUBLANES * (i + 1), :] for i in range(n_grp)] for acc_a in acc]
        for m in levels:
            if m < SUBLANES:
                continue
            is_query = [(SUBLANES * i) % (2 * m) >= m for i in range(n_grp)]
            for a in slots:
                roles = jnp.concatenate([(q[a] if is_query[i] else k[a])[SUBLANES * i:SUBLANES * (i + 1), :]
                                         for i in range(n_grp)], axis=0)
                w = roles * jnp.exp2(expo[a][m])
                q_m = jnp.concatenate([w[SUBLANES * i:SUBLANES * (i + 1), :]
                                       for i in range(n_grp) if is_query[i]], axis=0)
                a_m = _dot_nt(q_m.astype(BF16), w.astype(BF16))
                for n, i in enumerate([i for i in range(n_grp) if is_query[i]]):
                    rows = slice(SUBLANES * i, SUBLANES * (i + 1))
                    acc[a][i] = jnp.where(lvl_ref[rows, :] >= m,
                                          a_m[SUBLANES * n:SUBLANES * (n + 1), :], acc[a][i])
        for a in slots:
            a16_ref[a] = jnp.concatenate(acc[a], axis=0).astype(BF16)

    def back(chunks):
        for n, c in enumerate(chunks):
            r0 = row_start(c)
            for a in heads:
                o0 = a * 4 * dh
                slot = n * hps + a
                v16 = v16_ref[pl.ds(r0, chunk), a * dh:(a + 1) * dh]
                st = st_ref[a]
                oa = _dot_nt(qd_ref[slot], st.astype(BF16)) + _dot(a16_ref[slot], v16)
                st_ref[a] = st * dec_ref[slot] + _dot_tn(v16, kd_ref[slot])
                y = _rms_rows(oa, onw) * act_ref[pl.ds(r0, chunk), o0 + 2 * dh:o0 + 3 * dh]
                y_ref[0, pl.ds(r0, chunk), a * dh:(a + 1) * dh] = y.astype(BF16)

    n_iter = seq // (chunk * cpi)
    front([n for n in range(cpi)])

    def body(i, carry):
        back([(i - 1) * cpi + n for n in range(cpi)])
        front([i * cpi + n for n in range(cpi)])
        return carry

    lax.fori_loop(1, n_iter, body, 0)
    back([(n_iter - 1) * cpi + n for n in range(cpi)])


def _hgrn_mixer(x, norm_w_row, w_in, lb_raw, o_norm_w_row, layer_idx):
    bsz, seq, d = x.shape
    dh = A_HEAD_DIM
    hps = A_HEADS_PER_STEP
    n_steps = d // (dh * hps)
    chunk = A_CHUNK
    tri = jnp.asarray(np.tril(np.ones((chunk, chunk), np.float32)), BF16)
    cpi = A_CHUNKS_PER_ITER
    n_slots = hps * cpi
    assert seq % (chunk * cpi) == 0
    kern = functools.partial(_hgrn_kernel, layer_idx=layer_idx, seq=seq, chunk=chunk, hps=hps, cpi=cpi)

    def w_spec(part):
        return pl.BlockSpec((d, hps * dh), lambda b, h, part=part: (0, part * n_steps + h))

    return pl.pallas_call(
        kern,
        grid=(bsz, n_steps),
        in_specs=[pl.BlockSpec((1, seq, d), lambda b, h: (b, 0, 0)),
                  pl.BlockSpec((1, d), lambda b, h: (0, 0)),
                  w_spec(0), w_spec(1), w_spec(2), w_spec(3),
                  pl.BlockSpec((lb_raw.shape[0], hps * dh), lambda b, h: (0, h)),
                  pl.BlockSpec((1, dh), lambda b, h: (0, 0)),
                  pl.BlockSpec((chunk, chunk), lambda b, h: (0, 0))],
        out_specs=pl.BlockSpec((1, seq, hps * dh), lambda b, h: (b, 0, h)),
        out_shape=jax.ShapeDtypeStruct((bsz, seq, d), BF16),
        scratch_shapes=[pltpu.VMEM((seq, d), BF16),
                        pltpu.VMEM((seq, hps * 4 * dh), F32),
                        pltpu.VMEM((seq, hps * dh), BF16),
                        pltpu.VMEM((n_slots, chunk, dh), F32),
                        pltpu.VMEM((hps, dh, dh), F32),
                        pltpu.VMEM((chunk, chunk), jnp.int32),
                        pltpu.VMEM((chunk, chunk), jnp.int32),
                        pltpu.VMEM((chunk, dh), F32),
                        pltpu.VMEM((n_slots, chunk, dh), BF16),
                        pltpu.VMEM((n_slots, chunk, dh), BF16),
                        pltpu.VMEM((n_slots, chunk, chunk), BF16),
                        pltpu.VMEM((n_slots, 1, dh), F32)],
        compiler_params=_params(("arbitrary", "arbitrary")),
        name="hgrn_mixer",
    )(x, norm_w_row, w_in, w_in, w_in, w_in, lb_raw, o_norm_w_row, tri)


def _bproj_kernel(x_ref, nw_ref, w_ref, wkt_ref, hw_ref, eseg_ref, esegt_ref, o_ref, h_ref, hn_ref,
                  *, seq, dils):
    j = pl.program_id(1)
    n_qkv_tiles = 3 * len(dils)

    @pl.when(j == 0)
    def _():
        d = x_ref.shape[-1]
        n_planes = d // LANES
        rb = 256
        for i in range(seq // rb):
            hn = _rms_rows(x_ref[0, i * rb:(i + 1) * rb, :], nw_ref[...])
            for c in range(n_planes):
                hn_ref[c, i * rb:(i + 1) * rb, :] = hn[:, c * LANES:(c + 1) * LANES]
        for g, dil in enumerate(dils):
            sub = seq // dil
            for r in range(dil):
                for c0 in range(0, sub, rb):
                    rows = min(rb, sub - c0)
                    if dil == 1:
                        idx = pl.ds(c0, rows)
                    else:
                        idx = pl.ds(r + c0 * dil, rows, stride=dil)
                    for c in range(n_planes):
                        h_ref[g, r * sub + c0:r * sub + c0 + rows, c * LANES:(c + 1) * LANES] = (
                            hn_ref[c, idx, :].astype(BF16))

    grp = jnp.where(j < n_qkv_tiles, j // 3, 0)
    kind = jnp.where(j < n_qkv_tiles, j % 3, 3)
    pb = 512

    def tile(i):
        return _dot(h_ref[grp, i * pb:(i + 1) * pb, :], w_ref[...])

    def store(i, val):
        for hp in range(val.shape[-1] // LANES):
            o_ref[hp, i * pb:(i + 1) * pb, :] = val[:, hp * LANES:(hp + 1) * LANES].astype(BF16)

    @pl.when(kind == 1)
    def _():
        d = x_ref.shape[-1]
        hd = B_HEAD_DIM
        for i in range(seq // pb):
            yt = _dot_nt(wkt_ref[...], h_ref[grp, i * pb:(i + 1) * pb, :])
            for h in range(d // hd):
                yh = yt[h * hd:(h + 1) * hd, :]
                r = lax.rsqrt(jnp.mean(yh * yh, axis=0, keepdims=True) + NORM_EPS)
                yh = (yh * r).astype(BF16)
                hp, half = divmod(h * hd, LANES)
                for kb in range(pb // LANES):
                    row0 = (i * (pb // LANES) + kb) * LANES + half
                    o_ref[hp, row0:row0 + hd, :] = yh[:, kb * LANES:(kb + 1) * LANES]

    @pl.when(kind == 0)
    def _():
        hw = hw_ref[0]
        n_heads = x_ref.shape[-1] // B_HEAD_DIM
        hi_lanes = lax.broadcasted_iota(jnp.int32, (pb, LANES), 1) < n_heads
        for i in range(seq // pb):
            y = tile(i)
            ssq = _dot((y * y).astype(BF16), eseg_ref[...])
            r = lax.rsqrt(ssq * (1.0 / B_HEAD_DIM) + NORM_EPS)
            r_hi = r.astype(BF16).astype(F32)
            r_split = jnp.where(hi_lanes, r_hi, r - r_hi).astype(BF16)
            rexp = _dot(r_split, esegt_ref[...])
            store(i, y * rexp * hw)

    @pl.when(kind == 2)
    def _():
        for i in range(seq // pb):
            store(i, tile(i))

    @pl.when(kind == 3)
    def _():
        for i in range(seq // pb):
            store(i, _silu(tile(i)))


def _attn_in_proj(x, norm_w_row, w_in, head_w):
    bsz, seq, d = x.shape
    n_out = w_in.shape[1]
    tn = d
    dils = tuple(dil for _, dil in DILATED_GROUPS)
    n_grp = len(dils)
    w_k_t = jnp.transpose(w_in.reshape(d, n_out // tn, tn)[:, 1:3 * n_grp:3, :], (1, 2, 0)).astype(BF16)
    n_heads = d // B_HEAD_DIM
    assert 2 * n_heads <= LANES
    eseg = np.zeros((d, LANES), np.float32)
    eseg[np.arange(d), np.arange(d) // B_HEAD_DIM] = 1.0
    eseg[np.arange(d), n_heads + np.arange(d) // B_HEAD_DIM] = 1.0
    eseg = jnp.asarray(eseg, BF16)
    kern = functools.partial(_bproj_kernel, seq=seq, dils=dils)
    return pl.pallas_call(
        kern,
        grid=(bsz, n_out // tn),
        in_specs=[pl.BlockSpec((1, seq, d), lambda b, j: (b, 0, 0)),
                  pl.BlockSpec((1, d), lambda b, j: (0, 0)),
                  pl.BlockSpec((d, tn), lambda b, j: (0, j)),
                  pl.BlockSpec((None, tn, d), lambda b, j: (jnp.minimum(j // 3, n_grp - 1), 0, 0)),
                  pl.BlockSpec((1, 1, tn), lambda b, j: (j, 0, 0)),
                  pl.BlockSpec((d, LANES), lambda b, j: (0, 0)),
                  pl.BlockSpec((LANES, d), lambda b, j: (0, 0))],
        out_specs=pl.BlockSpec((None, None, tn // LANES, seq, LANES), lambda b, j: (b, j, 0, 0, 0)),
        out_shape=jax.ShapeDtypeStruct((bsz, n_out // tn, tn // LANES, seq, LANES), BF16),
        scratch_shapes=[pltpu.VMEM((len(dils), seq, d), BF16),
                        pltpu.VMEM((d // LANES, seq, LANES), F32)],
        compiler_params=_params(("arbitrary", "arbitrary")),
        name="attn_in_proj",
    )(x, norm_w_row, w_in.astype(BF16), w_k_t, head_w, eseg, eseg.T)


def _attn_kernel(slopes_ref, q0, k0, v0, q1, k1, v1, q2, k2, v2, g_ref, o_ref,
                 num_ref, den_ref, max_ref, bias_ref, s_ref, p_ref, *, seq, dils, unrolls):
    hp = pl.program_id(1)
    blk = B_BLOCK
    hd = B_HEAD_DIM
    n_heads = (LANES // hd) * pl.num_programs(1)
    qkv = ((q0, k0, v0), (q1, k1, v1), (q2, k2, v2))
    head_a = lax.broadcasted_iota(jnp.int32, (blk, LANES), 1) < hd


    def token_rows(g, tok0):
        return pl.ds(tok0, blk) if dils[g] == 1 else pl.ds(tok0, blk, stride=dils[g])

    def qk_phase(g, specs, has_prev, slot):
        q_ref, kt_ref, _ = qkv[g]
        nk = 2 * blk if has_prev else blk
        for w, (row0, _) in enumerate(specs):
            q2d = q_ref[pl.ds(row0, blk), :]
            zero = jnp.zeros_like(q2d)
            qq = jnp.concatenate([jnp.where(head_a, q2d, zero), jnp.where(head_a, zero, q2d)], axis=0)
            kt = kt_ref[pl.ds(row0, blk), :]
            if has_prev:
                kt = jnp.concatenate([kt_ref[pl.ds(row0 - blk, blk), :], kt], axis=1)
            s_ref[slot, w, :, 0:nk] = _dot(qq, kt)

    def softmax_phase(g, specs, has_prev, slot):
        nk = 2 * blk if has_prev else blk
        off = 0 if has_prev else blk
        for w, (_, tok0) in enumerate(specs):
            ms, ls = [], []
            for a in range(2):
                sa = s_ref[slot, w, a * blk:(a + 1) * blk, 0:nk] + bias_ref[a, :, off:off + nk]
                m = jnp.max(sa, axis=-1, keepdims=True)
                e = jnp.exp2(sa - m)
                p_ref[slot, w, a * blk:(a + 1) * blk, 0:nk] = e.astype(BF16)
                ms.append(m)
                ls.append(jnp.sum(e, axis=-1, keepdims=True))
            idx = token_rows(g, tok0)
            max_ref[g, idx, :] = jnp.where(head_a, ms[0], ms[1])
            den_ref[g, idx, :] = jnp.where(head_a, ls[0], ls[1])

    def pv_phase(g, specs, has_prev, slot):
        v_ref = qkv[g][2]
        nk = 2 * blk if has_prev else blk
        for w, (row0, tok0) in enumerate(specs):
            k_lo = row0 - blk if has_prev else row0
            o = _dot(p_ref[slot, w, :, 0:nk], v_ref[pl.ds(k_lo, nk), :])
            num_ref[g, token_rows(g, tok0), :] = jnp.where(head_a, o[0:blk, :], o[blk:2 * blk, :])

    def pipelined(n_iter, make_specs, g, has_prev):
        def step(t):
            static = isinstance(t, int)
            if (not static) or 0 <= t - 2 < n_iter:
                pv_phase(g, make_specs(t - 2), has_prev, t % 2)
            if (not static) or 0 <= t - 1 < n_iter:
                softmax_phase(g, make_specs(t - 1), has_prev, (t - 1) % 2)
            if (not static) or 0 <= t < n_iter:
                qk_phase(g, make_specs(t), has_prev, t % 2)

        for t in range(2):
            step(t)
        if n_iter > 2:
            def body(t, carry):
                step(t)
                return carry

            lax.fori_loop(2, n_iter, body, 0)
        for t in range(max(n_iter, 2), n_iter + 2):
            step(t)

    qi = lax.broadcasted_iota(jnp.int32, (blk, 2 * blk), 0)
    kj = lax.broadcasted_iota(jnp.int32, (blk, 2 * blk), 1)
    dist = blk + qi - kj
    valid = (dist >= 0) & (dist <= blk)

    for g, dil in enumerate(dils):
        sub = seq // dil
        n_blk = sub // blk
        bias_base = jnp.where(valid, -(dist * dil).astype(F32), NEG_BIG)
        for a in range(2):
            slope = slopes_ref[g * n_heads + 2 * hp + a]
            bias_ref[a] = (slope * LOG2E) * bias_base

        u1, u2 = unrolls[g]

        def aligned(row):
            return row if isinstance(row, int) else pl.multiple_of(row, blk)

        def first_specs(i, sub=sub, u1=u1):
            return [(aligned((i * u1 + w) * sub), i * u1 + w) for w in range(u1)]

        def rest_specs(i, dil=dil, sub=sub, n_blk=n_blk, u2=u2):
            specs = []
            for w in range(u2):
                idx = i * u2 + w
                r = idx // (n_blk - 1)
                b = idx % (n_blk - 1) + 1
                specs.append((aligned(r * sub + b * blk), b * blk * dil + r))
            return specs

        pipelined(dil // u1, first_specs, g, False)
        n_rest = dil * (n_blk - 1)
        if n_rest:
            pipelined(n_rest // u2, rest_specs, g, True)

    rb = 256
    n_g = len(dils)

    def merge(i, carry):
        r0 = pl.multiple_of(i * rb, rb)
        ms = [max_ref[g, pl.ds(r0, rb), :] for g in range(n_g)]
        mx = ms[0]
        for g in range(1, n_g):
            mx = jnp.maximum(mx, ms[g])
        num = jnp.zeros((rb, LANES), F32)
        den = jnp.zeros((rb, LANES), F32)
        for g in range(n_g):
            w = jnp.exp2(ms[g] - mx)
            num = num + w * num_ref[g, pl.ds(r0, rb), :]
            den = den + w * den_ref[g, pl.ds(r0, rb), :]
        gate = g_ref[pl.ds(r0, rb), :].astype(F32)
        o_ref[pl.ds(r0, rb), :] = (num / den * gate).astype(BF16)
        return carry

    lax.fori_loop(0, seq // rb, merge, 0)


def _attention(proj, slopes):
    bsz, n_tiles, n_pairs, seq, _ = proj.shape
    assert n_tiles == 3 * B_N_GROUPS + 1
    dils = tuple(dil for _, dil in DILATED_GROUPS)
    unrolls = ((1, 5), (4, 6), (4, 1))
    for (u1, u2), dil in zip(unrolls, dils):
        n_rest = dil * (seq // dil // B_BLOCK - 1)
        assert dil % u1 == 0 and n_rest % u2 == 0

    def col_spec(tile):
        return pl.BlockSpec((None, None, None, seq, LANES), lambda b, h, s, tile=tile: (b, tile, h, 0, 0))

    kern = functools.partial(_attn_kernel, seq=seq, dils=dils, unrolls=unrolls)
    grid_spec = pltpu.PrefetchScalarGridSpec(
        num_scalar_prefetch=1,
        grid=(bsz, n_pairs),
        in_specs=[col_spec(t) for t in range(n_tiles)],
        out_specs=pl.BlockSpec((None, None, seq, LANES), lambda b, h, s: (b, h, 0, 0)),
        scratch_shapes=[pltpu.VMEM((B_N_GROUPS, seq, LANES), F32)] * 3
        + [pltpu.VMEM((2, B_BLOCK, 2 * B_BLOCK), F32),
           pltpu.VMEM((2, max(max(u) for u in unrolls), 2 * B_BLOCK, 2 * B_BLOCK), F32),
           pltpu.VMEM((2, max(max(u) for u in unrolls), 2 * B_BLOCK, 2 * B_BLOCK), BF16)],
    )
    return pl.pallas_call(
        kern,
        grid_spec=grid_spec,
        out_shape=jax.ShapeDtypeStruct((bsz, n_pairs, seq, LANES), BF16),
        compiler_params=_params(("arbitrary", "arbitrary")),
        name="dilated_attention",
    )(slopes, *([proj] * n_tiles))


def _gmlp_kernel(x_ref, yprev_ref, wprev_ref, nw_ref, w_ref, vw_ref, vb_ref, ws_ref, bias_ref, wout_ref, o_ref,
                 x1_ref, h_ref, u_ref, v_ref, g_ref, y_ref, *, tm):
    d = x_ref.shape[-1]
    cg = d // C_GROUPS
    rb = 256
    for i in range(tm // rb):
        rows = slice(i * rb, (i + 1) * rb)
        y_prev = jnp.concatenate([yprev_ref[hp, rows, :] for hp in range(yprev_ref.shape[0])], axis=1)
        x1_ref[rows, :] = x_ref[0, rows, :] + _dot(y_prev, wprev_ref[...])
        h_ref[rows, :] = _rms_rows(x1_ref[rows, :], nw_ref[...]).astype(BF16)
    for i in range(tm // rb):
        rows = slice(i * rb, (i + 1) * rb)
        hh = h_ref[rows, :]
        u_ref[rows, :] = _gelu_tanh(_dot(hh, w_ref[:, 0:d]))
        vv = _gelu_tanh(_dot(hh, w_ref[:, d:2 * d]))
        mu = jnp.mean(vv, axis=-1, keepdims=True)
        vc = vv - mu
        var = jnp.mean(vc * vc, axis=-1, keepdims=True)
        v_ref[rows, :] = (vc * lax.rsqrt(var + NORM_EPS) * vw_ref[...] + vb_ref[...]).astype(BF16)
        g_ref[rows, :] = _silu(_dot(hh, w_ref[:, 2 * d:3 * d]))
    ti = lax.broadcasted_iota(jnp.int32, (C_CHUNK, C_CHUNK), 0)
    si = lax.broadcasted_iota(jnp.int32, (C_CHUNK, C_CHUNK), 1)
    n_ch = tm // C_CHUNK
    for grp in range(C_GROUPS):
        cols = slice(grp * cg, (grp + 1) * cg)
        wc = jnp.where(si <= ti, ws_ref[grp], 0.0).astype(BF16)
        rhs = jnp.concatenate([v_ref[n * C_CHUNK:(n + 1) * C_CHUNK, cols] for n in range(n_ch)], axis=1)
        s = _dot(wc, rhs)
        bias = bias_ref[:, cols]
        for n in range(n_ch):
            rows = slice(n * C_CHUNK, (n + 1) * C_CHUNK)
            sn = s[:, n * cg:(n + 1) * cg] + bias
            y_ref[rows, cols] = (u_ref[rows, cols] * sn * g_ref[rows, cols]).astype(BF16)
    for i in range(tm // rb):
        rows = slice(i * rb, (i + 1) * rb)
        o_ref[0, rows, :] = x1_ref[rows, :] + _dot(y_ref[rows, :], wout_ref[...])


def _gmlp_mixer(x, y_prev_pairs, w_prev_bf16, norm_w_row, w_bf16, vw_row, vb_row, w_s, bias_full, w_out_bf16,
                tm=512):
    bsz, seq, d = x.shape
    n_pairs, lanes = y_prev_pairs.shape[1], y_prev_pairs.shape[3]
    kern = functools.partial(_gmlp_kernel, tm=tm)
    return pl.pallas_call(
        kern,
        grid=(bsz, seq // tm),
        in_specs=[pl.BlockSpec((1, tm, d), lambda b, i: (b, i, 0)),
                  pl.BlockSpec((None, n_pairs, tm, lanes), lambda b, i: (b, 0, i, 0)),
                  pl.BlockSpec((d, d), lambda b, i: (0, 0)),
                  pl.BlockSpec((1, d), lambda b, i: (0, 0)),
                  pl.BlockSpec((d, 3 * d), lambda b, i: (0, 0)),
                  pl.BlockSpec((1, d), lambda b, i: (0, 0)),
                  pl.BlockSpec((1, d), lambda b, i: (0, 0)),
                  pl.BlockSpec(w_s.shape, lambda b, i: (0, 0, 0)),
                  pl.BlockSpec((C_CHUNK, d), lambda b, i: (0, 0)),
                  pl.BlockSpec((d, d), lambda b, i: (0, 0))],
        out_specs=pl.BlockSpec((1, tm, d), lambda b, i: (b, i, 0)),
        out_shape=jax.ShapeDtypeStruct((bsz, seq, d), F32),
        scratch_shapes=[pltpu.VMEM((tm, d), F32),
                        pltpu.VMEM((tm, d), BF16),
                        pltpu.VMEM((tm, d), F32),
                        pltpu.VMEM((tm, d), BF16),
                        pltpu.VMEM((tm, d), F32),
                        pltpu.VMEM((tm, d), BF16)],
        compiler_params=_params(("arbitrary", "arbitrary")),
        name="gmlp_mixer",
    )(x, y_prev_pairs, w_prev_bf16, norm_w_row, w_bf16, vw_row, vb_row, w_s, bias_full, w_out_bf16)


def kernel(x, norm_w, a_w_in, a_lower_bounds, a_o_norm_w, a_w_out, b_w_in, b_q_norm_w, b_k_norm_w, b_w_out,
           c_w_in, c_v_norm_w, c_v_norm_b, c_w_s, c_b_s, c_w_out):
    bsz, seq, d = x.shape
    depth = norm_w.shape[0]
    n_mixers = 3
    assert seq == DILATED_GROUPS[-1][0] and seq % A_CHUNK == 0 and d % LANES == 0

    def residual(y, w_out, xin):
        out = _out_proj(y.reshape(bsz * seq, d), w_out.astype(BF16), xin.reshape(bsz * seq, d))
        return out.reshape(bsz, seq, d)

    pending = None
    for layer in range(depth):
        kind, idx = layer % n_mixers, layer // n_mixers
        nw = norm_w[layer][None, :]
        if kind == 0:
            y = _hgrn_mixer(x, nw, a_w_in[idx].astype(BF16), a_lower_bounds, a_o_norm_w[idx][None, :], idx)
            x = residual(y, a_w_out[idx], x)
        elif kind == 1:
            n_heads = d // B_HEAD_DIM
            n_total = B_N_GROUPS * n_heads
            slopes = jnp.exp2(-ALIBI_MAX_EXP * jnp.arange(1, n_total + 1, dtype=F32) / n_total)
            rows = []
            for g in range(B_N_GROUPS):
                rows.append(jnp.tile(b_q_norm_w[idx, g] * b_k_norm_w[idx, g], n_heads)
                            * (B_HEAD_DIM ** -0.5 * LOG2E))
                rows.append(jnp.ones((d,), F32))
                rows.append(jnp.ones((d,), F32))
            rows.append(jnp.ones((d,), F32))
            head_w = jnp.stack(rows)[:, None, :]
            proj = _attn_in_proj(x, nw, b_w_in[idx], head_w)
            pending = (_attention(proj, slopes), b_w_out[idx].astype(BF16))
            if layer == depth - 1:
                x = _out_proj_pairs(*pending, x)
        else:
            bias_full = jnp.repeat(c_b_s[idx].T, d // C_GROUPS, axis=1)
            x = _gmlp_mixer(x, *pending, nw, c_w_in[idx].astype(BF16), c_v_norm_w[idx][None, :],
                            c_v_norm_b[idx][None, :], c_w_s[idx], bias_full, c_w_out[idx].astype(BF16))
    return x
```

```python
import functools

import numpy as np
import jax
import jax.numpy as jnp
from jax import lax
from jax.experimental import pallas as pl
from jax.experimental.pallas import tpu as pltpu

F32 = jnp.float32
BF16 = jnp.bfloat16

NORM_EPS = 1e-6
NEG_BIG = -1e30
LB_FLOOR = 1e-30
LOG2E = float(np.log2(np.e))

LANES = 128
SUBLANES = 8
VMEM_LIMIT_BYTES = 60 * 1024 * 1024

A_HEAD_DIM = 128
A_CHUNK = 128
A_CHUNKS_PER_ITER = 2
A_HEADS_PER_STEP = 4
B_HEAD_DIM = 64
DILATED_GROUPS = ((128, 1), (512, 4), (2048, 16))
B_N_GROUPS = len(DILATED_GROUPS)
B_BLOCK = 128
ALIBI_MAX_EXP = 8.0
C_CHUNK = 128
C_GROUPS = 8


def _silu(x):
    hx = 0.5 * x
    return hx + hx * jnp.tanh(hx)


def _gelu_tanh(x):
    c = np.float32(np.sqrt(2.0 / np.pi))
    return 0.5 * x * (1.0 + jnp.tanh(c * (x + 0.044715 * (x * x * x))))


def _rms_rows(xf, w):
    ms = jnp.mean(xf * xf, axis=-1, keepdims=True)
    return xf * lax.rsqrt(ms + NORM_EPS) * w


def _dot(a, b):
    return jnp.dot(a, b, preferred_element_type=F32)


def _dot_nt(a, b):
    return lax.dot_general(a, b, (((1,), (1,)), ((), ())), preferred_element_type=F32)


def _dot_tn(a, b):
    return lax.dot_general(a, b, (((0,), (0,)), ((), ())), preferred_element_type=F32)


def _params(sem):
    return pltpu.CompilerParams(dimension_semantics=sem, vmem_limit_bytes=VMEM_LIMIT_BYTES)


def _out_proj_kernel(y_ref, w_ref, x_ref, o_ref):
    o_ref[...] = x_ref[...] + _dot(y_ref[...], w_ref[...])


def _out_proj(y2d, w_bf16, x2d, tm=2048):
    m, k = y2d.shape
    n = w_bf16.shape[1]
    return pl.pallas_call(
        _out_proj_kernel,
        grid=(m // tm,),
        in_specs=[pl.BlockSpec((tm, k), lambda i: (i, 0)),
                  pl.BlockSpec((k, n), lambda i: (0, 0)),
                  pl.BlockSpec((tm, n), lambda i: (i, 0))],
        out_specs=pl.BlockSpec((tm, n), lambda i: (i, 0)),
        out_shape=jax.ShapeDtypeStruct((m, n), F32),
        compiler_params=_params(("arbitrary",)),
        name="out_proj",
    )(y2d, w_bf16, x2d)


def _out_proj_pairs_kernel(y_ref, w_ref, x_ref, o_ref):
    y = jnp.concatenate([y_ref[hp] for hp in range(y_ref.shape[0])], axis=1)
    o_ref[...] = x_ref[...] + _dot(y, w_ref[...])


def _out_proj_pairs(y_pairs, w_bf16, x, tm=1024):
    bsz, n_pairs, seq, lanes = y_pairs.shape
    d = n_pairs * lanes
    n = w_bf16.shape[1]
    return pl.pallas_call(
        _out_proj_pairs_kernel,
        grid=(bsz, seq // tm),
        in_specs=[pl.BlockSpec((None, n_pairs, tm, lanes), lambda b, i: (b, 0, i, 0)),
                  pl.BlockSpec((d, n), lambda b, i: (0, 0)),
                  pl.BlockSpec((None, tm, n), lambda b, i: (b, i, 0))],
        out_specs=pl.BlockSpec((None, tm, n), lambda b, i: (b, i, 0)),
        out_shape=jax.ShapeDtypeStruct((bsz, seq, n), F32),
        compiler_params=_params(("arbitrary", "arbitrary")),
        name="out_proj_pairs",
    )(y_pairs, w_bf16, x)


A_BAND = 4


def _hgrn_levels(chunk):
    return [1 << j for j in range(int(np.log2(chunk))) if (2 << j) > A_BAND]


def _hgrn_level_exponents(b, b_view, sgn_ref, chunk):
    out = {}
    n_vregs = chunk // SUBLANES
    width = b.shape[-1]

    def row(r, n):
        one_group = jnp.broadcast_to(b_view[pl.ds(r, 1), :], (SUBLANES, width))
        return jnp.concatenate([one_group] * (n // SUBLANES), axis=0)

    for m in _hgrn_levels(chunk):
        if m >= SUBLANES:
            parts = []
            for j in range(chunk // (2 * m)):
                base = j * 2 * m
                ref_row = row(base + m - 1, m)
                parts.append(ref_row - b[base:base + m, :])
                parts.append(b[base + m:base + 2 * m, :] - ref_row)
            out[m] = jnp.concatenate(parts, axis=0)
        else:
            assert m == 4
            ref = jnp.concatenate([row(SUBLANES * i + 3, SUBLANES) for i in range(n_vregs)], axis=0)
            out[m] = (b - ref) * sgn_ref[...]
    return out


def _hgrn_kernel(x_ref, nw_ref, wq_ref, wf_ref, wi_ref, wg_ref, lbraw_ref, onw_ref, tri_ref, y_ref,
                 h_ref, act_ref, v16_ref, b_ref, st_ref, lvl_ref, band_ref, sgn_ref, qd_ref, kd_ref, a16_ref,
                 dec_ref,
                 *, layer_idx, seq, chunk, hps, cpi):
    step = pl.program_id(1)
    dh = A_HEAD_DIM
    rb = 256

    @pl.when(step == 0)
    def _():
        for i in range(seq // rb):
            xs = x_ref[0, i * rb:(i + 1) * rb, :]
            h_ref[i * rb:(i + 1) * rb, :] = _rms_rows(xs, nw_ref[...]).astype(BF16)

    raw = lbraw_ref[...]
    n_layers = raw.shape[0]
    mx = raw[0:1, :]
    for i in range(1, n_layers):
        mx = jnp.maximum(mx, raw[i:i + 1, :])
    es = [jnp.exp(raw[i:i + 1, :] - mx) for i in range(n_layers)]
    z = es[0]
    for i in range(1, n_layers):
        z = z + es[i]
    soft = [e / z for e in es]
    csum = soft[0]
    for i in range(1, layer_idx + 1):
        csum = csum + soft[i]
    lb = csum - soft[0]
    log_lb = jnp.log(jnp.maximum(lb, LB_FLOOR))
    log1m_lb = jnp.log1p(-lb)
    one_m_lb = 1.0 - lb
    heads = range(hps)

    pb = 512
    for i in range(seq // pb):
        rows = slice(i * pb, (i + 1) * pb)
        hh = h_ref[rows, :]
        res_q, res_f, res_i, res_g = (_dot(hh, w[...]) for w in (wq_ref, wf_ref, wi_ref, wg_ref))
        v16_ref[rows, :] = res_i.astype(BF16)
        for a in heads:
            o0 = a * 4 * dh
            lbs = slice(a * dh, (a + 1) * dh)
            pf = res_f[:, lbs]
            log_sig = jnp.minimum(pf, 0.0) - jnp.log(1.0 + jnp.exp(-jnp.abs(pf)))
            bb = log1m_lb[:, lbs] + log_sig
            ll = log_lb[:, lbs]
            log_f = jnp.maximum(ll, bb) + jnp.log(1.0 + jnp.exp(-jnp.abs(ll - bb)))
            act_ref[rows, o0:o0 + dh] = _silu(res_q[:, lbs])
            act_ref[rows, o0 + dh:o0 + 2 * dh] = one_m_lb[:, lbs] * jnp.exp(log_sig - pf)
            act_ref[rows, o0 + 2 * dh:o0 + 3 * dh] = _silu(res_g[:, lbs])
            act_ref[rows, o0 + 3 * dh:o0 + 4 * dh] = log_f * LOG2E

    ti = lax.broadcasted_iota(jnp.int32, (chunk, chunk), 0)
    si = lax.broadcasted_iota(jnp.int32, (chunk, chunk), 1)
    lvl_ref[...] = jnp.where(si < ti, ti ^ si, jnp.where(si == ti, 0, -1))
    band_ref[...] = jnp.where((si <= ti) & ((ti ^ si) < A_BAND), ti - si, -1)
    tr = lax.broadcasted_iota(jnp.int32, (chunk, dh), 0)
    sgn_ref[...] = jnp.where((tr & 4) != 0, 1.0, -1.0)
    st_ref[...] = jnp.zeros_like(st_ref)
    onw = onw_ref[...]
    levels = _hgrn_levels(chunk)

    def row_start(c):
        return c * chunk if isinstance(c, int) else pl.multiple_of(c * chunk, chunk)

    def front(chunks):
        q, k, log_f = [], [], []
        for c in chunks:
            for a in heads:
                o0 = a * 4 * dh
                q.append(act_ref[pl.ds(row_start(c), chunk), o0:o0 + dh])
                k.append(act_ref[pl.ds(row_start(c), chunk), o0 + dh:o0 + 2 * dh])
                log_f.append(act_ref[pl.ds(row_start(c), chunk), o0 + 3 * dh:o0 + 4 * dh])
        slots = range(len(q))

        b = []
        for a in slots:
            lf_hi = log_f[a].astype(BF16)
            lf_lo = (log_f[a] - lf_hi.astype(F32)).astype(BF16)
            b.append(_dot(tri_ref[...], lf_hi) + _dot(tri_ref[...], lf_lo))
            b_ref[a] = b[a]

        expo, acc = [], []
        for a in slots:
            expo.append(_hgrn_level_exponents(b[a], b_ref.at[a], sgn_ref, chunk))
            qd_ref[a] = (q[a] * jnp.exp2(b[a])).astype(BF16)
            b_last = b_ref[a, chunk - 1:chunk, :]
            kd_ref[a] = (k[a] * jnp.exp2(b_last - b[a])).astype(BF16)
            dec_ref[a] = jnp.exp2(b_last)
        n_grp = chunk // SUBLANES

        def shifted(x, dist):
            return pltpu.roll(x.reshape(n_grp, SUBLANES, dh), dist, axis=1).reshape(chunk, dh)

        for a in slots:
            acc.append(jnp.where(band_ref[...] == 0, jnp.sum(q[a] * k[a], axis=-1, keepdims=True), 0.0))
        for dist in range(1, A_BAND):
            for a in slots:
                decay = jnp.exp2(b[a] - shifted(b[a], dist))
                c = jnp.sum(q[a] * shifted(k[a], dist) * decay, axis=-1, keepdims=True)
                acc[a] = jnp.where(band_ref[...] == dist, c, acc[a])
        for m in levels:
            if m >= SUBLANES:
                break
            is_query_row = sgn_ref[...] > 0.0
            for a in slots:
                w = (jnp.where(is_query_row, q[a], k[a]) * jnp.exp2(expo[a][m])).astype(BF16)
                acc[a] = jnp.where(lvl_ref[...] >= m, _dot_nt(w, w), acc[a])
        n_grp = chunk // SUBLANES
        acc = [[acc_a[SUBLANES * i:SUBLANES * (i + 1), :] for i in range(n_grp)] for acc_a in acc]
        for m in levels:
            if m < SUBLANES:
                continue
            is_query = [(SUBLANES * i) % (2 * m) >= m for i in range(n_grp)]
            for a in slots:
                roles = jnp.concatenate([(q[a] if is_query[i] else k[a])[SUBLANES * i:SUBLANES * (i + 1), :]
                                         for i in range(n_grp)], axis=0)
                w = roles * jnp.exp2(expo[a][m])
                q_m = jnp.concatenate([w[SUBLANES * i:SUBLANES * (i + 1), :]
                                       for i in range(n_grp) if is_query[i]], axis=0)
                a_m = _dot_nt(q_m.astype(BF16), w.astype(BF16))
                for n, i in enumerate([i for i in range(n_grp) if is_query[i]]):
                    rows = slice(SUBLANES * i, SUBLANES * (i + 1))
                    acc[a][i] = jnp.where(lvl_ref[rows, :] >= m,
                                          a_m[SUBLANES * n:SUBLANES * (n + 1), :], acc[a][i])
        for a in slots:
            a16_ref[a] = jnp.concatenate(acc[a], axis=0).astype(BF16)

    def back(chunks):
        for n, c in enumerate(chunks):
            r0 = row_start(c)
            for a in heads:
                o0 = a * 4 * dh
                slot = n * hps + a
                v16 = v16_ref[pl.ds(r0, chunk), a * dh:(a + 1) * dh]
                st = st_ref[a]
                oa = _dot_nt(qd_ref[slot], st.astype(BF16)) + _dot(a16_ref[slot], v16)
                st_ref[a] = st * dec_ref[slot] + _dot_tn(v16, kd_ref[slot])
                y = _rms_rows(oa, onw) * act_ref[pl.ds(r0, chunk), o0 + 2 * dh:o0 + 3 * dh]
                y_ref[0, pl.ds(r0, chunk), a * dh:(a + 1) * dh] = y.astype(BF16)

    n_iter = seq // (chunk * cpi)
    front([n for n in range(cpi)])

    def body(i, carry):
        back([(i - 1) * cpi + n for n in range(cpi)])
        front([i * cpi + n for n in range(cpi)])
        return carry

    lax.fori_loop(1, n_iter, body, 0)
    back([(n_iter - 1) * cpi + n for n in range(cpi)])


def _hgrn_mixer(x, norm_w_row, w_in, lb_raw, o_norm_w_row, layer_idx):
    bsz, seq, d = x.shape
    dh = A_HEAD_DIM
    hps = A_HEADS_PER_STEP
    n_steps = d // (dh * hps)
    chunk = A_CHUNK
    tri = jnp.asarray(np.tril(np.ones((chunk, chunk), np.float32)), BF16)
    cpi = A_CHUNKS_PER_ITER
    n_slots = hps * cpi
    assert seq % (chunk * cpi) == 0
    kern = functools.partial(_hgrn_kernel, layer_idx=layer_idx, seq=seq, chunk=chunk, hps=hps, cpi=cpi)

    def w_spec(part):
        return pl.BlockSpec((d, hps * dh), lambda b, h, part=part: (0, part * n_steps + h))

    return pl.pallas_call(
        kern,
        grid=(bsz, n_steps),
        in_specs=[pl.BlockSpec((1, seq, d), lambda b, h: (b, 0, 0)),
                  pl.BlockSpec((1, d), lambda b, h: (0, 0)),
                  w_spec(0), w_spec(1), w_spec(2), w_spec(3),
                  pl.BlockSpec((lb_raw.shape[0], hps * dh), lambda b, h: (0, h)),
                  pl.BlockSpec((1, dh), lambda b, h: (0, 0)),
                  pl.BlockSpec((chunk, chunk), lambda b, h: (0, 0))],
        out_specs=pl.BlockSpec((1, seq, hps * dh), lambda b, h: (b, 0, h)),
        out_shape=jax.ShapeDtypeStruct((bsz, seq, d), BF16),
        scratch_shapes=[pltpu.VMEM((seq, d), BF16),
                        pltpu.VMEM((seq, hps * 4 * dh), F32),
                        pltpu.VMEM((seq, hps * dh), BF16),
                        pltpu.VMEM((n_slots, chunk, dh), F32),
                        pltpu.VMEM((hps, dh, dh), F32),
                        pltpu.VMEM((chunk, chunk), jnp.int32),
                        pltpu.VMEM((chunk, chunk), jnp.int32),
                        pltpu.VMEM((chunk, dh), F32),
                        pltpu.VMEM((n_slots, chunk, dh), BF16),
                        pltpu.VMEM((n_slots, chunk, dh), BF16),
                        pltpu.VMEM((n_slots, chunk, chunk), BF16),
                        pltpu.VMEM((n_slots, 1, dh), F32)],
        compiler_params=_params(("arbitrary", "arbitrary")),
        name="hgrn_mixer",
    )(x, norm_w_row, w_in, w_in, w_in, w_in, lb_raw, o_norm_w_row, tri)


def _bproj_kernel(x_ref, nw_ref, w_ref, wkt_ref, hw_ref, o_ref, h_ref, hn_ref,
                  *, seq, dils):
    j = pl.program_id(1)
    n_qkv_tiles = 3 * len(dils)

    @pl.when(j == 0)
    def _():
        d = x_ref.shape[-1]
        n_planes = d // LANES
        rb = 256
        for i in range(seq // rb):
            hn = _rms_rows(x_ref[0, i * rb:(i + 1) * rb, :], nw_ref[...])
            for c in range(n_planes):
                hn_ref[c, i * rb:(i + 1) * rb, :] = hn[:, c * LANES:(c + 1) * LANES]
        for g, dil in enumerate(dils):
            sub = seq // dil
            for r in range(dil):
                for c0 in range(0, sub, rb):
                    rows = min(rb, sub - c0)
                    if dil == 1:
                        idx = pl.ds(c0, rows)
                    else:
                        idx = pl.ds(r + c0 * dil, rows, stride=dil)
                    for c in range(n_planes):
                        h_ref[g, r * sub + c0:r * sub + c0 + rows, c * LANES:(c + 1) * LANES] = (
                            hn_ref[c, idx, :].astype(BF16))

    grp = jnp.where(j < n_qkv_tiles, j // 3, 0)
    kind = jnp.where(j < n_qkv_tiles, j % 3, 3)
    pb = 512

    def tile(i):
        return _dot(h_ref[grp, i * pb:(i + 1) * pb, :], w_ref[...])

    def store(i, val):
        for hp in range(val.shape[-1] // LANES):
            o_ref[hp, i * pb:(i + 1) * pb, :] = val[:, hp * LANES:(hp + 1) * LANES].astype(BF16)

    @pl.when(kind == 1)
    def _():
        d = x_ref.shape[-1]
        hd = B_HEAD_DIM
        for i in range(seq // pb):
            yt = _dot_nt(wkt_ref[...], h_ref[grp, i * pb:(i + 1) * pb, :])
            for h in range(d // hd):
                yh = yt[h * hd:(h + 1) * hd, :]
                r = lax.rsqrt(jnp.mean(yh * yh, axis=0, keepdims=True) + NORM_EPS)
                yh = (yh * r).astype(BF16)
                hp, half = divmod(h * hd, LANES)
                for kb in range(pb // LANES):
                    row0 = (i * (pb // LANES) + kb) * LANES + half
                    o_ref[hp, row0:row0 + hd, :] = yh[:, kb * LANES:(kb + 1) * LANES]

    @pl.when(kind == 0)
    def _():
        first_head = lax.broadcasted_iota(jnp.int32, (pb, LANES), 1) < B_HEAD_DIM
        for i in range(seq // pb):
            y = tile(i)
            for hp in range(y.shape[-1] // LANES):
                cols = slice(hp * LANES, (hp + 1) * LANES)
                yc = y[:, cols]
                sq = yc * yc
                ss_a = jnp.sum(jnp.where(first_head, sq, 0.0), axis=-1, keepdims=True)
                ss_b = jnp.sum(jnp.where(first_head, 0.0, sq), axis=-1, keepdims=True)
                r_a = lax.rsqrt(ss_a * (1.0 / B_HEAD_DIM) + NORM_EPS)
                r_b = lax.rsqrt(ss_b * (1.0 / B_HEAD_DIM) + NORM_EPS)
                o_ref[hp, i * pb:(i + 1) * pb, :] = (
                    yc * jnp.where(first_head, r_a, r_b) * hw_ref[0, :, cols]).astype(BF16)

    @pl.when(kind == 2)
    def _():
        for i in range(seq // pb):
            store(i, tile(i))

    @pl.when(kind == 3)
    def _():
        for i in range(seq // pb):
            store(i, _silu(tile(i)))


def _attn_in_proj(x, norm_w_row, w_in, head_w):
    bsz, seq, d = x.shape
    n_out = w_in.shape[1]
    tn = d
    dils = tuple(dil for _, dil in DILATED_GROUPS)
    n_grp = len(dils)
    w_k = jnp.stack([w_in[:, (3 * g + 1) * tn:(3 * g + 2) * tn] for g in range(n_grp)])
    w_k_t = jnp.transpose(w_k, (0, 2, 1)).astype(BF16)
    kern = functools.partial(_bproj_kernel, seq=seq, dils=dils)
    return pl.pallas_call(
        kern,
        grid=(bsz, n_out // tn),
        in_specs=[pl.BlockSpec((1, seq, d), lambda b, j: (b, 0, 0)),
                  pl.BlockSpec((1, d), lambda b, j: (0, 0)),
                  pl.BlockSpec((d, tn), lambda b, j: (0, j)),
                  pl.BlockSpec((None, tn, d), lambda b, j: (jnp.minimum(j // 3, n_grp - 1), 0, 0)),
                  pl.BlockSpec((1, 1, tn), lambda b, j: (j, 0, 0))],
        out_specs=pl.BlockSpec((None, None, tn // LANES, seq, LANES), lambda b, j: (b, j, 0, 0, 0)),
        out_shape=jax.ShapeDtypeStruct((bsz, n_out // tn, tn // LANES, seq, LANES), BF16),
        scratch_shapes=[pltpu.VMEM((len(dils), seq, d), BF16),
                        pltpu.VMEM((d // LANES, seq, LANES), F32)],
        compiler_params=_params(("arbitrary", "arbitrary")),
        name="attn_in_proj",
    )(x, norm_w_row, w_in.astype(BF16), w_k_t, head_w)


def _attn_kernel(slopes_ref, q0, k0, v0, q1, k1, v1, q2, k2, v2, g_ref, o_ref,
                 num_ref, den_ref, max_ref, bias_ref, s_ref, p_ref, *, seq, dils, unrolls):
    hp = pl.program_id(1)
    blk = B_BLOCK
    hd = B_HEAD_DIM
    n_heads = (LANES // hd) * pl.num_programs(1)
    qkv = ((q0, k0, v0), (q1, k1, v1), (q2, k2, v2))
    head_a = lax.broadcasted_iota(jnp.int32, (blk, LANES), 1) < hd


    def token_rows(g, tok0):
        return pl.ds(tok0, blk) if dils[g] == 1 else pl.ds(tok0, blk, stride=dils[g])

    def qk_phase(g, specs, has_prev, slot):
        q_ref, kt_ref, _ = qkv[g]
        nk = 2 * blk if has_prev else blk
        for w, (row0, _) in enumerate(specs):
            q2d = q_ref[pl.ds(row0, blk), :]
            zero = jnp.zeros_like(q2d)
            qq = jnp.concatenate([jnp.where(head_a, q2d, zero), jnp.where(head_a, zero, q2d)], axis=0)
            kt = kt_ref[pl.ds(row0, blk), :]
            if has_prev:
                kt = jnp.concatenate([kt_ref[pl.ds(row0 - blk, blk), :], kt], axis=1)
            s_ref[slot, w, :, 0:nk] = _dot(qq, kt)

    def softmax_phase(g, specs, has_prev, slot):
        nk = 2 * blk if has_prev else blk
        off = 0 if has_prev else blk
        for w, (_, tok0) in enumerate(specs):
            ms, ls = [], []
            for a in range(2):
                sa = s_ref[slot, w, a * blk:(a + 1) * blk, 0:nk] + bias_ref[a, :, off:off + nk]
                m = jnp.max(sa, axis=-1, keepdims=True)
                e = jnp.exp2(sa - m)
                p_ref[slot, w, a * blk:(a + 1) * blk, 0:nk] = e.astype(BF16)
                ms.append(m)
                ls.append(jnp.sum(e, axis=-1, keepdims=True))
            idx = token_rows(g, tok0)
            max_ref[g, idx, :] = jnp.where(head_a, ms[0], ms[1])
            den_ref[g, idx, :] = jnp.where(head_a, ls[0], ls[1])

    def pv_phase(g, specs, has_prev, slot):
        v_ref = qkv[g][2]
        nk = 2 * blk if has_prev else blk
        for w, (row0, tok0) in enumerate(specs):
            k_lo = row0 - blk if has_prev else row0
            o = _dot(p_ref[slot, w, :, 0:nk], v_ref[pl.ds(k_lo, nk), :])
            num_ref[g, token_rows(g, tok0), :] = jnp.where(head_a, o[0:blk, :], o[blk:2 * blk, :])

    def pipelined(n_iter, make_specs, g, has_prev):
        def step(t):
            static = isinstance(t, int)
            if (not static) or 0 <= t - 2 < n_iter:
                pv_phase(g, make_specs(t - 2), has_prev, t % 2)
            if (not static) or 0 <= t - 1 < n_iter:
                softmax_phase(g, make_specs(t - 1), has_prev, (t - 1) % 2)
            if (not static) or 0 <= t < n_iter:
                qk_phase(g, make_specs(t), has_prev, t % 2)

        for t in range(2):
            step(t)
        if n_iter > 2:
            def body(t, carry):
                step(t)
                return carry

            lax.fori_loop(2, n_iter, body, 0)
        for t in range(max(n_iter, 2), n_iter + 2):
            step(t)

    qi = lax.broadcasted_iota(jnp.int32, (blk, 2 * blk), 0)
    kj = lax.broadcasted_iota(jnp.int32, (blk, 2 * blk), 1)
    dist = blk + qi - kj
    valid = (dist >= 0) & (dist <= blk)

    for g, dil in enumerate(dils):
        sub = seq // dil
        n_blk = sub // blk
        bias_base = jnp.where(valid, -(dist * dil).astype(F32), NEG_BIG)
        for a in range(2):
            slope = slopes_ref[g * n_heads + 2 * hp + a]
            bias_ref[a] = (slope * LOG2E) * bias_base

        u1, u2 = unrolls[g]

        def aligned(row):
            return row if isinstance(row, int) else pl.multiple_of(row, blk)

        def first_specs(i, sub=sub, u1=u1):
            return [(aligned((i * u1 + w) * sub), i * u1 + w) for w in range(u1)]

        def rest_specs(i, dil=dil, sub=sub, n_blk=n_blk, u2=u2):
            specs = []
            for w in range(u2):
                idx = i * u2 + w
                r = idx // (n_blk - 1)
                b = idx % (n_blk - 1) + 1
                specs.append((aligned(r * sub + b * blk), b * blk * dil + r))
            return specs

        pipelined(dil // u1, first_specs, g, False)
        n_rest = dil * (n_blk - 1)
        if n_rest:
            pipelined(n_rest // u2, rest_specs, g, True)

    rb = 256
    n_g = len(dils)

    def merge(i, carry):
        r0 = pl.multiple_of(i * rb, rb)
        ms = [max_ref[g, pl.ds(r0, rb), :] for g in range(n_g)]
        mx = ms[0]
        for g in range(1, n_g):
            mx = jnp.maximum(mx, ms[g])
        num = jnp.zeros((rb, LANES), F32)
        den = jnp.zeros((rb, LANES), F32)
        for g in range(n_g):
            w = jnp.exp2(ms[g] - mx)
            num = num + w * num_ref[g, pl.ds(r0, rb), :]
            den = den + w * den_ref[g, pl.ds(r0, rb), :]
        gate = g_ref[pl.ds(r0, rb), :].astype(F32)
        o_ref[pl.ds(r0, rb), :] = (num / den * gate).astype(BF16)
        return carry

    lax.fori_loop(0, seq // rb, merge, 0)


def _attention(proj, slopes):
    bsz, n_tiles, n_pairs, seq, _ = proj.shape
    assert n_tiles == 3 * B_N_GROUPS + 1
    dils = tuple(dil for _, dil in DILATED_GROUPS)
    unrolls = ((1, 5), (4, 6), (4, 1))
    for (u1, u2), dil in zip(unrolls, dils):
        n_rest = dil * (seq // dil // B_BLOCK - 1)
        assert dil % u1 == 0 and n_rest % u2 == 0

    def col_spec(tile):
        return pl.BlockSpec((None, None, None, seq, LANES), lambda b, h, s, tile=tile: (b, tile, h, 0, 0))

    kern = functools.partial(_attn_kernel, seq=seq, dils=dils, unrolls=unrolls)
    grid_spec = pltpu.PrefetchScalarGridSpec(
        num_scalar_prefetch=1,
        grid=(bsz, n_pairs),
        in_specs=[col_spec(t) for t in range(n_tiles)],
        out_specs=pl.BlockSpec((None, None, seq, LANES), lambda b, h, s: (b, h, 0, 0)),
        scratch_shapes=[pltpu.VMEM((B_N_GROUPS, seq, LANES), F32)] * 3
        + [pltpu.VMEM((2, B_BLOCK, 2 * B_BLOCK), F32),
           pltpu.VMEM((2, max(max(u) for u in unrolls), 2 * B_BLOCK, 2 * B_BLOCK), F32),
           pltpu.VMEM((2, max(max(u) for u in unrolls), 2 * B_BLOCK, 2 * B_BLOCK), BF16)],
    )
    return pl.pallas_call(
        kern,
        grid_spec=grid_spec,
        out_shape=jax.ShapeDtypeStruct((bsz, n_pairs, seq, LANES), BF16),
        compiler_params=_params(("arbitrary", "arbitrary")),
        name="dilated_attention",
    )(slopes, *([proj] * n_tiles))


def _gmlp_kernel(x_ref, yprev_ref, wprev_ref, nw_ref, w_ref, vw_ref, vb_ref, ws_ref, bias_ref, wout_ref, o_ref,
                 x1_ref, h_ref, u_ref, v_ref, g_ref, y_ref, *, tm):
    d = x_ref.shape[-1]
    cg = d // C_GROUPS
    rb = 256
    for i in range(tm // rb):
        rows = slice(i * rb, (i + 1) * rb)
        y_prev = jnp.concatenate([yprev_ref[hp, rows, :] for hp in range(yprev_ref.shape[0])], axis=1)
        x1_ref[rows, :] = x_ref[0, rows, :] + _dot(y_prev, wprev_ref[...])
        h_ref[rows, :] = _rms_rows(x1_ref[rows, :], nw_ref[...]).astype(BF16)
    for i in range(tm // rb):
        rows = slice(i * rb, (i + 1) * rb)
        hh = h_ref[rows, :]
        u_ref[rows, :] = _gelu_tanh(_dot(hh, w_ref[:, 0:d]))
        vv = _gelu_tanh(_dot(hh, w_ref[:, d:2 * d]))
        mu = jnp.mean(vv, axis=-1, keepdims=True)
        vc = vv - mu
        var = jnp.mean(vc * vc, axis=-1, keepdims=True)
        v_ref[rows, :] = (vc * lax.rsqrt(var + NORM_EPS) * vw_ref[...] + vb_ref[...]).astype(BF16)
        g_ref[rows, :] = _silu(_dot(hh, w_ref[:, 2 * d:3 * d]))
    ti = lax.broadcasted_iota(jnp.int32, (C_CHUNK, C_CHUNK), 0)
    si = lax.broadcasted_iota(jnp.int32, (C_CHUNK, C_CHUNK), 1)
    n_ch = tm // C_CHUNK
    for grp in range(C_GROUPS):
        cols = slice(grp * cg, (grp + 1) * cg)
        wc = jnp.where(si <= ti, ws_ref[grp], 0.0).astype(BF16)
        rhs = jnp.concatenate([v_ref[n * C_CHUNK:(n + 1) * C_CHUNK, cols] for n in range(n_ch)], axis=1)
        s = _dot(wc, rhs)
        bias = bias_ref[:, cols]
        for n in range(n_ch):
            rows = slice(n * C_CHUNK, (n + 1) * C_CHUNK)
            sn = s[:, n * cg:(n + 1) * cg] + bias
            y_ref[rows, cols] = (u_ref[rows, cols] * sn * g_ref[rows, cols]).astype(BF16)
    for i in range(tm // rb):
        rows = slice(i * rb, (i + 1) * rb)
        o_ref[0, rows, :] = x1_ref[rows, :] + _dot(y_ref[rows, :], wout_ref[...])


def _gmlp_mixer(x, y_prev_pairs, w_prev_bf16, norm_w_row, w_bf16, vw_row, vb_row, w_s, bias_full, w_out_bf16,
                tm=512):
    bsz, seq, d = x.shape
    n_pairs, lanes = y_prev_pairs.shape[1], y_prev_pairs.shape[3]
    kern = functools.partial(_gmlp_kernel, tm=tm)
    return pl.pallas_call(
        kern,
        grid=(bsz, seq // tm),
        in_specs=[pl.BlockSpec((1, tm, d), lambda b, i: (b, i, 0)),
                  pl.BlockSpec((None, n_pairs, tm, lanes), lambda b, i: (b, 0, i, 0)),
                  pl.BlockSpec((d, d), lambda b, i: (0, 0)),
                  pl.BlockSpec((1, d), lambda b, i: (0, 0)),
                  pl.BlockSpec((d, 3 * d), lambda b, i: (0, 0)),
                  pl.BlockSpec((1, d), lambda b, i: (0, 0)),
                  pl.BlockSpec((1, d), lambda b, i: (0, 0)),
                  pl.BlockSpec(w_s.shape, lambda b, i: (0, 0, 0)),
                  pl.BlockSpec((C_CHUNK, d), lambda b, i: (0, 0)),
                  pl.BlockSpec((d, d), lambda b, i: (0, 0))],
        out_specs=pl.BlockSpec((1, tm, d), lambda b, i: (b, i, 0)),
        out_shape=jax.ShapeDtypeStruct((bsz, seq, d), F32),
        scratch_shapes=[pltpu.VMEM((tm, d), F32),
                        pltpu.VMEM((tm, d), BF16),
                        pltpu.VMEM((tm, d), F32),
                        pltpu.VMEM((tm, d), BF16),
                        pltpu.VMEM((tm, d), F32),
                        pltpu.VMEM((tm, d), BF16)],
        compiler_params=_params(("arbitrary", "arbitrary")),
        name="gmlp_mixer",
    )(x, y_prev_pairs, w_prev_bf16, norm_w_row, w_bf16, vw_row, vb_row, w_s, bias_full, w_out_bf16)


def kernel(x, norm_w, a_w_in, a_lower_bounds, a_o_norm_w, a_w_out, b_w_in, b_q_norm_w, b_k_norm_w, b_w_out,
           c_w_in, c_v_norm_w, c_v_norm_b, c_w_s, c_b_s, c_w_out):
    bsz, seq, d = x.shape
    depth = norm_w.shape[0]
    n_mixers = 3
    assert seq == DILATED_GROUPS[-1][0] and seq % A_CHUNK == 0 and d % LANES == 0

    def residual(y, w_out, xin):
        out = _out_proj(y.reshape(bsz * seq, d), w_out.astype(BF16), xin.reshape(bsz * seq, d))
        return out.reshape(bsz, seq, d)

    pending = None
    for layer in range(depth):
        kind, idx = layer % n_mixers, layer // n_mixers
        nw = norm_w[layer][None, :]
        if kind == 0:
            y = _hgrn_mixer(x, nw, a_w_in[idx].astype(BF16), a_lower_bounds, a_o_norm_w[idx][None, :], idx)
            x = residual(y, a_w_out[idx], x)
        elif kind == 1:
            n_heads = d // B_HEAD_DIM
            n_total = B_N_GROUPS * n_heads
            slopes = jnp.exp2(-ALIBI_MAX_EXP * jnp.arange(1, n_total + 1, dtype=F32) / n_total)
            rows = []
            for g in range(B_N_GROUPS):
                rows.append(jnp.tile(b_q_norm_w[idx, g] * b_k_norm_w[idx, g], n_heads)
                            * (B_HEAD_DIM ** -0.5 * LOG2E))
                rows.append(jnp.ones((d,), F32))
                rows.append(jnp.ones((d,), F32))
            rows.append(jnp.ones((d,), F32))
            head_w = jnp.stack(rows)[:, None, :]
            proj = _attn_in_proj(x, nw, b_w_in[idx], head_w)
            pending = (_attention(proj, slopes), b_w_out[idx].astype(BF16))
            if layer == depth - 1:
                x = _out_proj_pairs(*pending, x)
        else:
            bias_full = jnp.repeat(c_b_s[idx].T, d // C_GROUPS, axis=1)
            x = _gmlp_mixer(x, *pending, nw, c_w_in[idx].astype(BF16), c_v_norm_w[idx][None, :],
                            c_v_norm_b[idx][None, :], c_w_s[idx], bias_full, c_w_out[idx].astype(BF16))
    return x
```

```python
import functools

import numpy as np
import jax
import jax.numpy as jnp
from jax import lax
from jax.experimental import pallas as pl
from jax.experimental.pallas import tpu as pltpu

F32 = jnp.float32
BF16 = jnp.bfloat16

NORM_EPS = 1e-6
NEG_BIG = -1e30
LB_FLOOR = 1e-30
LOG2E = float(np.log2(np.e))

LANES = 128
SUBLANES = 8
VMEM_LIMIT_BYTES = 60 * 1024 * 1024

A_HEAD_DIM = 128
A_CHUNK = 128
A_CHUNKS_PER_ITER = 2
A_HEADS_PER_STEP = 4
B_HEAD_DIM = 64
DILATED_GROUPS = ((128, 1), (512, 4), (2048, 16))
B_N_GROUPS = len(DILATED_GROUPS)
B_BLOCK = 128
ALIBI_MAX_EXP = 8.0
C_CHUNK = 128
C_GROUPS = 8


def _silu(x):
    hx = 0.5 * x
    return hx + hx * jnp.tanh(hx)


def _gelu_tanh(x):
    c = np.float32(np.sqrt(2.0 / np.pi))
    return 0.5 * x * (1.0 + jnp.tanh(c * (x + 0.044715 * (x * x * x))))


def _rms_rows(xf, w):
    ms = jnp.mean(xf * xf, axis=-1, keepdims=True)
    return xf * lax.rsqrt(ms + NORM_EPS) * w


def _dot(a, b):
    return jnp.dot(a, b, preferred_element_type=F32)


def _dot_nt(a, b):
    return lax.dot_general(a, b, (((1,), (1,)), ((), ())), preferred_element_type=F32)


def _dot_tn(a, b):
    return lax.dot_general(a, b, (((0,), (0,)), ((), ())), preferred_element_type=F32)


def _params(sem):
    return pltpu.CompilerParams(dimension_semantics=sem, vmem_limit_bytes=VMEM_LIMIT_BYTES)


def _out_proj_kernel(y_ref, w_ref, x_ref, o_ref):
    o_ref[...] = x_ref[...] + _dot(y_ref[...], w_ref[...])


def _out_proj(y2d, w_bf16, x2d, tm=2048):
    m, k = y2d.shape
    n = w_bf16.shape[1]
    return pl.pallas_call(
        _out_proj_kernel,
        grid=(m // tm,),
        in_specs=[pl.BlockSpec((tm, k), lambda i: (i, 0)),
                  pl.BlockSpec((k, n), lambda i: (0, 0)),
                  pl.BlockSpec((tm, n), lambda i: (i, 0))],
        out_specs=pl.BlockSpec((tm, n), lambda i: (i, 0)),
        out_shape=jax.ShapeDtypeStruct((m, n), F32),
        compiler_params=_params(("arbitrary",)),
        name="out_proj",
    )(y2d, w_bf16, x2d)


def _out_proj_pairs_kernel(y_ref, w_ref, x_ref, o_ref):
    y = jnp.concatenate([y_ref[hp] for hp in range(y_ref.shape[0])], axis=1)
    o_ref[...] = x_ref[...] + _dot(y, w_ref[...])


def _out_proj_pairs(y_pairs, w_bf16, x, tm=1024):
    bsz, n_pairs, seq, lanes = y_pairs.shape
    d = n_pairs * lanes
    n = w_bf16.shape[1]
    return pl.pallas_call(
        _out_proj_pairs_kernel,
        grid=(bsz, seq // tm),
        in_specs=[pl.BlockSpec((None, n_pairs, tm, lanes), lambda b, i: (b, 0, i, 0)),
                  pl.BlockSpec((d, n), lambda b, i: (0, 0)),
                  pl.BlockSpec((None, tm, n), lambda b, i: (b, i, 0))],
        out_specs=pl.BlockSpec((None, tm, n), lambda b, i: (b, i, 0)),
        out_shape=jax.ShapeDtypeStruct((bsz, seq, n), F32),
        compiler_params=_params(("arbitrary", "arbitrary")),
        name="out_proj_pairs",
    )(y_pairs, w_bf16, x)


A_BAND = 4


def _hgrn_levels(chunk):
    return [1 << j for j in range(int(np.log2(chunk))) if (2 << j) > A_BAND]


def _hgrn_level_exponents(b, b_view, sgn_ref, chunk):
    out = {}
    n_vregs = chunk // SUBLANES
    width = b.shape[-1]

    def row(r, n):
        one_group = jnp.broadcast_to(b_view[pl.ds(r, 1), :], (SUBLANES, width))
        return jnp.concatenate([one_group] * (n // SUBLANES), axis=0)

    for m in _hgrn_levels(chunk):
        if m >= SUBLANES:
            parts = []
            for j in range(chunk // (2 * m)):
                base = j * 2 * m
                ref_row = row(base + m - 1, m)
                parts.append(ref_row - b[base:base + m, :])
                parts.append(b[base + m:base + 2 * m, :] - ref_row)
            out[m] = jnp.concatenate(parts, axis=0)
        else:
            assert m == 4
            ref = jnp.concatenate([row(SUBLANES * i + 3, SUBLANES) for i in range(n_vregs)], axis=0)
            out[m] = (b - ref) * sgn_ref[...]
    return out


def _hgrn_kernel(x_ref, nw_ref, wq_ref, wf_ref, wi_ref, wg_ref, lbraw_ref, onw_ref, tri_ref, y_ref,
                 h_ref, act_ref, v16_ref, b_ref, st_ref, lvl_ref, band_ref, sgn_ref, qd_ref, kd_ref, a16_ref,
                 dec_ref,
                 *, layer_idx, seq, chunk, hps, cpi):
    step = pl.program_id(1)
    dh = A_HEAD_DIM
    rb = 256

    @pl.when(step == 0)
    def _():
        for i in range(seq // rb):
            xs = x_ref[0, i * rb:(i + 1) * rb, :]
            h_ref[i * rb:(i + 1) * rb, :] = _rms_rows(xs, nw_ref[...]).astype(BF16)

    raw = lbraw_ref[...]
    n_layers = raw.shape[0]
    mx = raw[0:1, :]
    for i in range(1, n_layers):
        mx = jnp.maximum(mx, raw[i:i + 1, :])
    es = [jnp.exp(raw[i:i + 1, :] - mx) for i in range(n_layers)]
    z = es[0]
    for i in range(1, n_layers):
        z = z + es[i]
    soft = [e / z for e in es]
    csum = soft[0]
    for i in range(1, layer_idx + 1):
        csum = csum + soft[i]
    lb = csum - soft[0]
    log_lb = jnp.log(jnp.maximum(lb, LB_FLOOR))
    log1m_lb = jnp.log1p(-lb)
    one_m_lb = 1.0 - lb
    heads = range(hps)

    pb = 512
    for i in range(seq // pb):
        rows = slice(i * pb, (i + 1) * pb)
        hh = h_ref[rows, :]
        res_q, res_f, res_i, res_g = (_dot(hh, w[...]) for w in (wq_ref, wf_ref, wi_ref, wg_ref))
        v16_ref[rows, :] = res_i.astype(BF16)
        for a in heads:
            o0 = a * 4 * dh
            lbs = slice(a * dh, (a + 1) * dh)
            pf = res_f[:, lbs]
            log_sig = jnp.minimum(pf, 0.0) - jnp.log(1.0 + jnp.exp(-jnp.abs(pf)))
            bb = log1m_lb[:, lbs] + log_sig
            ll = log_lb[:, lbs]
            log_f = jnp.maximum(ll, bb) + jnp.log(1.0 + jnp.exp(-jnp.abs(ll - bb)))
            act_ref[rows, o0:o0 + dh] = _silu(res_q[:, lbs])
            act_ref[rows, o0 + dh:o0 + 2 * dh] = one_m_lb[:, lbs] * jnp.exp(log_sig - pf)
            act_ref[rows, o0 + 2 * dh:o0 + 3 * dh] = _silu(res_g[:, lbs])
            act_ref[rows, o0 + 3 * dh:o0 + 4 * dh] = log_f * LOG2E

    ti = lax.broadcasted_iota(jnp.int32, (chunk, chunk), 0)
    si = lax.broadcasted_iota(jnp.int32, (chunk, chunk), 1)
    lvl_ref[...] = jnp.where(si < ti, ti ^ si, jnp.where(si == ti, 0, -1))
    band_ref[...] = jnp.where((si <= ti) & ((ti ^ si) < A_BAND), ti - si, -1)
    tr = lax.broadcasted_iota(jnp.int32, (chunk, dh), 0)
    sgn_ref[...] = jnp.where((tr & 4) != 0, 1.0, -1.0)
    st_ref[...] = jnp.zeros_like(st_ref)
    onw = onw_ref[...]
    levels = _hgrn_levels(chunk)

    def row_start(c):
        return c * chunk if isinstance(c, int) else pl.multiple_of(c * chunk, chunk)

    def front(chunks):
        q, k, log_f = [], [], []
        for c in chunks:
            for a in heads:
                o0 = a * 4 * dh
                q.append(act_ref[pl.ds(row_start(c), chunk), o0:o0 + dh])
                k.append(act_ref[pl.ds(row_start(c), chunk), o0 + dh:o0 + 2 * dh])
                log_f.append(act_ref[pl.ds(row_start(c), chunk), o0 + 3 * dh:o0 + 4 * dh])
        slots = range(len(q))

        b = []
        for a in slots:
            lf_hi = log_f[a].astype(BF16)
            lf_lo = (log_f[a] - lf_hi.astype(F32)).astype(BF16)
            b.append(_dot(tri_ref[...], lf_hi) + _dot(tri_ref[...], lf_lo))
            b_ref[a] = b[a]

        expo, acc = [], []
        for a in slots:
            expo.append(_hgrn_level_exponents(b[a], b_ref.at[a], sgn_ref, chunk))
            qd_ref[a] = (q[a] * jnp.exp2(b[a])).astype(BF16)
            b_last = b_ref[a, chunk - 1:chunk, :]
            kd_ref[a] = (k[a] * jnp.exp2(b_last - b[a])).astype(BF16)
            dec_ref[a] = jnp.exp2(b_last)
        n_grp = chunk // SUBLANES

        def shifted(x, dist):
            return pltpu.roll(x.reshape(n_grp, SUBLANES, dh), dist, axis=1).reshape(chunk, dh)

        for a in slots:
            acc.append(jnp.where(band_ref[...] == 0, jnp.sum(q[a] * k[a], axis=-1, keepdims=True), 0.0))
        for dist in range(1, A_BAND):
            for a in slots:
                decay = jnp.exp2(b[a] - shifted(b[a], dist))
                c = jnp.sum(q[a] * shifted(k[a], dist) * decay, axis=-1, keepdims=True)
                acc[a] = jnp.where(band_ref[...] == dist, c, acc[a])
        for m in levels:
            if m >= SUBLANES:
                break
            is_query_row = sgn_ref[...] > 0.0
            for a in slots:
                w = (jnp.where(is_query_row, q[a], k[a]) * jnp.exp2(expo[a][m])).astype(BF16)
                acc[a] = jnp.where(lvl_ref[...] >= m, _dot_nt(w, w), acc[a])
        n_grp = chunk // SUBLANES
        acc = [[acc_a[SUBLANES * i:SUBLANES * (i + 1), :] for i in range(n_grp)] for acc_a in acc]
        for m in levels:
            if m < SUBLANES:
                continue
            is_query = [(SUBLANES * i) % (2 * m) >= m for i in range(n_grp)]
            for a in slots:
                roles = jnp.concatenate([(q[a] if is_query[i] else k[a])[SUBLANES * i:SUBLANES * (i + 1), :]
                                         for i in range(n_grp)], axis=0)
                w = roles * jnp.exp2(expo[a][m])
                q_m = jnp.concatenate([w[SUBLANES * i:SUBLANES * (i + 1), :]
                                       for i in range(n_grp) if is_query[i]], axis=0)
                a_m = _dot_nt(q_m.astype(BF16), w.astype(BF16))
                for n, i in enumerate([i for i in range(n_grp) if is_query[i]]):
                    rows = slice(SUBLANES * i, SUBLANES * (i + 1))
                    acc[a][i] = jnp.where(lvl_ref[rows, :] >= m,
                                          a_m[SUBLANES * n:SUBLANES * (n + 1), :], acc[a][i])
        for a in slots:
            a16_ref[a] = jnp.concatenate(acc[a], axis=0).astype(BF16)

    def back(chunks):
        for n, c in enumerate(chunks):
            r0 = row_start(c)
            for a in heads:
                o0 = a * 4 * dh
                slot = n * hps + a
                v16 = v16_ref[pl.ds(r0, chunk), a * dh:(a + 1) * dh]
                st = st_ref[a]
                oa = _dot_nt(qd_ref[slot], st.astype(BF16)) + _dot(a16_ref[slot], v16)
                st_ref[a] = st * dec_ref[slot] + _dot_tn(v16, kd_ref[slot])
                y = _rms_rows(oa, onw) * act_ref[pl.ds(r0, chunk), o0 + 2 * dh:o0 + 3 * dh]
                y_ref[0, pl.ds(r0, chunk), a * dh:(a + 1) * dh] = y.astype(BF16)

    n_iter = seq // (chunk * cpi)
    front([n for n in range(cpi)])

    def body(i, carry):
        back([(i - 1) * cpi + n for n in range(cpi)])
        front([i * cpi + n for n in range(cpi)])
        return carry

    lax.fori_loop(1, n_iter, body, 0)
    back([(n_iter - 1) * cpi + n for n in range(cpi)])


def _hgrn_mixer(x, norm_w_row, w_in, lb_raw, o_norm_w_row, layer_idx):
    bsz, seq, d = x.shape
    dh = A_HEAD_DIM
    hps = A_HEADS_PER_STEP
    n_steps = d // (dh * hps)
    chunk = A_CHUNK
    tri = jnp.asarray(np.tril(np.ones((chunk, chunk), np.float32)), BF16)
    cpi = A_CHUNKS_PER_ITER
    n_slots = hps * cpi
    assert seq % (chunk * cpi) == 0
    kern = functools.partial(_hgrn_kernel, layer_idx=layer_idx, seq=seq, chunk=chunk, hps=hps, cpi=cpi)

    def w_spec(part):
        return pl.BlockSpec((d, hps * dh), lambda b, h, part=part: (0, part * n_steps + h))

    return pl.pallas_call(
        kern,
        grid=(bsz, n_steps),
        in_specs=[pl.BlockSpec((1, seq, d), lambda b, h: (b, 0, 0)),
                  pl.BlockSpec((1, d), lambda b, h: (0, 0)),
                  w_spec(0), w_spec(1), w_spec(2), w_spec(3),
                  pl.BlockSpec((lb_raw.shape[0], hps * dh), lambda b, h: (0, h)),
                  pl.BlockSpec((1, dh), lambda b, h: (0, 0)),
                  pl.BlockSpec((chunk, chunk), lambda b, h: (0, 0))],
        out_specs=pl.BlockSpec((1, seq, hps * dh), lambda b, h: (b, 0, h)),
        out_shape=jax.ShapeDtypeStruct((bsz, seq, d), BF16),
        scratch_shapes=[pltpu.VMEM((seq, d), BF16),
                        pltpu.VMEM((seq, hps * 4 * dh), F32),
                        pltpu.VMEM((seq, hps * dh), BF16),
                        pltpu.VMEM((n_slots, chunk, dh), F32),
                        pltpu.VMEM((hps, dh, dh), F32),
                        pltpu.VMEM((chunk, chunk), jnp.int32),
                        pltpu.VMEM((chunk, chunk), jnp.int32),
                        pltpu.VMEM((chunk, dh), F32),
                        pltpu.VMEM((n_slots, chunk, dh), BF16),
                        pltpu.VMEM((n_slots, chunk, dh), BF16),
                        pltpu.VMEM((n_slots, chunk, chunk), BF16),
                        pltpu.VMEM((n_slots, 1, dh), F32)],
        compiler_params=_params(("arbitrary", "arbitrary")),
        name="hgrn_mixer",
    )(x, norm_w_row, w_in, w_in, w_in, w_in, lb_raw, o_norm_w_row, tri)


def _bproj_kernel(x_ref, nw_ref, w_ref, wkt_ref, hw_ref, o_ref, h_ref, hn_ref,
                  *, seq, dils):
    j = pl.program_id(1)
    n_qkv_tiles = 3 * len(dils)

    @pl.when(j == 0)
    def _():
        d = x_ref.shape[-1]
        n_planes = d // LANES
        rb = 256
        for i in range(seq // rb):
            hn = _rms_rows(x_ref[0, i * rb:(i + 1) * rb, :], nw_ref[...])
            for c in range(n_planes):
                hn_ref[c, i * rb:(i + 1) * rb, :] = hn[:, c * LANES:(c + 1) * LANES]
        for g, dil in enumerate(dils):
            sub = seq // dil
            for r in range(dil):
                for c0 in range(0, sub, rb):
                    rows = min(rb, sub - c0)
                    if dil == 1:
                        idx = pl.ds(c0, rows)
                    else:
                        idx = pl.ds(r + c0 * dil, rows, stride=dil)
                    for c in range(n_planes):
                        h_ref[g, r * sub + c0:r * sub + c0 + rows, c * LANES:(c + 1) * LANES] = (
                            hn_ref[c, idx, :].astype(BF16))

    grp = jnp.where(j < n_qkv_tiles, j // 3, 0)
    kind = jnp.where(j < n_qkv_tiles, j % 3, 3)
    pb = 512

    def tile(i):
        return _dot(h_ref[grp, i * pb:(i + 1) * pb, :], w_ref[...])

    def store(i, val):
        for hp in range(val.shape[-1] // LANES):
            o_ref[hp, i * pb:(i + 1) * pb, :] = val[:, hp * LANES:(hp + 1) * LANES].astype(BF16)

    @pl.when(kind == 1)
    def _():
        d = x_ref.shape[-1]
        hd = B_HEAD_DIM
        for i in range(seq // pb):
            yt = _dot_nt(wkt_ref[...], h_ref[grp, i * pb:(i + 1) * pb, :])
            for h in range(d // hd):
                yh = yt[h * hd:(h + 1) * hd, :]
                r = lax.rsqrt(jnp.mean(yh * yh, axis=0, keepdims=True) + NORM_EPS)
                yh = (yh * r).astype(BF16)
                hp, half = divmod(h * hd, LANES)
                for kb in range(pb // LANES):
                    row0 = (i * (pb // LANES) + kb) * LANES + half
                    o_ref[hp, row0:row0 + hd, :] = yh[:, kb * LANES:(kb + 1) * LANES]

    @pl.when(kind == 0)
    def _():
        first_head = lax.broadcasted_iota(jnp.int32, (pb, LANES), 1) < B_HEAD_DIM
        for i in range(seq // pb):
            y = tile(i)
            for hp in range(y.shape[-1] // LANES):
                cols = slice(hp * LANES, (hp + 1) * LANES)
                yc = y[:, cols]
                sq = yc * yc
                ss_a = jnp.sum(jnp.where(first_head, sq, 0.0), axis=-1, keepdims=True)
                ss_b = jnp.sum(jnp.where(first_head, 0.0, sq), axis=-1, keepdims=True)
                r_a = lax.rsqrt(ss_a * (1.0 / B_HEAD_DIM) + NORM_EPS)
                r_b = lax.rsqrt(ss_b * (1.0 / B_HEAD_DIM) + NORM_EPS)
                o_ref[hp, i * pb:(i + 1) * pb, :] = (
                    yc * jnp.where(first_head, r_a, r_b) * hw_ref[0, :, cols]).astype(BF16)

    @pl.when(kind == 2)
    def _():
        for i in range(seq // pb):
            store(i, tile(i))

    @pl.when(kind == 3)
    def _():
        for i in range(seq // pb):
            store(i, _silu(tile(i)))


def _transpose_tiles_kernel(w_ref, o_ref):
    n = w_ref.shape[0] // LANES
    for bi in range(n):
        for bj in range(n):
            o_ref[bj * LANES:(bj + 1) * LANES, bi * LANES:(bi + 1) * LANES] = (
                w_ref[bi * LANES:(bi + 1) * LANES, bj * LANES:(bj + 1) * LANES].T)


def _transposed_k_weights(w_bf16, d, n_grp):
    return pl.pallas_call(
        _transpose_tiles_kernel,
        grid=(n_grp,),
        in_specs=[pl.BlockSpec((d, d), lambda g: (0, 3 * g + 1))],
        out_specs=pl.BlockSpec((None, d, d), lambda g: (g, 0, 0)),
        out_shape=jax.ShapeDtypeStruct((n_grp, d, d), BF16),
        compiler_params=_params(("arbitrary",)),
        name="transpose_k_weights",
    )(w_bf16)


def _attn_in_proj(x, norm_w_row, w_in, head_w):
    bsz, seq, d = x.shape
    n_out = w_in.shape[1]
    tn = d
    dils = tuple(dil for _, dil in DILATED_GROUPS)
    n_grp = len(dils)
    w_bf16 = w_in.astype(BF16)
    w_k_t = _transposed_k_weights(w_bf16, d, n_grp)
    kern = functools.partial(_bproj_kernel, seq=seq, dils=dils)
    return pl.pallas_call(
        kern,
        grid=(bsz, n_out // tn),
        in_specs=[pl.BlockSpec((1, seq, d), lambda b, j: (b, 0, 0)),
                  pl.BlockSpec((1, d), lambda b, j: (0, 0)),
                  pl.BlockSpec((d, tn), lambda b, j: (0, j)),
                  pl.BlockSpec((None, tn, d), lambda b, j: (jnp.minimum(j // 3, n_grp - 1), 0, 0)),
                  pl.BlockSpec((1, 1, tn), lambda b, j: (j, 0, 0))],
        out_specs=pl.BlockSpec((None, None, tn // LANES, seq, LANES), lambda b, j: (b, j, 0, 0, 0)),
        out_shape=jax.ShapeDtypeStruct((bsz, n_out // tn, tn // LANES, seq, LANES), BF16),
        scratch_shapes=[pltpu.VMEM((len(dils), seq, d), BF16),
                        pltpu.VMEM((d // LANES, seq, LANES), F32)],
        compiler_params=_params(("arbitrary", "arbitrary")),
        name="attn_in_proj",
    )(x, norm_w_row, w_bf16, w_k_t, head_w)


def _attn_kernel(slopes_ref, q0, k0, v0, q1, k1, v1, q2, k2, v2, g_ref, o_ref,
                 num_ref, den_ref, max_ref, bias_ref, s_ref, p_ref, *, seq, dils, unrolls):
    hp = pl.program_id(1)
    blk = B_BLOCK
    hd = B_HEAD_DIM
    n_heads = (LANES // hd) * pl.num_programs(1)
    qkv = ((q0, k0, v0), (q1, k1, v1), (q2, k2, v2))
    head_a = lax.broadcasted_iota(jnp.int32, (blk, LANES), 1) < hd


    def token_rows(g, tok0):
        return pl.ds(tok0, blk) if dils[g] == 1 else pl.ds(tok0, blk, stride=dils[g])

    def qk_phase(g, specs, has_prev, slot):
        q_ref, kt_ref, _ = qkv[g]
        nk = 2 * blk if has_prev else blk
        for w, (row0, _) in enumerate(specs):
            q2d = q_ref[pl.ds(row0, blk), :]
            zero = jnp.zeros_like(q2d)
            qq = jnp.concatenate([jnp.where(head_a, q2d, zero), jnp.where(head_a, zero, q2d)], axis=0)
            kt = kt_ref[pl.ds(row0, blk), :]
            if has_prev:
                kt = jnp.concatenate([kt_ref[pl.ds(row0 - blk, blk), :], kt], axis=1)
            s_ref[slot, w, :, 0:nk] = _dot(qq, kt)

    def softmax_phase(g, specs, has_prev, slot):
        nk = 2 * blk if has_prev else blk
        off = 0 if has_prev else blk
        for w, (_, tok0) in enumerate(specs):
            ms, ls = [], []
            for a in range(2):
                sa = s_ref[slot, w, a * blk:(a + 1) * blk, 0:nk] + bias_ref[a, :, off:off + nk]
                m = jnp.max(sa, axis=-1, keepdims=True)
                e = jnp.exp2(sa - m)
                p_ref[slot, w, a * blk:(a + 1) * blk, 0:nk] = e.astype(BF16)
                ms.append(m)
                ls.append(jnp.sum(e, axis=-1, keepdims=True))
            idx = token_rows(g, tok0)
            max_ref[g, idx, :] = jnp.where(head_a, ms[0], ms[1])
            den_ref[g, idx, :] = jnp.where(head_a, ls[0], ls[1])

    def pv_phase(g, specs, has_prev, slot):
        v_ref = qkv[g][2]
        nk = 2 * blk if has_prev else blk
        for w, (row0, tok0) in enumerate(specs):
            k_lo = row0 - blk if has_prev else row0
            o = _dot(p_ref[slot, w, :, 0:nk], v_ref[pl.ds(k_lo, nk), :])
            num_ref[g, token_rows(g, tok0), :] = jnp.where(head_a, o[0:blk, :], o[blk:2 * blk, :])

    def pipelined(n_iter, make_specs, g, has_prev):
        def step(t):
            static = isinstance(t, int)
            if (not static) or 0 <= t - 2 < n_iter:
                pv_phase(g, make_specs(t - 2), has_prev, t % 2)
            if (not static) or 0 <= t - 1 < n_iter:
                softmax_phase(g, make_specs(t - 1), has_prev, (t - 1) % 2)
            if (not static) or 0 <= t < n_iter:
                qk_phase(g, make_specs(t), has_prev, t % 2)

        for t in range(2):
            step(t)
        if n_iter > 2:
            def body(t, carry):
                step(t)
                return carry

            lax.fori_loop(2, n_iter, body, 0)
        for t in range(max(n_iter, 2), n_iter + 2):
            step(t)

    qi = lax.broadcasted_iota(jnp.int32, (blk, 2 * blk), 0)
    kj = lax.broadcasted_iota(jnp.int32, (blk, 2 * blk), 1)
    dist = blk + qi - kj
    valid = (dist >= 0) & (dist <= blk)

    for g, dil in enumerate(dils):
        sub = seq // dil
        n_blk = sub // blk
        bias_base = jnp.where(valid, -(dist * dil).astype(F32), NEG_BIG)
        for a in range(2):
            slope = slopes_ref[g * n_heads + 2 * hp + a]
            bias_ref[a] = (slope * LOG2E) * bias_base

        u1, u2 = unrolls[g]

        def aligned(row):
            return row if isinstance(row, int) else pl.multiple_of(row, blk)

        def first_specs(i, sub=sub, u1=u1):
            return [(aligned((i * u1 + w) * sub), i * u1 + w) for w in range(u1)]

        def rest_specs(i, dil=dil, sub=sub, n_blk=n_blk, u2=u2):
            specs = []
            for w in range(u2):
                idx = i * u2 + w
                r = idx // (n_blk - 1)
                b = idx % (n_blk - 1) + 1
                specs.append((aligned(r * sub + b * blk), b * blk * dil + r))
            return specs

        pipelined(dil // u1, first_specs, g, False)
        n_rest = dil * (n_blk - 1)
        if n_rest:
            pipelined(n_rest // u2, rest_specs, g, True)

    rb = 256
    n_g = len(dils)

    def merge(i, carry):
        r0 = pl.multiple_of(i * rb, rb)
        ms = [max_ref[g, pl.ds(r0, rb), :] for g in range(n_g)]
        mx = ms[0]
        for g in range(1, n_g):
            mx = jnp.maximum(mx, ms[g])
        num = jnp.zeros((rb, LANES), F32)
        den = jnp.zeros((rb, LANES), F32)
        for g in range(n_g):
            w = jnp.exp2(ms[g] - mx)
            num = num + w * num_ref[g, pl.ds(r0, rb), :]
            den = den + w * den_ref[g, pl.ds(r0, rb), :]
        gate = g_ref[pl.ds(r0, rb), :].astype(F32)
        o_ref[pl.ds(r0, rb), :] = (num / den * gate).astype(BF16)
        return carry

    lax.fori_loop(0, seq // rb, merge, 0)


def _attention(proj, slopes):
    bsz, n_tiles, n_pairs, seq, _ = proj.shape
    assert n_tiles == 3 * B_N_GROUPS + 1
    dils = tuple(dil for _, dil in DILATED_GROUPS)
    unrolls = ((1, 5), (4, 6), (4, 1))
    for (u1, u2), dil in zip(unrolls, dils):
        n_rest = dil * (seq // dil // B_BLOCK - 1)
        assert dil % u1 == 0 and n_rest % u2 == 0

    def col_spec(tile):
        return pl.BlockSpec((None, None, None, seq, LANES), lambda b, h, s, tile=tile: (b, tile, h, 0, 0))

    kern = functools.partial(_attn_kernel, seq=seq, dils=dils, unrolls=unrolls)
    grid_spec = pltpu.PrefetchScalarGridSpec(
        num_scalar_prefetch=1,
        grid=(bsz, n_pairs),
        in_specs=[col_spec(t) for t in range(n_tiles)],
        out_specs=pl.BlockSpec((None, None, seq, LANES), lambda b, h, s: (b, h, 0, 0)),
        scratch_shapes=[pltpu.VMEM((B_N_GROUPS, seq, LANES), F32)] * 3
        + [pltpu.VMEM((2, B_BLOCK, 2 * B_BLOCK), F32),
           pltpu.VMEM((2, max(max(u) for u in unrolls), 2 * B_BLOCK, 2 * B_BLOCK), F32),
           pltpu.VMEM((2, max(max(u) for u in unrolls), 2 * B_BLOCK, 2 * B_BLOCK), BF16)],
    )
    return pl.pallas_call(
        kern,
        grid_spec=grid_spec,
        out_shape=jax.ShapeDtypeStruct((bsz, n_pairs, seq, LANES), BF16),
        compiler_params=_params(("arbitrary", "arbitrary")),
        name="dilated_attention",
    )(slopes, *([proj] * n_tiles))


def _gmlp_kernel(x_ref, yprev_ref, wprev_ref, nw_ref, w_ref, vw_ref, vb_ref, ws_ref, bias_ref, wout_ref, o_ref,
                 x1_ref, h_ref, u_ref, v_ref, g_ref, y_ref, *, tm):
    d = x_ref.shape[-1]
    cg = d // C_GROUPS
    rb = 256
    for i in range(tm // rb):
        rows = slice(i * rb, (i + 1) * rb)
        y_prev = jnp.concatenate([yprev_ref[hp, rows, :] for hp in range(yprev_ref.shape[0])], axis=1)
        x1_ref[rows, :] = x_ref[0, rows, :] + _dot(y_prev, wprev_ref[...])
        h_ref[rows, :] = _rms_rows(x1_ref[rows, :], nw_ref[...]).astype(BF16)
    for i in range(tm // rb):
        rows = slice(i * rb, (i + 1) * rb)
        hh = h_ref[rows, :]
        u_ref[rows, :] = _gelu_tanh(_dot(hh, w_ref[:, 0:d]))
        vv = _gelu_tanh(_dot(hh, w_ref[:, d:2 * d]))
        mu = jnp.mean(vv, axis=-1, keepdims=True)
        vc = vv - mu
        var = jnp.mean(vc * vc, axis=-1, keepdims=True)
        v_ref[rows, :] = (vc * lax.rsqrt(var + NORM_EPS) * vw_ref[...] + vb_ref[...]).astype(BF16)
        g_ref[rows, :] = _silu(_dot(hh, w_ref[:, 2 * d:3 * d]))
    ti = lax.broadcasted_iota(jnp.int32, (C_CHUNK, C_CHUNK), 0)
    si = lax.broadcasted_iota(jnp.int32, (C_CHUNK, C_CHUNK), 1)
    n_ch = tm // C_CHUNK
    for grp in range(C_GROUPS):
        cols = slice(grp * cg, (grp + 1) * cg)
        wc = jnp.where(si <= ti, ws_ref[grp], 0.0).astype(BF16)
        rhs = jnp.concatenate([v_ref[n * C_CHUNK:(n + 1) * C_CHUNK, cols] for n in range(n_ch)], axis=1)
        s = _dot(wc, rhs)
        bias = bias_ref[:, cols]
        for n in range(n_ch):
            rows = slice(n * C_CHUNK, (n + 1) * C_CHUNK)
            sn = s[:, n * cg:(n + 1) * cg] + bias
            y_ref[rows, cols] = (u_ref[rows, cols] * sn * g_ref[rows, cols]).astype(BF16)
    for i in range(tm // rb):
        rows = slice(i * rb, (i + 1) * rb)
        o_ref[0, rows, :] = x1_ref[rows, :] + _dot(y_ref[rows, :], wout_ref[...])


def _gmlp_mixer(x, y_prev_pairs, w_prev_bf16, norm_w_row, w_bf16, vw_row, vb_row, w_s, bias_full, w_out_bf16,
                tm=512):
    bsz, seq, d = x.shape
    n_pairs, lanes = y_prev_pairs.shape[1], y_prev_pairs.shape[3]
    kern = functools.partial(_gmlp_kernel, tm=tm)
    return pl.pallas_call(
        kern,
        grid=(bsz, seq // tm),
        in_specs=[pl.BlockSpec((1, tm, d), lambda b, i: (b, i, 0)),
                  pl.BlockSpec((None, n_pairs, tm, lanes), lambda b, i: (b, 0, i, 0)),
                  pl.BlockSpec((d, d), lambda b, i: (0, 0)),
                  pl.BlockSpec((1, d), lambda b, i: (0, 0)),
                  pl.BlockSpec((d, 3 * d), lambda b, i: (0, 0)),
                  pl.BlockSpec((1, d), lambda b, i: (0, 0)),
                  pl.BlockSpec((1, d), lambda b, i: (0, 0)),
                  pl.BlockSpec(w_s.shape, lambda b, i: (0, 0, 0)),
                  pl.BlockSpec((C_CHUNK, d), lambda b, i: (0, 0)),
                  pl.BlockSpec((d, d), lambda b, i: (0, 0))],
        out_specs=pl.BlockSpec((1, tm, d), lambda b, i: (b, i, 0)),
        out_shape=jax.ShapeDtypeStruct((bsz, seq, d), F32),
        scratch_shapes=[pltpu.VMEM((tm, d), F32),
                        pltpu.VMEM((tm, d), BF16),
                        pltpu.VMEM((tm, d), F32),
                        pltpu.VMEM((tm, d), BF16),
                        pltpu.VMEM((tm, d), F32),
                        pltpu.VMEM((tm, d), BF16)],
        compiler_params=_params(("arbitrary", "arbitrary")),
        name="gmlp_mixer",
    )(x, y_prev_pairs, w_prev_bf16, norm_w_row, w_bf16, vw_row, vb_row, w_s, bias_full, w_out_bf16)


def kernel(x, norm_w, a_w_in, a_lower_bounds, a_o_norm_w, a_w_out, b_w_in, b_q_norm_w, b_k_norm_w, b_w_out,
           c_w_in, c_v_norm_w, c_v_norm_b, c_w_s, c_b_s, c_w_out):
    bsz, seq, d = x.shape
    depth = norm_w.shape[0]
    n_mixers = 3
    assert seq == DILATED_GROUPS[-1][0] and seq % A_CHUNK == 0 and d % LANES == 0

    def residual(y, w_out, xin):
        out = _out_proj(y.reshape(bsz * seq, d), w_out.astype(BF16), xin.reshape(bsz * seq, d))
        return out.reshape(bsz, seq, d)

    pending = None
    for layer in range(depth):
        kind, idx = layer % n_mixers, layer // n_mixers
        nw = norm_w[layer][None, :]
        if kind == 0:
            y = _hgrn_mixer(x, nw, a_w_in[idx].astype(BF16), a_lower_bounds, a_o_norm_w[idx][None, :], idx)
            x = residual(y, a_w_out[idx], x)
        elif kind == 1:
            n_heads = d // B_HEAD_DIM
            n_total = B_N_GROUPS * n_heads
            slopes = jnp.exp2(-ALIBI_MAX_EXP * jnp.arange(1, n_total + 1, dtype=F32) / n_total)
            rows = []
            for g in range(B_N_GROUPS):
                rows.append(jnp.tile(b_q_norm_w[idx, g] * b_k_norm_w[idx, g], n_heads)
                            * (B_HEAD_DIM ** -0.5 * LOG2E))
                rows.append(jnp.ones((d,), F32))
                rows.append(jnp.ones((d,), F32))
            rows.append(jnp.ones((d,), F32))
            head_w = jnp.stack(rows)[:, None, :]
            proj = _attn_in_proj(x, nw, b_w_in[idx], head_w)
            pending = (_attention(proj, slopes), b_w_out[idx].astype(BF16))
            if layer == depth - 1:
                x = _out_proj_pairs(*pending, x)
        else:
            bias_full = jnp.repeat(c_b_s[idx].T, d // C_GROUPS, axis=1)
            x = _gmlp_mixer(x, *pending, nw, c_w_in[idx].astype(BF16), c_v_norm_w[idx][None, :],
                            c_v_norm_b[idx][None, :], c_w_s[idx], bias_full, c_w_out[idx].astype(BF16))
    return x
```

```python
import functools

import numpy as np
import jax
import jax.numpy as jnp
from jax import lax
from jax.experimental import pallas as pl
from jax.experimental.pallas import tpu as pltpu

F32 = jnp.float32
BF16 = jnp.bfloat16

NORM_EPS = 1e-6
NEG_BIG = -1e30
LB_FLOOR = 1e-30
LOG2E = float(np.log2(np.e))

LANES = 128
SUBLANES = 8
VMEM_LIMIT_BYTES = 60 * 1024 * 1024

A_HEAD_DIM = 128
A_CHUNK = 128
A_CHUNKS_PER_ITER = 2
A_HEADS_PER_STEP = 4
B_HEAD_DIM = 64
DILATED_GROUPS = ((128, 1), (512, 4), (2048, 16))
B_N_GROUPS = len(DILATED_GROUPS)
B_BLOCK = 128
B_GATHER_STRIDE = 4
ALIBI_MAX_EXP = 8.0
C_CHUNK = 128
C_GROUPS = 8


def _silu(x):
    hx = 0.5 * x
    return hx + hx * jnp.tanh(hx)


def _gelu_tanh(x):
    c = np.float32(np.sqrt(2.0 / np.pi))
    return 0.5 * x * (1.0 + jnp.tanh(c * (x + 0.044715 * (x * x * x))))


def _rms_rows(xf, w):
    ms = jnp.mean(xf * xf, axis=-1, keepdims=True)
    return xf * lax.rsqrt(ms + NORM_EPS) * w


def _dot(a, b):
    return jnp.dot(a, b, preferred_element_type=F32)


def _dot_nt(a, b):
    return lax.dot_general(a, b, (((1,), (1,)), ((), ())), preferred_element_type=F32)


def _dot_tn(a, b):
    return lax.dot_general(a, b, (((0,), (0,)), ((), ())), preferred_element_type=F32)


def _params(sem):
    return pltpu.CompilerParams(dimension_semantics=sem, vmem_limit_bytes=VMEM_LIMIT_BYTES)


def _out_proj_kernel(y_ref, w_ref, x_ref, o_ref):
    o_ref[...] = x_ref[...] + _dot(y_ref[...], w_ref[...])


def _out_proj(y2d, w_bf16, x2d, tm=2048):
    m, k = y2d.shape
    n = w_bf16.shape[1]
    return pl.pallas_call(
        _out_proj_kernel,
        grid=(m // tm,),
        in_specs=[pl.BlockSpec((tm, k), lambda i: (i, 0)),
                  pl.BlockSpec((k, n), lambda i: (0, 0)),
                  pl.BlockSpec((tm, n), lambda i: (i, 0))],
        out_specs=pl.BlockSpec((tm, n), lambda i: (i, 0)),
        out_shape=jax.ShapeDtypeStruct((m, n), F32),
        compiler_params=_params(("arbitrary",)),
        name="out_proj",
    )(y2d, w_bf16, x2d)


def _out_proj_pairs_kernel(y_ref, w_ref, x_ref, o_ref):
    y = jnp.concatenate([y_ref[hp] for hp in range(y_ref.shape[0])], axis=1)
    o_ref[...] = x_ref[...] + _dot(y, w_ref[...])


def _out_proj_pairs(y_pairs, w_bf16, x, tm=1024):
    bsz, n_pairs, seq, lanes = y_pairs.shape
    d = n_pairs * lanes
    n = w_bf16.shape[1]
    return pl.pallas_call(
        _out_proj_pairs_kernel,
        grid=(bsz, seq // tm),
        in_specs=[pl.BlockSpec((None, n_pairs, tm, lanes), lambda b, i: (b, 0, i, 0)),
                  pl.BlockSpec((d, n), lambda b, i: (0, 0)),
                  pl.BlockSpec((None, tm, n), lambda b, i: (b, i, 0))],
        out_specs=pl.BlockSpec((None, tm, n), lambda b, i: (b, i, 0)),
        out_shape=jax.ShapeDtypeStruct((bsz, seq, n), F32),
        compiler_params=_params(("arbitrary", "arbitrary")),
        name="out_proj_pairs",
    )(y_pairs, w_bf16, x)


A_BAND = 4


def _hgrn_levels(chunk):
    return [1 << j for j in range(int(np.log2(chunk))) if (2 << j) > A_BAND]


def _hgrn_level_exponents(b, b_view, sgn_ref, chunk):
    out = {}
    n_vregs = chunk // SUBLANES
    width = b.shape[-1]

    def row(r, n):
        one_group = jnp.broadcast_to(b_view[pl.ds(r, 1), :], (SUBLANES, width))
        return jnp.concatenate([one_group] * (n // SUBLANES), axis=0)

    for m in _hgrn_levels(chunk):
        if m >= SUBLANES:
            parts = []
            for j in range(chunk // (2 * m)):
                base = j * 2 * m
                ref_row = row(base + m - 1, m)
                parts.append(ref_row - b[base:base + m, :])
                parts.append(b[base + m:base + 2 * m, :] - ref_row)
            out[m] = jnp.concatenate(parts, axis=0)
        else:
            assert m == 4
            ref = jnp.concatenate([row(SUBLANES * i + 3, SUBLANES) for i in range(n_vregs)], axis=0)
            out[m] = (b - ref) * sgn_ref[...]
    return out


def _hgrn_kernel(x_ref, nw_ref, wq_ref, wf_ref, wi_ref, wg_ref, lbraw_ref, onw_ref, tri_ref, y_ref,
                 h_ref, act_ref, v16_ref, b_ref, st_ref, lvl_ref, band_ref, sgn_ref, qd_ref, kd_ref, a16_ref,
                 dec_ref,
                 *, layer_idx, seq, chunk, hps, cpi):
    step = pl.program_id(1)
    dh = A_HEAD_DIM
    rb = 256

    @pl.when(step == 0)
    def _():
        for i in range(seq // rb):
            xs = x_ref[0, i * rb:(i + 1) * rb, :]
            h_ref[i * rb:(i + 1) * rb, :] = _rms_rows(xs, nw_ref[...]).astype(BF16)

    raw = lbraw_ref[...]
    n_layers = raw.shape[0]
    mx = raw[0:1, :]
    for i in range(1, n_layers):
        mx = jnp.maximum(mx, raw[i:i + 1, :])
    es = [jnp.exp(raw[i:i + 1, :] - mx) for i in range(n_layers)]
    z = es[0]
    for i in range(1, n_layers):
        z = z + es[i]
    soft = [e / z for e in es]
    csum = soft[0]
    for i in range(1, layer_idx + 1):
        csum = csum + soft[i]
    lb = csum - soft[0]
    log_lb = jnp.log(jnp.maximum(lb, LB_FLOOR))
    log1m_lb = jnp.log1p(-lb)
    one_m_lb = 1.0 - lb
    heads = range(hps)

    pb = 512
    for i in range(seq // pb):
        rows = slice(i * pb, (i + 1) * pb)
        hh = h_ref[rows, :]
        res_q, res_f, res_i, res_g = (_dot(hh, w[...]) for w in (wq_ref, wf_ref, wi_ref, wg_ref))
        v16_ref[rows, :] = res_i.astype(BF16)
        for a in heads:
            o0 = a * 4 * dh
            lbs = slice(a * dh, (a + 1) * dh)
            pf = res_f[:, lbs]
            log_sig = jnp.minimum(pf, 0.0) - jnp.log(1.0 + jnp.exp(-jnp.abs(pf)))
            bb = log1m_lb[:, lbs] + log_sig
            ll = log_lb[:, lbs]
            log_f = jnp.maximum(ll, bb) + jnp.log(1.0 + jnp.exp(-jnp.abs(ll - bb)))
            act_ref[rows, o0:o0 + dh] = _silu(res_q[:, lbs])
            act_ref[rows, o0 + dh:o0 + 2 * dh] = one_m_lb[:, lbs] * jnp.exp(log_sig - pf)
            act_ref[rows, o0 + 2 * dh:o0 + 3 * dh] = _silu(res_g[:, lbs])
            act_ref[rows, o0 + 3 * dh:o0 + 4 * dh] = log_f * LOG2E

    ti = lax.broadcasted_iota(jnp.int32, (chunk, chunk), 0)
    si = lax.broadcasted_iota(jnp.int32, (chunk, chunk), 1)
    lvl_ref[...] = jnp.where(si < ti, ti ^ si, jnp.where(si == ti, 0, -1))
    band_ref[...] = jnp.where((si <= ti) & ((ti ^ si) < A_BAND), ti - si, -1)
    tr = lax.broadcasted_iota(jnp.int32, (chunk, dh), 0)
    sgn_ref[...] = jnp.where((tr & 4) != 0, 1.0, -1.0)
    st_ref[...] = jnp.zeros_like(st_ref)
    onw = onw_ref[...]
    levels = _hgrn_levels(chunk)

    def row_start(c):
        return c * chunk if isinstance(c, int) else pl.multiple_of(c * chunk, chunk)

    def front(chunks):
        q, k, log_f = [], [], []
        for c in chunks:
            for a in heads:
                o0 = a * 4 * dh
                q.append(act_ref[pl.ds(row_start(c), chunk), o0:o0 + dh])
                k.append(act_ref[pl.ds(row_start(c), chunk), o0 + dh:o0 + 2 * dh])
                log_f.append(act_ref[pl.ds(row_start(c), chunk), o0 + 3 * dh:o0 + 4 * dh])
        slots = range(len(q))

        b = []
        for a in slots:
            lf_hi = log_f[a].astype(BF16)
            lf_lo = (log_f[a] - lf_hi.astype(F32)).astype(BF16)
            b.append(_dot(tri_ref[...], lf_hi) + _dot(tri_ref[...], lf_lo))
            b_ref[a] = b[a]

        expo, acc = [], []
        for a in slots:
            expo.append(_hgrn_level_exponents(b[a], b_ref.at[a], sgn_ref, chunk))
            qd_ref[a] = (q[a] * jnp.exp2(b[a])).astype(BF16)
            b_last = b_ref[a, chunk - 1:chunk, :]
            kd_ref[a] = (k[a] * jnp.exp2(b_last - b[a])).astype(BF16)
            dec_ref[a] = jnp.exp2(b_last)
        n_grp = chunk // SUBLANES

        def shifted(x, dist):
            return pltpu.roll(x.reshape(n_grp, SUBLANES, dh), dist, axis=1).reshape(chunk, dh)

        for a in slots:
            acc.append(jnp.where(band_ref[...] == 0, jnp.sum(q[a] * k[a], axis=-1, keepdims=True), 0.0))
        for dist in range(1, A_BAND):
            for a in slots:
                decay = jnp.exp2(b[a] - shifted(b[a], dist))
                c = jnp.sum(q[a] * shifted(k[a], dist) * decay, axis=-1, keepdims=True)
                acc[a] = jnp.where(band_ref[...] == dist, c, acc[a])
        for m in levels:
            if m >= SUBLANES:
                break
            is_query_row = sgn_ref[...] > 0.0
            for a in slots:
                w = (jnp.where(is_query_row, q[a], k[a]) * jnp.exp2(expo[a][m])).astype(BF16)
                acc[a] = jnp.where(lvl_ref[...] >= m, _dot_nt(w, w), acc[a])
        n_grp = chunk // SUBLANES
        acc = [[acc_a[SUBLANES * i:SUBLANES * (i + 1), :] for i in range(n_grp)] for acc_a in acc]
        for m in levels:
            if m < SUBLANES:
                continue
            is_query = [(SUBLANES * i) % (2 * m) >= m for i in range(n_grp)]
            for a in slots:
                roles = jnp.concatenate([(q[a] if is_query[i] else k[a])[SUBLANES * i:SUBLANES * (i + 1), :]
                                         for i in range(n_grp)], axis=0)
                w = roles * jnp.exp2(expo[a][m])
                q_m = jnp.concatenate([w[SUBLANES * i:SUBLANES * (i + 1), :]
                                       for i in range(n_grp) if is_query[i]], axis=0)
                a_m = _dot_nt(q_m.astype(BF16), w.astype(BF16))
                for n, i in enumerate([i for i in range(n_grp) if is_query[i]]):
                    rows = slice(SUBLANES * i, SUBLANES * (i + 1))
                    acc[a][i] = jnp.where(lvl_ref[rows, :] >= m,
                                          a_m[SUBLANES * n:SUBLANES * (n + 1), :], acc[a][i])
        for a in slots:
            a16_ref[a] = jnp.concatenate(acc[a], axis=0).astype(BF16)

    def back(chunks):
        for n, c in enumerate(chunks):
            r0 = row_start(c)
            for a in heads:
                o0 = a * 4 * dh
                slot = n * hps + a
                v16 = v16_ref[pl.ds(r0, chunk), a * dh:(a + 1) * dh]
                st = st_ref[a]
                oa = _dot_nt(qd_ref[slot], st.astype(BF16)) + _dot(a16_ref[slot], v16)
                st_ref[a] = st * dec_ref[slot] + _dot_tn(v16, kd_ref[slot])
                y = _rms_rows(oa, onw) * act_ref[pl.ds(r0, chunk), o0 + 2 * dh:o0 + 3 * dh]
                y_ref[0, pl.ds(r0, chunk), a * dh:(a + 1) * dh] = y.astype(BF16)

    n_iter = seq // (chunk * cpi)
    front([n for n in range(cpi)])

    def body(i, carry):
        back([(i - 1) * cpi + n for n in range(cpi)])
        front([i * cpi + n for n in range(cpi)])
        return carry

    lax.fori_loop(1, n_iter, body, 0)
    back([(n_iter - 1) * cpi + n for n in range(cpi)])


def _hgrn_mixer(x, norm_w_row, w_in, lb_raw, o_norm_w_row, layer_idx):
    bsz, seq, d = x.shape
    dh = A_HEAD_DIM
    hps = A_HEADS_PER_STEP
    n_steps = d // (dh * hps)
    chunk = A_CHUNK
    tri = jnp.asarray(np.tril(np.ones((chunk, chunk), np.float32)), BF16)
    cpi = A_CHUNKS_PER_ITER
    n_slots = hps * cpi
    assert seq % (chunk * cpi) == 0
    kern = functools.partial(_hgrn_kernel, layer_idx=layer_idx, seq=seq, chunk=chunk, hps=hps, cpi=cpi)

    def w_spec(part):
        return pl.BlockSpec((d, hps * dh), lambda b, h, part=part: (0, part * n_steps + h))

    return pl.pallas_call(
        kern,
        grid=(bsz, n_steps),
        in_specs=[pl.BlockSpec((1, seq, d), lambda b, h: (b, 0, 0)),
                  pl.BlockSpec((1, d), lambda b, h: (0, 0)),
                  w_spec(0), w_spec(1), w_spec(2), w_spec(3),
                  pl.BlockSpec((lb_raw.shape[0], hps * dh), lambda b, h: (0, h)),
                  pl.BlockSpec((1, dh), lambda b, h: (0, 0)),
                  pl.BlockSpec((chunk, chunk), lambda b, h: (0, 0))],
        out_specs=pl.BlockSpec((1, seq, hps * dh), lambda b, h: (b, 0, h)),
        out_shape=jax.ShapeDtypeStruct((bsz, seq, d), BF16),
        scratch_shapes=[pltpu.VMEM((seq, d), BF16),
                        pltpu.VMEM((seq, hps * 4 * dh), F32),
                        pltpu.VMEM((seq, hps * dh), BF16),
                        pltpu.VMEM((n_slots, chunk, dh), F32),
                        pltpu.VMEM((hps, dh, dh), F32),
                        pltpu.VMEM((chunk, chunk), jnp.int32),
                        pltpu.VMEM((chunk, chunk), jnp.int32),
                        pltpu.VMEM((chunk, dh), F32),
                        pltpu.VMEM((n_slots, chunk, dh), BF16),
                        pltpu.VMEM((n_slots, chunk, dh), BF16),
                        pltpu.VMEM((n_slots, chunk, chunk), BF16),
                        pltpu.VMEM((n_slots, 1, dh), F32)],
        compiler_params=_params(("arbitrary", "arbitrary")),
        name="hgrn_mixer",
    )(x, norm_w_row, w_in, w_in, w_in, w_in, lb_raw, o_norm_w_row, tri)


def _residue_of_slot(dil, slot):
    if dil <= B_GATHER_STRIDE:
        return slot
    return slot // B_GATHER_STRIDE + B_GATHER_STRIDE * (slot % B_GATHER_STRIDE)


def _bproj_kernel(x_ref, nw_ref, w_ref, wkt_ref, hw_ref, o_ref, h_ref, hn_ref, tmp_ref,
                  *, seq, dils):
    j = pl.program_id(1)
    n_qkv_tiles = 3 * len(dils)

    @pl.when(j == 0)
    def _():
        d = x_ref.shape[-1]
        n_planes = d // LANES
        rb = 256
        for i in range(seq // rb):
            hn = _rms_rows(x_ref[0, i * rb:(i + 1) * rb, :], nw_ref[...])
            for c in range(n_planes):
                hn_ref[c, i * rb:(i + 1) * rb, :] = hn[:, c * LANES:(c + 1) * LANES]
        assert dils == (1, B_GATHER_STRIDE, B_GATHER_STRIDE ** 2)
        step = B_GATHER_STRIDE
        sub = seq // step
        for c in range(n_planes):
            cols = slice(c * LANES, (c + 1) * LANES)
            for c0 in range(0, seq, rb):
                h_ref[0, c0:c0 + rb, cols] = hn_ref[c, c0:c0 + rb, :].astype(BF16)
            for r in range(step):
                for c0 in range(0, sub, rb):
                    rows = hn_ref[c, pl.ds(r + c0 * step, rb, stride=step), :]
                    tmp_ref[r * sub + c0:r * sub + c0 + rb, :] = rows
                    h_ref[1, r * sub + c0:r * sub + c0 + rb, cols] = rows.astype(BF16)
            sub2 = sub // step
            for r in range(step):
                for r2 in range(step):
                    rows = tmp_ref[pl.ds(r * sub + r2, sub2, stride=step), :]
                    slot = r * step + r2
                    h_ref[2, slot * sub2:(slot + 1) * sub2, cols] = rows.astype(BF16)

    grp = jnp.where(j < n_qkv_tiles, j // 3, 0)
    kind = jnp.where(j < n_qkv_tiles, j % 3, 3)
    pb = 512

    def tile(i):
        return _dot(h_ref[grp, i * pb:(i + 1) * pb, :], w_ref[...])

    def store(i, val):
        for hp in range(val.shape[-1] // LANES):
            o_ref[hp, i * pb:(i + 1) * pb, :] = val[:, hp * LANES:(hp + 1) * LANES].astype(BF16)

    @pl.when(kind == 1)
    def _():
        d = x_ref.shape[-1]
        hd = B_HEAD_DIM
        for i in range(seq // pb):
            yt = _dot_nt(wkt_ref[...], h_ref[grp, i * pb:(i + 1) * pb, :])
            for h in range(d // hd):
                yh = yt[h * hd:(h + 1) * hd, :]
                r = lax.rsqrt(jnp.mean(yh * yh, axis=0, keepdims=True) + NORM_EPS)
                yh = (yh * r).astype(BF16)
                hp, half = divmod(h * hd, LANES)
                for kb in range(pb // LANES):
                    row0 = (i * (pb // LANES) + kb) * LANES + half
                    o_ref[hp, row0:row0 + hd, :] = yh[:, kb * LANES:(kb + 1) * LANES]

    @pl.when(kind == 0)
    def _():
        first_head = lax.broadcasted_iota(jnp.int32, (pb, LANES), 1) < B_HEAD_DIM
        for i in range(seq // pb):
            y = tile(i)
            for hp in range(y.shape[-1] // LANES):
                cols = slice(hp * LANES, (hp + 1) * LANES)
                yc = y[:, cols]
                sq = yc * yc
                ss_a = jnp.sum(jnp.where(first_head, sq, 0.0), axis=-1, keepdims=True)
                ss_b = jnp.sum(jnp.where(first_head, 0.0, sq), axis=-1, keepdims=True)
                r_a = lax.rsqrt(ss_a * (1.0 / B_HEAD_DIM) + NORM_EPS)
                r_b = lax.rsqrt(ss_b * (1.0 / B_HEAD_DIM) + NORM_EPS)
                o_ref[hp, i * pb:(i + 1) * pb, :] = (
                    yc * jnp.where(first_head, r_a, r_b) * hw_ref[0, :, cols]).astype(BF16)

    @pl.when(kind == 2)
    def _():
        for i in range(seq // pb):
            store(i, tile(i))

    @pl.when(kind == 3)
    def _():
        for i in range(seq // pb):
            store(i, _silu(tile(i)))


def _transpose_tiles_kernel(w_ref, o_ref):
    n = w_ref.shape[0] // LANES
    for bi in range(n):
        for bj in range(n):
            o_ref[bj * LANES:(bj + 1) * LANES, bi * LANES:(bi + 1) * LANES] = (
                w_ref[bi * LANES:(bi + 1) * LANES, bj * LANES:(bj + 1) * LANES].T)


def _transposed_k_weights(w_bf16, d, n_grp):
    return pl.pallas_call(
        _transpose_tiles_kernel,
        grid=(n_grp,),
        in_specs=[pl.BlockSpec((d, d), lambda g: (0, 3 * g + 1))],
        out_specs=pl.BlockSpec((None, d, d), lambda g: (g, 0, 0)),
        out_shape=jax.ShapeDtypeStruct((n_grp, d, d), BF16),
        compiler_params=_params(("arbitrary",)),
        name="transpose_k_weights",
    )(w_bf16)


def _attn_in_proj(x, norm_w_row, w_in, head_w):
    bsz, seq, d = x.shape
    n_out = w_in.shape[1]
    tn = d
    dils = tuple(dil for _, dil in DILATED_GROUPS)
    n_grp = len(dils)
    w_bf16 = w_in.astype(BF16)
    w_k_t = _transposed_k_weights(w_bf16, d, n_grp)
    kern = functools.partial(_bproj_kernel, seq=seq, dils=dils)
    return pl.pallas_call(
        kern,
        grid=(bsz, n_out // tn),
        in_specs=[pl.BlockSpec((1, seq, d), lambda b, j: (b, 0, 0)),
                  pl.BlockSpec((1, d), lambda b, j: (0, 0)),
                  pl.BlockSpec((d, tn), lambda b, j: (0, j)),
                  pl.BlockSpec((None, tn, d), lambda b, j: (jnp.minimum(j // 3, n_grp - 1), 0, 0)),
                  pl.BlockSpec((1, 1, tn), lambda b, j: (j, 0, 0))],
        out_specs=pl.BlockSpec((None, None, tn // LANES, seq, LANES), lambda b, j: (b, j, 0, 0, 0)),
        out_shape=jax.ShapeDtypeStruct((bsz, n_out // tn, tn // LANES, seq, LANES), BF16),
        scratch_shapes=[pltpu.VMEM((len(dils), seq, d), BF16),
                        pltpu.VMEM((d // LANES, seq, LANES), F32),
                        pltpu.VMEM((seq, LANES), F32)],
        compiler_params=_params(("arbitrary", "arbitrary")),
        name="attn_in_proj",
    )(x, norm_w_row, w_bf16, w_k_t, head_w)


def _attn_kernel(slopes_ref, q0, k0, v0, q1, k1, v1, q2, k2, v2, g_ref, o_ref,
                 num_ref, den_ref, max_ref, bias_ref, s_ref, p_ref, *, seq, dils, unrolls):
    hp = pl.program_id(1)
    blk = B_BLOCK
    hd = B_HEAD_DIM
    n_heads = (LANES // hd) * pl.num_programs(1)
    qkv = ((q0, k0, v0), (q1, k1, v1), (q2, k2, v2))
    head_a = lax.broadcasted_iota(jnp.int32, (blk, LANES), 1) < hd


    def token_rows(g, tok0):
        return pl.ds(tok0, blk) if dils[g] == 1 else pl.ds(tok0, blk, stride=dils[g])

    def qk_phase(g, specs, has_prev, slot):
        q_ref, kt_ref, _ = qkv[g]
        nk = 2 * blk if has_prev else blk
        for w, (row0, _) in enumerate(specs):
            q2d = q_ref[pl.ds(row0, blk), :]
            zero = jnp.zeros_like(q2d)
            qq = jnp.concatenate([jnp.where(head_a, q2d, zero), jnp.where(head_a, zero, q2d)], axis=0)
            kt = kt_ref[pl.ds(row0, blk), :]
            if has_prev:
                kt = jnp.concatenate([kt_ref[pl.ds(row0 - blk, blk), :], kt], axis=1)
            s_ref[slot, w, :, 0:nk] = _dot(qq, kt)

    def softmax_phase(g, specs, has_prev, slot):
        nk = 2 * blk if has_prev else blk
        off = 0 if has_prev else blk
        for w, (_, tok0) in enumerate(specs):
            ms, ls = [], []
            for a in range(2):
                sa = s_ref[slot, w, a * blk:(a + 1) * blk, 0:nk] + bias_ref[a, :, off:off + nk]
                m = jnp.max(sa, axis=-1, keepdims=True)
                e = jnp.exp2(sa - m)
                p_ref[slot, w, a * blk:(a + 1) * blk, 0:nk] = e.astype(BF16)
                ms.append(m)
                ls.append(jnp.sum(e, axis=-1, keepdims=True))
            idx = token_rows(g, tok0)
            max_ref[g, idx, :] = jnp.where(head_a, ms[0], ms[1])
            den_ref[g, idx, :] = jnp.where(head_a, ls[0], ls[1])

    def pv_phase(g, specs, has_prev, slot):
        v_ref = qkv[g][2]
        nk = 2 * blk if has_prev else blk
        for w, (row0, tok0) in enumerate(specs):
            k_lo = row0 - blk if has_prev else row0
            o = _dot(p_ref[slot, w, :, 0:nk], v_ref[pl.ds(k_lo, nk), :])
            num_ref[g, token_rows(g, tok0), :] = jnp.where(head_a, o[0:blk, :], o[blk:2 * blk, :])

    def pipelined(n_iter, make_specs, g, has_prev):
        def step(t):
            static = isinstance(t, int)
            if (not static) or 0 <= t - 2 < n_iter:
                pv_phase(g, make_specs(t - 2), has_prev, t % 2)
            if (not static) or 0 <= t - 1 < n_iter:
                softmax_phase(g, make_specs(t - 1), has_prev, (t - 1) % 2)
            if (not static) or 0 <= t < n_iter:
                qk_phase(g, make_specs(t), has_prev, t % 2)

        for t in range(2):
            step(t)
        if n_iter > 2:
            def body(t, carry):
                step(t)
                return carry

            lax.fori_loop(2, n_iter, body, 0)
        for t in range(max(n_iter, 2), n_iter + 2):
            step(t)

    qi = lax.broadcasted_iota(jnp.int32, (blk, 2 * blk), 0)
    kj = lax.broadcasted_iota(jnp.int32, (blk, 2 * blk), 1)
    dist = blk + qi - kj
    valid = (dist >= 0) & (dist <= blk)

    for g, dil in enumerate(dils):
        sub = seq // dil
        n_blk = sub // blk
        bias_base = jnp.where(valid, -(dist * dil).astype(F32), NEG_BIG)
        for a in range(2):
            slope = slopes_ref[g * n_heads + 2 * hp + a]
            bias_ref[a] = (slope * LOG2E) * bias_base

        u1, u2 = unrolls[g]

        def aligned(row):
            return row if isinstance(row, int) else pl.multiple_of(row, blk)

        def first_specs(i, dil=dil, sub=sub, u1=u1):
            return [(aligned((i * u1 + w) * sub), _residue_of_slot(dil, i * u1 + w)) for w in range(u1)]

        def rest_specs(i, dil=dil, sub=sub, n_blk=n_blk, u2=u2):
            specs = []
            for w in range(u2):
                idx = i * u2 + w
                r = idx // (n_blk - 1)
                b = idx % (n_blk - 1) + 1
                specs.append((aligned(r * sub + b * blk), b * blk * dil + r))
            return specs

        pipelined(dil // u1, first_specs, g, False)
        n_rest = dil * (n_blk - 1)
        if n_rest:
            pipelined(n_rest // u2, rest_specs, g, True)

    rb = 256
    n_g = len(dils)

    def merge(i, carry):
        r0 = pl.multiple_of(i * rb, rb)
        ms = [max_ref[g, pl.ds(r0, rb), :] for g in range(n_g)]
        mx = ms[0]
        for g in range(1, n_g):
            mx = jnp.maximum(mx, ms[g])
        num = jnp.zeros((rb, LANES), F32)
        den = jnp.zeros((rb, LANES), F32)
        for g in range(n_g):
            w = jnp.exp2(ms[g] - mx)
            num = num + w * num_ref[g, pl.ds(r0, rb), :]
            den = den + w * den_ref[g, pl.ds(r0, rb), :]
        gate = g_ref[pl.ds(r0, rb), :].astype(F32)
        o_ref[pl.ds(r0, rb), :] = (num / den * gate).astype(BF16)
        return carry

    lax.fori_loop(0, seq // rb, merge, 0)


def _attention(proj, slopes):
    bsz, n_tiles, n_pairs, seq, _ = proj.shape
    assert n_tiles == 3 * B_N_GROUPS + 1
    dils = tuple(dil for _, dil in DILATED_GROUPS)
    unrolls = ((1, 5), (4, 6), (4, 1))
    for (u1, u2), dil in zip(unrolls, dils):
        n_rest = dil * (seq // dil // B_BLOCK - 1)
        assert dil % u1 == 0 and n_rest % u2 == 0

    def col_spec(tile):
        return pl.BlockSpec((None, None, None, seq, LANES), lambda b, h, s, tile=tile: (b, tile, h, 0, 0))

    kern = functools.partial(_attn_kernel, seq=seq, dils=dils, unrolls=unrolls)
    grid_spec = pltpu.PrefetchScalarGridSpec(
        num_scalar_prefetch=1,
        grid=(bsz, n_pairs),
        in_specs=[col_spec(t) for t in range(n_tiles)],
        out_specs=pl.BlockSpec((None, None, seq, LANES), lambda b, h, s: (b, h, 0, 0)),
        scratch_shapes=[pltpu.VMEM((B_N_GROUPS, seq, LANES), F32)] * 3
        + [pltpu.VMEM((2, B_BLOCK, 2 * B_BLOCK), F32),
           pltpu.VMEM((2, max(max(u) for u in unrolls), 2 * B_BLOCK, 2 * B_BLOCK), F32),
           pltpu.VMEM((2, max(max(u) for u in unrolls), 2 * B_BLOCK, 2 * B_BLOCK), BF16)],
    )
    return pl.pallas_call(
        kern,
        grid_spec=grid_spec,
        out_shape=jax.ShapeDtypeStruct((bsz, n_pairs, seq, LANES), BF16),
        compiler_params=_params(("arbitrary", "arbitrary")),
        name="dilated_attention",
    )(slopes, *([proj] * n_tiles))


def _gmlp_kernel(x_ref, yprev_ref, wprev_ref, nw_ref, w_ref, vw_ref, vb_ref, ws_ref, bias_ref, wout_ref, o_ref,
                 x1_ref, h_ref, u_ref, v_ref, g_ref, y_ref, *, tm):
    d = x_ref.shape[-1]
    cg = d // C_GROUPS
    rb = 256
    for i in range(tm // rb):
        rows = slice(i * rb, (i + 1) * rb)
        y_prev = jnp.concatenate([yprev_ref[hp, rows, :] for hp in range(yprev_ref.shape[0])], axis=1)
        x1_ref[rows, :] = x_ref[0, rows, :] + _dot(y_prev, wprev_ref[...])
        h_ref[rows, :] = _rms_rows(x1_ref[rows, :], nw_ref[...]).astype(BF16)
    for i in range(tm // rb):
        rows = slice(i * rb, (i + 1) * rb)
        hh = h_ref[rows, :]
        u_ref[rows, :] = _gelu_tanh(_dot(hh, w_ref[:, 0:d]))
        vv = _gelu_tanh(_dot(hh, w_ref[:, d:2 * d]))
        mu = jnp.mean(vv, axis=-1, keepdims=True)
        vc = vv - mu
        var = jnp.mean(vc * vc, axis=-1, keepdims=True)
        v_ref[rows, :] = (vc * lax.rsqrt(var + NORM_EPS) * vw_ref[...] + vb_ref[...]).astype(BF16)
        g_ref[rows, :] = _silu(_dot(hh, w_ref[:, 2 * d:3 * d]))
    ti = lax.broadcasted_iota(jnp.int32, (C_CHUNK, C_CHUNK), 0)
    si = lax.broadcasted_iota(jnp.int32, (C_CHUNK, C_CHUNK), 1)
    n_ch = tm // C_CHUNK
    for grp in range(C_GROUPS):
        cols = slice(grp * cg, (grp + 1) * cg)
        wc = jnp.where(si <= ti, ws_ref[grp], 0.0).astype(BF16)
        rhs = jnp.concatenate([v_ref[n * C_CHUNK:(n + 1) * C_CHUNK, cols] for n in range(n_ch)], axis=1)
        s = _dot(wc, rhs)
        bias = bias_ref[:, cols]
        for n in range(n_ch):
            rows = slice(n * C_CHUNK, (n + 1) * C_CHUNK)
            sn = s[:, n * cg:(n + 1) * cg] + bias
            y_ref[rows, cols] = (u_ref[rows, cols] * sn * g_ref[rows, cols]).astype(BF16)
    for i in range(tm // rb):
        rows = slice(i * rb, (i + 1) * rb)
        o_ref[0, rows, :] = x1_ref[rows, :] + _dot(y_ref[rows, :], wout_ref[...])


def _gmlp_mixer(x, y_prev_pairs, w_prev_bf16, norm_w_row, w_bf16, vw_row, vb_row, w_s, bias_full, w_out_bf16,
                tm=512):
    bsz, seq, d = x.shape
    n_pairs, lanes = y_prev_pairs.shape[1], y_prev_pairs.shape[3]
    kern = functools.partial(_gmlp_kernel, tm=tm)
    return pl.pallas_call(
        kern,
        grid=(bsz, seq // tm),
        in_specs=[pl.BlockSpec((1, tm, d), lambda b, i: (b, i, 0)),
                  pl.BlockSpec((None, n_pairs, tm, lanes), lambda b, i: (b, 0, i, 0)),
                  pl.BlockSpec((d, d), lambda b, i: (0, 0)),
                  pl.BlockSpec((1, d), lambda b, i: (0, 0)),
                  pl.BlockSpec((d, 3 * d), lambda b, i: (0, 0)),
                  pl.BlockSpec((1, d), lambda b, i: (0, 0)),
                  pl.BlockSpec((1, d), lambda b, i: (0, 0)),
                  pl.BlockSpec(w_s.shape, lambda b, i: (0, 0, 0)),
                  pl.BlockSpec((C_CHUNK, d), lambda b, i: (0, 0)),
                  pl.BlockSpec((d, d), lambda b, i: (0, 0))],
        out_specs=pl.BlockSpec((1, tm, d), lambda b, i: (b, i, 0)),
        out_shape=jax.ShapeDtypeStruct((bsz, seq, d), F32),
        scratch_shapes=[pltpu.VMEM((tm, d), F32),
                        pltpu.VMEM((tm, d), BF16),
                        pltpu.VMEM((tm, d), F32),
                        pltpu.VMEM((tm, d), BF16),
                        pltpu.VMEM((tm, d), F32),
                        pltpu.VMEM((tm, d), BF16)],
        compiler_params=_params(("arbitrary", "arbitrary")),
        name="gmlp_mixer",
    )(x, y_prev_pairs, w_prev_bf16, norm_w_row, w_bf16, vw_row, vb_row, w_s, bias_full, w_out_bf16)


def kernel(x, norm_w, a_w_in, a_lower_bounds, a_o_norm_w, a_w_out, b_w_in, b_q_norm_w, b_k_norm_w, b_w_out,
           c_w_in, c_v_norm_w, c_v_norm_b, c_w_s, c_b_s, c_w_out):
    bsz, seq, d = x.shape
    depth = norm_w.shape[0]
    n_mixers = 3
    assert seq == DILATED_GROUPS[-1][0] and seq % A_CHUNK == 0 and d % LANES == 0

    def residual(y, w_out, xin):
        out = _out_proj(y.reshape(bsz * seq, d), w_out.astype(BF16), xin.reshape(bsz * seq, d))
        return out.reshape(bsz, seq, d)

    pending = None
    for layer in range(depth):
        kind, idx = layer % n_mixers, layer // n_mixers
        nw = norm_w[layer][None, :]
        if kind == 0:
            y = _hgrn_mixer(x, nw, a_w_in[idx].astype(BF16), a_lower_bounds, a_o_norm_w[idx][None, :], idx)
            x = residual(y, a_w_out[idx], x)
        elif kind == 1:
            n_heads = d // B_HEAD_DIM
            n_total = B_N_GROUPS * n_heads
            slopes = jnp.exp2(-ALIBI_MAX_EXP * jnp.arange(1, n_total + 1, dtype=F32) / n_total)
            rows = []
            for g in range(B_N_GROUPS):
                rows.append(jnp.tile(b_q_norm_w[idx, g] * b_k_norm_w[idx, g], n_heads)
                            * (B_HEAD_DIM ** -0.5 * LOG2E))
                rows.append(jnp.ones((d,), F32))
                rows.append(jnp.ones((d,), F32))
            rows.append(jnp.ones((d,), F32))
            head_w = jnp.stack(rows)[:, None, :]
            proj = _attn_in_proj(x, nw, b_w_in[idx], head_w)
            pending = (_attention(proj, slopes), b_w_out[idx].astype(BF16))
            if layer == depth - 1:
                x = _out_proj_pairs(*pending, x)
        else:
            bias_full = jnp.repeat(c_b_s[idx].T, d // C_GROUPS, axis=1)
            x = _gmlp_mixer(x, *pending, nw, c_w_in[idx].astype(BF16), c_v_norm_w[idx][None, :],
                            c_v_norm_b[idx][None, :], c_w_s[idx], bias_full, c_w_out[idx].astype(BF16))
    return x
```

```python
import functools

import numpy as np
import jax
import jax.numpy as jnp
from jax import lax
from jax.experimental import pallas as pl
from jax.experimental.pallas import tpu as pltpu

F32 = jnp.float32
BF16 = jnp.bfloat16

NORM_EPS = 1e-6
NEG_BIG = -1e30
LB_FLOOR = 1e-30
LOG2E = float(np.log2(np.e))

LANES = 128
SUBLANES = 8
VMEM_LIMIT_BYTES = 60 * 1024 * 1024

A_HEAD_DIM = 128
A_CHUNK = 128
A_CHUNKS_PER_ITER = 2
A_HEADS_PER_STEP = 4
B_HEAD_DIM = 64
DILATED_GROUPS = ((128, 1), (512, 4), (2048, 16))
B_N_GROUPS = len(DILATED_GROUPS)
B_BLOCK = 128
B_GATHER_STRIDE = 4
ALIBI_MAX_EXP = 8.0
C_CHUNK = 128
C_GROUPS = 8


def _silu(x):
    hx = 0.5 * x
    return hx + hx * jnp.tanh(hx)


def _gelu_tanh(x):
    c = np.float32(np.sqrt(2.0 / np.pi))
    return 0.5 * x * (1.0 + jnp.tanh(c * (x + 0.044715 * (x * x * x))))


def _rms_rows(xf, w):
    ms = jnp.mean(xf * xf, axis=-1, keepdims=True)
    return xf * lax.rsqrt(ms + NORM_EPS) * w


def _dot(a, b):
    return jnp.dot(a, b, preferred_element_type=F32)


def _dot_nt(a, b):
    return lax.dot_general(a, b, (((1,), (1,)), ((), ())), preferred_element_type=F32)


def _dot_tn(a, b):
    return lax.dot_general(a, b, (((0,), (0,)), ((), ())), preferred_element_type=F32)


def _params(sem):
    return pltpu.CompilerParams(dimension_semantics=sem, vmem_limit_bytes=VMEM_LIMIT_BYTES)


def _out_proj_kernel(y_ref, w_ref, x_ref, o_ref):
    o_ref[...] = x_ref[...] + _dot(y_ref[...], w_ref[...])


def _out_proj(y2d, w_bf16, x2d, tm=2048):
    m, k = y2d.shape
    n = w_bf16.shape[1]
    return pl.pallas_call(
        _out_proj_kernel,
        grid=(m // tm,),
        in_specs=[pl.BlockSpec((tm, k), lambda i: (i, 0)),
                  pl.BlockSpec((k, n), lambda i: (0, 0)),
                  pl.BlockSpec((tm, n), lambda i: (i, 0))],
        out_specs=pl.BlockSpec((tm, n), lambda i: (i, 0)),
        out_shape=jax.ShapeDtypeStruct((m, n), F32),
        compiler_params=_params(("arbitrary",)),
        name="out_proj",
    )(y2d, w_bf16, x2d)


def _out_proj_pairs_kernel(y_ref, w_ref, x_ref, o_ref):
    y = jnp.concatenate([y_ref[hp] for hp in range(y_ref.shape[0])], axis=1)
    o_ref[...] = x_ref[...] + _dot(y, w_ref[...])


def _out_proj_pairs(y_pairs, w_bf16, x, tm=1024):
    bsz, n_pairs, seq, lanes = y_pairs.shape
    d = n_pairs * lanes
    n = w_bf16.shape[1]
    return pl.pallas_call(
        _out_proj_pairs_kernel,
        grid=(bsz, seq // tm),
        in_specs=[pl.BlockSpec((None, n_pairs, tm, lanes), lambda b, i: (b, 0, i, 0)),
                  pl.BlockSpec((d, n), lambda b, i: (0, 0)),
                  pl.BlockSpec((None, tm, n), lambda b, i: (b, i, 0))],
        out_specs=pl.BlockSpec((None, tm, n), lambda b, i: (b, i, 0)),
        out_shape=jax.ShapeDtypeStruct((bsz, seq, n), F32),
        compiler_params=_params(("arbitrary", "arbitrary")),
        name="out_proj_pairs",
    )(y_pairs, w_bf16, x)


A_BAND = 4


def _hgrn_levels(chunk):
    return [1 << j for j in range(int(np.log2(chunk))) if (2 << j) > A_BAND]


def _hgrn_level_exponents(b, b_view, sgn_ref, chunk):
    out = {}
    n_vregs = chunk // SUBLANES
    width = b.shape[-1]

    def row(r, n):
        one_group = jnp.broadcast_to(b_view[pl.ds(r, 1), :], (SUBLANES, width))
        return jnp.concatenate([one_group] * (n // SUBLANES), axis=0)

    for m in _hgrn_levels(chunk):
        if m >= SUBLANES:
            parts = []
            for j in range(chunk // (2 * m)):
                base = j * 2 * m
                ref_row = row(base + m - 1, m)
                parts.append(ref_row - b[base:base + m, :])
                parts.append(b[base + m:base + 2 * m, :] - ref_row)
            out[m] = jnp.concatenate(parts, axis=0)
        else:
            assert m == 4
            ref = jnp.concatenate([row(SUBLANES * i + 3, SUBLANES) for i in range(n_vregs)], axis=0)
            out[m] = (b - ref) * sgn_ref[...]
    return out


def _hgrn_kernel(x_ref, nw_ref, wq_ref, wf_ref, wi_ref, wg_ref, lbraw_ref, onw_ref, tri_ref, y_ref,
                 h_ref, act_ref, v16_ref, b_ref, st_ref, lvl_ref, band_ref, sgn_ref, qd_ref, kd_ref, a16_ref,
                 dec_ref,
                 *, layer_idx, seq, chunk, hps, cpi):
    step = pl.program_id(1)
    dh = A_HEAD_DIM
    rb = 256

    @pl.when(step == 0)
    def _():
        for i in range(seq // rb):
            xs = x_ref[0, i * rb:(i + 1) * rb, :]
            h_ref[i * rb:(i + 1) * rb, :] = _rms_rows(xs, nw_ref[...]).astype(BF16)

    raw = lbraw_ref[...]
    n_layers = raw.shape[0]
    mx = raw[0:1, :]
    for i in range(1, n_layers):
        mx = jnp.maximum(mx, raw[i:i + 1, :])
    es = [jnp.exp(raw[i:i + 1, :] - mx) for i in range(n_layers)]
    z = es[0]
    for i in range(1, n_layers):
        z = z + es[i]
    soft = [e / z for e in es]
    csum = soft[0]
    for i in range(1, layer_idx + 1):
        csum = csum + soft[i]
    lb = csum - soft[0]
    log_lb = jnp.log(jnp.maximum(lb, LB_FLOOR))
    log1m_lb = jnp.log1p(-lb)
    one_m_lb = 1.0 - lb
    heads = range(hps)

    pb = 512
    for i in range(seq // pb):
        rows = slice(i * pb, (i + 1) * pb)
        hh = h_ref[rows, :]
        res_q, res_f, res_i, res_g = (_dot(hh, w[...]) for w in (wq_ref, wf_ref, wi_ref, wg_ref))
        v16_ref[rows, :] = res_i.astype(BF16)
        for a in heads:
            o0 = a * 4 * dh
            lbs = slice(a * dh, (a + 1) * dh)
            pf = res_f[:, lbs]
            log_sig = jnp.minimum(pf, 0.0) - jnp.log(1.0 + jnp.exp(-jnp.abs(pf)))
            bb = log1m_lb[:, lbs] + log_sig
            ll = log_lb[:, lbs]
            log_f = jnp.maximum(ll, bb) + jnp.log(1.0 + jnp.exp(-jnp.abs(ll - bb)))
            act_ref[rows, o0:o0 + dh] = _silu(res_q[:, lbs])
            act_ref[rows, o0 + dh:o0 + 2 * dh] = one_m_lb[:, lbs] * jnp.exp(log_sig - pf)
            act_ref[rows, o0 + 2 * dh:o0 + 3 * dh] = _silu(res_g[:, lbs])
            act_ref[rows, o0 + 3 * dh:o0 + 4 * dh] = log_f * LOG2E

    ti = lax.broadcasted_iota(jnp.int32, (chunk, chunk), 0)
    si = lax.broadcasted_iota(jnp.int32, (chunk, chunk), 1)
    lvl_ref[...] = jnp.where(si < ti, ti ^ si, jnp.where(si == ti, 0, -1))
    band_ref[...] = jnp.where((si <= ti) & ((ti ^ si) < A_BAND), ti - si, -1)
    tr = lax.broadcasted_iota(jnp.int32, (chunk, dh), 0)
    sgn_ref[...] = jnp.where((tr & 4) != 0, 1.0, -1.0)
    st_ref[...] = jnp.zeros_like(st_ref)
    onw = onw_ref[...]
    levels = _hgrn_levels(chunk)

    def row_start(c):
        return c * chunk if isinstance(c, int) else pl.multiple_of(c * chunk, chunk)

    def front(chunks):
        q, k, log_f = [], [], []
        for c in chunks:
            for a in heads:
                o0 = a * 4 * dh
                q.append(act_ref[pl.ds(row_start(c), chunk), o0:o0 + dh])
                k.append(act_ref[pl.ds(row_start(c), chunk), o0 + dh:o0 + 2 * dh])
                log_f.append(act_ref[pl.ds(row_start(c), chunk), o0 + 3 * dh:o0 + 4 * dh])
        slots = range(len(q))

        b = []
        for a in slots:
            lf_hi = log_f[a].astype(BF16)
            lf_lo = (log_f[a] - lf_hi.astype(F32)).astype(BF16)
            b.append(_dot(tri_ref[...], lf_hi) + _dot(tri_ref[...], lf_lo))
            b_ref[a] = b[a]

        expo, acc = [], []
        for a in slots:
            expo.append(_hgrn_level_exponents(b[a], b_ref.at[a], sgn_ref, chunk))
            qd_ref[a] = (q[a] * jnp.exp2(b[a])).astype(BF16)
            b_last = b_ref[a, chunk - 1:chunk, :]
            kd_ref[a] = (k[a] * jnp.exp2(b_last - b[a])).astype(BF16)
            dec_ref[a] = jnp.exp2(b_last)
        n_grp = chunk // SUBLANES

        def shifted(x, dist):
            return pltpu.roll(x.reshape(n_grp, SUBLANES, dh), dist, axis=1).reshape(chunk, dh)

        for a in slots:
            acc.append(jnp.where(band_ref[...] == 0, jnp.sum(q[a] * k[a], axis=-1, keepdims=True), 0.0))
        for dist in range(1, A_BAND):
            for a in slots:
                decay = jnp.exp2(b[a] - shifted(b[a], dist))
                c = jnp.sum(q[a] * shifted(k[a], dist) * decay, axis=-1, keepdims=True)
                acc[a] = jnp.where(band_ref[...] == dist, c, acc[a])
        for m in levels:
            if m >= SUBLANES:
                break
            is_query_row = sgn_ref[...] > 0.0
            for a in slots:
                w = (jnp.where(is_query_row, q[a], k[a]) * jnp.exp2(expo[a][m])).astype(BF16)
                acc[a] = jnp.where(lvl_ref[...] >= m, _dot_nt(w, w), acc[a])
        n_grp = chunk // SUBLANES
        acc = [[acc_a[SUBLANES * i:SUBLANES * (i + 1), :] for i in range(n_grp)] for acc_a in acc]
        for m in levels:
            if m < SUBLANES:
                continue
            is_query = [(SUBLANES * i) % (2 * m) >= m for i in range(n_grp)]
            for a in slots:
                roles = jnp.concatenate([(q[a] if is_query[i] else k[a])[SUBLANES * i:SUBLANES * (i + 1), :]
                                         for i in range(n_grp)], axis=0)
                w = roles * jnp.exp2(expo[a][m])
                q_m = jnp.concatenate([w[SUBLANES * i:SUBLANES * (i + 1), :]
                                       for i in range(n_grp) if is_query[i]], axis=0)
                a_m = _dot_nt(q_m.astype(BF16), w.astype(BF16))
                for n, i in enumerate([i for i in range(n_grp) if is_query[i]]):
                    rows = slice(SUBLANES * i, SUBLANES * (i + 1))
                    acc[a][i] = jnp.where(lvl_ref[rows, :] >= m,
                                          a_m[SUBLANES * n:SUBLANES * (n + 1), :], acc[a][i])
        for a in slots:
            a16_ref[a] = jnp.concatenate(acc[a], axis=0).astype(BF16)

    def back(chunks):
        for n, c in enumerate(chunks):
            r0 = row_start(c)
            for a in heads:
                o0 = a * 4 * dh
                slot = n * hps + a
                v16 = v16_ref[pl.ds(r0, chunk), a * dh:(a + 1) * dh]
                st = st_ref[a]
                oa = _dot_nt(qd_ref[slot], st.astype(BF16)) + _dot(a16_ref[slot], v16)
                st_ref[a] = st * dec_ref[slot] + _dot_tn(v16, kd_ref[slot])
                y = _rms_rows(oa, onw) * act_ref[pl.ds(r0, chunk), o0 + 2 * dh:o0 + 3 * dh]
                y_ref[0, pl.ds(r0, chunk), a * dh:(a + 1) * dh] = y.astype(BF16)

    n_iter = seq // (chunk * cpi)
    front([n for n in range(cpi)])

    def body(i, carry):
        back([(i - 1) * cpi + n for n in range(cpi)])
        front([i * cpi + n for n in range(cpi)])
        return carry

    lax.fori_loop(1, n_iter, body, 0)
    back([(n_iter - 1) * cpi + n for n in range(cpi)])


def _hgrn_mixer(x, norm_w_row, w_in, lb_raw, o_norm_w_row, layer_idx):
    bsz, seq, d = x.shape
    dh = A_HEAD_DIM
    hps = A_HEADS_PER_STEP
    n_steps = d // (dh * hps)
    chunk = A_CHUNK
    tri = jnp.asarray(np.tril(np.ones((chunk, chunk), np.float32)), BF16)
    cpi = A_CHUNKS_PER_ITER
    n_slots = hps * cpi
    assert seq % (chunk * cpi) == 0
    kern = functools.partial(_hgrn_kernel, layer_idx=layer_idx, seq=seq, chunk=chunk, hps=hps, cpi=cpi)

    def w_spec(part):
        return pl.BlockSpec((d, hps * dh), lambda b, h, part=part: (0, part * n_steps + h))

    return pl.pallas_call(
        kern,
        grid=(bsz, n_steps),
        in_specs=[pl.BlockSpec((1, seq, d), lambda b, h: (b, 0, 0)),
                  pl.BlockSpec((1, d), lambda b, h: (0, 0)),
                  w_spec(0), w_spec(1), w_spec(2), w_spec(3),
                  pl.BlockSpec((lb_raw.shape[0], hps * dh), lambda b, h: (0, h)),
                  pl.BlockSpec((1, dh), lambda b, h: (0, 0)),
                  pl.BlockSpec((chunk, chunk), lambda b, h: (0, 0))],
        out_specs=pl.BlockSpec((1, seq, hps * dh), lambda b, h: (b, 0, h)),
        out_shape=jax.ShapeDtypeStruct((bsz, seq, d), BF16),
        scratch_shapes=[pltpu.VMEM((seq, d), BF16),
                        pltpu.VMEM((seq, hps * 4 * dh), F32),
                        pltpu.VMEM((seq, hps * dh), BF16),
                        pltpu.VMEM((n_slots, chunk, dh), F32),
                        pltpu.VMEM((hps, dh, dh), F32),
                        pltpu.VMEM((chunk, chunk), jnp.int32),
                        pltpu.VMEM((chunk, chunk), jnp.int32),
                        pltpu.VMEM((chunk, dh), F32),
                        pltpu.VMEM((n_slots, chunk, dh), BF16),
                        pltpu.VMEM((n_slots, chunk, dh), BF16),
                        pltpu.VMEM((n_slots, chunk, chunk), BF16),
                        pltpu.VMEM((n_slots, 1, dh), F32)],
        compiler_params=_params(("arbitrary", "arbitrary")),
        name="hgrn_mixer",
    )(x, norm_w_row, w_in, w_in, w_in, w_in, lb_raw, o_norm_w_row, tri)


def _residue_of_slot(dil, slot):
    if dil <= B_GATHER_STRIDE:
        return slot
    return slot // B_GATHER_STRIDE + B_GATHER_STRIDE * (slot % B_GATHER_STRIDE)


def _bproj_kernel(x_ref, nw_ref, w_ref, wkt_ref, hw_ref, o_ref, h_ref, hn_ref, tmp_ref,
                  *, seq, dils):
    j = pl.program_id(1)
    n_qkv_tiles = 3 * len(dils)

    @pl.when(j == 0)
    def _():
        d = x_ref.shape[-1]
        n_planes = d // LANES
        rb = 256
        for i in range(seq // rb):
            hn = _rms_rows(x_ref[0, i * rb:(i + 1) * rb, :], nw_ref[...])
            for c in range(n_planes):
                hn_ref[c, i * rb:(i + 1) * rb, :] = hn[:, c * LANES:(c + 1) * LANES]
        assert dils == (1, B_GATHER_STRIDE, B_GATHER_STRIDE ** 2)
        step = B_GATHER_STRIDE
        sub = seq // step
        for c in range(n_planes):
            cols = slice(c * LANES, (c + 1) * LANES)
            for c0 in range(0, seq, rb):
                h_ref[0, c0:c0 + rb, cols] = hn_ref[c, c0:c0 + rb, :].astype(BF16)
            for r in range(step):
                for c0 in range(0, sub, rb):
                    rows = hn_ref[c, pl.ds(r + c0 * step, rb, stride=step), :]
                    tmp_ref[r * sub + c0:r * sub + c0 + rb, :] = rows
                    h_ref[1, r * sub + c0:r * sub + c0 + rb, cols] = rows.astype(BF16)
            sub2 = sub // step
            for r in range(step):
                for r2 in range(step):
                    rows = tmp_ref[pl.ds(r * sub + r2, sub2, stride=step), :]
                    slot = r * step + r2
                    h_ref[2, slot * sub2:(slot + 1) * sub2, cols] = rows.astype(BF16)

    grp = jnp.where(j < n_qkv_tiles, j // 3, 0)
    kind = jnp.where(j < n_qkv_tiles, j % 3, 3)
    pb = 512

    def tile(i):
        return _dot(h_ref[grp, i * pb:(i + 1) * pb, :], w_ref[...])

    def store(i, val):
        for hp in range(val.shape[-1] // LANES):
            o_ref[hp, i * pb:(i + 1) * pb, :] = val[:, hp * LANES:(hp + 1) * LANES].astype(BF16)

    @pl.when(kind == 1)
    def _():
        d = x_ref.shape[-1]
        hd = B_HEAD_DIM
        for i in range(seq // pb):
            yt = _dot_nt(wkt_ref[...], h_ref[grp, i * pb:(i + 1) * pb, :])
            for h in range(d // hd):
                yh = yt[h * hd:(h + 1) * hd, :]
                r = lax.rsqrt(jnp.mean(yh * yh, axis=0, keepdims=True) + NORM_EPS)
                yh = (yh * r).astype(BF16)
                hp, half = divmod(h * hd, LANES)
                for kb in range(pb // LANES):
                    row0 = (i * (pb // LANES) + kb) * LANES + half
                    o_ref[hp, row0:row0 + hd, :] = yh[:, kb * LANES:(kb + 1) * LANES]

    @pl.when(kind == 0)
    def _():
        first_head = lax.broadcasted_iota(jnp.int32, (pb, LANES), 1) < B_HEAD_DIM
        for i in range(seq // pb):
            y = tile(i)
            for hp in range(y.shape[-1] // LANES):
                cols = slice(hp * LANES, (hp + 1) * LANES)
                yc = y[:, cols]
                sq = yc * yc
                ss_a = jnp.sum(jnp.where(first_head, sq, 0.0), axis=-1, keepdims=True)
                ss_b = jnp.sum(jnp.where(first_head, 0.0, sq), axis=-1, keepdims=True)
                r_a = lax.rsqrt(ss_a * (1.0 / B_HEAD_DIM) + NORM_EPS)
                r_b = lax.rsqrt(ss_b * (1.0 / B_HEAD_DIM) + NORM_EPS)
                o_ref[hp, i * pb:(i + 1) * pb, :] = (
                    yc * jnp.where(first_head, r_a, r_b) * hw_ref[0, :, cols]).astype(BF16)

    @pl.when(kind == 2)
    def _():
        for i in range(seq // pb):
            store(i, tile(i))

    @pl.when(kind == 3)
    def _():
        for i in range(seq // pb):
            store(i, _silu(tile(i)))


def _transpose_tiles_kernel(w_ref, o_ref):
    n = w_ref.shape[0] // LANES
    for bi in range(n):
        for bj in range(n):
            o_ref[bj * LANES:(bj + 1) * LANES, bi * LANES:(bi + 1) * LANES] = (
                w_ref[bi * LANES:(bi + 1) * LANES, bj * LANES:(bj + 1) * LANES].T)


def _transposed_k_weights(w_bf16, d, n_grp):
    return pl.pallas_call(
        _transpose_tiles_kernel,
        grid=(n_grp,),
        in_specs=[pl.BlockSpec((d, d), lambda g: (0, 3 * g + 1))],
        out_specs=pl.BlockSpec((None, d, d), lambda g: (g, 0, 0)),
        out_shape=jax.ShapeDtypeStruct((n_grp, d, d), BF16),
        compiler_params=_params(("arbitrary",)),
        name="transpose_k_weights",
    )(w_bf16)


def _attn_in_proj(x, norm_w_row, w_in, head_w):
    bsz, seq, d = x.shape
    n_out = w_in.shape[1]
    tn = d
    dils = tuple(dil for _, dil in DILATED_GROUPS)
    n_grp = len(dils)
    w_bf16 = w_in.astype(BF16)
    w_k_t = _transposed_k_weights(w_bf16, d, n_grp)
    kern = functools.partial(_bproj_kernel, seq=seq, dils=dils)
    return pl.pallas_call(
        kern,
        grid=(bsz, n_out // tn),
        in_specs=[pl.BlockSpec((1, seq, d), lambda b, j: (b, 0, 0)),
                  pl.BlockSpec((1, d), lambda b, j: (0, 0)),
                  pl.BlockSpec((d, tn), lambda b, j: (0, j)),
                  pl.BlockSpec((None, tn, d), lambda b, j: (jnp.minimum(j // 3, n_grp - 1), 0, 0)),
                  pl.BlockSpec((1, 1, tn), lambda b, j: (j, 0, 0))],
        out_specs=pl.BlockSpec((None, None, tn // LANES, seq, LANES), lambda b, j: (b, j, 0, 0, 0)),
        out_shape=jax.ShapeDtypeStruct((bsz, n_out // tn, tn // LANES, seq, LANES), BF16),
        scratch_shapes=[pltpu.VMEM((len(dils), seq, d), BF16),
                        pltpu.VMEM((d // LANES, seq, LANES), F32),
                        pltpu.VMEM((seq, LANES), F32)],
        compiler_params=_params(("arbitrary", "arbitrary")),
        name="attn_in_proj",
    )(x, norm_w_row, w_bf16, w_k_t, head_w)


def _attn_kernel(slopes_ref, q0, k0, v0, q1, k1, v1, q2, k2, v2, g_ref, o_ref,
                 num_ref, den_ref, max_ref, bias_ref, s_ref, p_ref, stage_ref, *, seq, dils, unrolls):
    hp = pl.program_id(1)
    blk = B_BLOCK
    hd = B_HEAD_DIM
    n_heads = (LANES // hd) * pl.num_programs(1)
    qkv = ((q0, k0, v0), (q1, k1, v1), (q2, k2, v2))
    head_a = lax.broadcasted_iota(jnp.int32, (blk, LANES), 1) < hd


    step4 = B_GATHER_STRIDE

    def two_step(g):
        return dils[g] > step4

    def token_rows(g, tok0):
        if dils[g] == 1:
            return pl.ds(tok0, blk)
        if two_step(g):
            return pl.ds((tok0 % step4) * (seq // step4) + tok0 // step4, blk, stride=step4)
        return pl.ds(tok0, blk, stride=dils[g])

    def put(which, g, tok0, val):
        if two_step(g):
            stage_ref[which, token_rows(g, tok0), :] = val
        else:
            (num_ref, den_ref, max_ref)[which][g, token_rows(g, tok0), :] = val

    def unstage(g):
        sub4 = seq // step4
        for which, ref in enumerate((num_ref, den_ref, max_ref)):
            for r in range(step4):
                for c0 in range(0, sub4, blk):
                    ref[g, pl.ds(r + c0 * step4, blk, stride=step4), :] = (
                        stage_ref[which, r * sub4 + c0:r * sub4 + c0 + blk, :])

    def qk_phase(g, specs, has_prev, slot):
        q_ref, kt_ref, _ = qkv[g]
        nk = 2 * blk if has_prev else blk
        for w, (row0, _) in enumerate(specs):
            q2d = q_ref[pl.ds(row0, blk), :]
            zero = jnp.zeros_like(q2d)
            qq = jnp.concatenate([jnp.where(head_a, q2d, zero), jnp.where(head_a, zero, q2d)], axis=0)
            kt = kt_ref[pl.ds(row0, blk), :]
            if has_prev:
                kt = jnp.concatenate([kt_ref[pl.ds(row0 - blk, blk), :], kt], axis=1)
            s_ref[slot, w, :, 0:nk] = _dot(qq, kt)

    def softmax_phase(g, specs, has_prev, slot):
        nk = 2 * blk if has_prev else blk
        off = 0 if has_prev else blk
        for w, (_, tok0) in enumerate(specs):
            ms, ls = [], []
            for a in range(2):
                sa = s_ref[slot, w, a * blk:(a + 1) * blk, 0:nk] + bias_ref[a, :, off:off + nk]
                m = jnp.max(sa, axis=-1, keepdims=True)
                e = jnp.exp2(sa - m)
                p_ref[slot, w, a * blk:(a + 1) * blk, 0:nk] = e.astype(BF16)
                ms.append(m)
                ls.append(jnp.sum(e, axis=-1, keepdims=True))
            put(2, g, tok0, jnp.where(head_a, ms[0], ms[1]))
            put(1, g, tok0, jnp.where(head_a, ls[0], ls[1]))

    def pv_phase(g, specs, has_prev, slot):
        v_ref = qkv[g][2]
        nk = 2 * blk if has_prev else blk
        for w, (row0, tok0) in enumerate(specs):
            k_lo = row0 - blk if has_prev else row0
            o = _dot(p_ref[slot, w, :, 0:nk], v_ref[pl.ds(k_lo, nk), :])
            put(0, g, tok0, jnp.where(head_a, o[0:blk, :], o[blk:2 * blk, :]))

    def pipelined(n_iter, make_specs, g, has_prev):
        def step(t):
            static = isinstance(t, int)
            if (not static) or 0 <= t - 2 < n_iter:
                pv_phase(g, make_specs(t - 2), has_prev, t % 2)
            if (not static) or 0 <= t - 1 < n_iter:
                softmax_phase(g, make_specs(t - 1), has_prev, (t - 1) % 2)
            if (not static) or 0 <= t < n_iter:
                qk_phase(g, make_specs(t), has_prev, t % 2)

        for t in range(2):
            step(t)
        if n_iter > 2:
            def body(t, carry):
                step(t)
                return carry

            lax.fori_loop(2, n_iter, body, 0)
        for t in range(max(n_iter, 2), n_iter + 2):
            step(t)

    qi = lax.broadcasted_iota(jnp.int32, (blk, 2 * blk), 0)
    kj = lax.broadcasted_iota(jnp.int32, (blk, 2 * blk), 1)
    dist = blk + qi - kj
    valid = (dist >= 0) & (dist <= blk)

    for g, dil in enumerate(dils):
        sub = seq // dil
        n_blk = sub // blk
        bias_base = jnp.where(valid, -(dist * dil).astype(F32), NEG_BIG)
        for a in range(2):
            slope = slopes_ref[g * n_heads + 2 * hp + a]
            bias_ref[a] = (slope * LOG2E) * bias_base

        u1, u2 = unrolls[g]

        def aligned(row):
            return row if isinstance(row, int) else pl.multiple_of(row, blk)

        def first_specs(i, dil=dil, sub=sub, u1=u1):
            return [(aligned((i * u1 + w) * sub), _residue_of_slot(dil, i * u1 + w)) for w in range(u1)]

        def rest_specs(i, dil=dil, sub=sub, n_blk=n_blk, u2=u2):
            specs = []
            for w in range(u2):
                idx = i * u2 + w
                r = idx // (n_blk - 1)
                b = idx % (n_blk - 1) + 1
                specs.append((aligned(r * sub + b * blk), b * blk * dil + r))
            return specs

        pipelined(dil // u1, first_specs, g, False)
        n_rest = dil * (n_blk - 1)
        if n_rest:
            pipelined(n_rest // u2, rest_specs, g, True)
        if two_step(g):
            assert n_rest == 0
            unstage(g)

    rb = 256
    n_g = len(dils)

    def merge(i, carry):
        r0 = pl.multiple_of(i * rb, rb)
        ms = [max_ref[g, pl.ds(r0, rb), :] for g in range(n_g)]
        mx = ms[0]
        for g in range(1, n_g):
            mx = jnp.maximum(mx, ms[g])
        num = jnp.zeros((rb, LANES), F32)
        den = jnp.zeros((rb, LANES), F32)
        for g in range(n_g):
            w = jnp.exp2(ms[g] - mx)
            num = num + w * num_ref[g, pl.ds(r0, rb), :]
            den = den + w * den_ref[g, pl.ds(r0, rb), :]
        gate = g_ref[pl.ds(r0, rb), :].astype(F32)
        o_ref[pl.ds(r0, rb), :] = (num / den * gate).astype(BF16)
        return carry

    lax.fori_loop(0, seq // rb, merge, 0)


def _attention(proj, slopes):
    bsz, n_tiles, n_pairs, seq, _ = proj.shape
    assert n_tiles == 3 * B_N_GROUPS + 1
    dils = tuple(dil for _, dil in DILATED_GROUPS)
    unrolls = ((1, 5), (4, 6), (4, 1))
    for (u1, u2), dil in zip(unrolls, dils):
        n_rest = dil * (seq // dil // B_BLOCK - 1)
        assert dil % u1 == 0 and n_rest % u2 == 0

    def col_spec(tile):
        return pl.BlockSpec((None, None, None, seq, LANES), lambda b, h, s, tile=tile: (b, tile, h, 0, 0))

    kern = functools.partial(_attn_kernel, seq=seq, dils=dils, unrolls=unrolls)
    grid_spec = pltpu.PrefetchScalarGridSpec(
        num_scalar_prefetch=1,
        grid=(bsz, n_pairs),
        in_specs=[col_spec(t) for t in range(n_tiles)],
        out_specs=pl.BlockSpec((None, None, seq, LANES), lambda b, h, s: (b, h, 0, 0)),
        scratch_shapes=[pltpu.VMEM((B_N_GROUPS, seq, LANES), F32)] * 3
        + [pltpu.VMEM((2, B_BLOCK, 2 * B_BLOCK), F32),
           pltpu.VMEM((2, max(max(u) for u in unrolls), 2 * B_BLOCK, 2 * B_BLOCK), F32),
           pltpu.VMEM((2, max(max(u) for u in unrolls), 2 * B_BLOCK, 2 * B_BLOCK), BF16),
           pltpu.VMEM((3, seq, LANES), F32)],
    )
    return pl.pallas_call(
        kern,
        grid_spec=grid_spec,
        out_shape=jax.ShapeDtypeStruct((bsz, n_pairs, seq, LANES), BF16),
        compiler_params=_params(("arbitrary", "arbitrary")),
        name="dilated_attention",
    )(slopes, *([proj] * n_tiles))


def _gmlp_kernel(x_ref, yprev_ref, wprev_ref, nw_ref, w_ref, vw_ref, vb_ref, ws_ref, bias_ref, wout_ref, o_ref,
                 x1_ref, h_ref, u_ref, v_ref, g_ref, y_ref, *, tm):
    d = x_ref.shape[-1]
    cg = d // C_GROUPS
    rb = 256
    for i in range(tm // rb):
        rows = slice(i * rb, (i + 1) * rb)
        y_prev = jnp.concatenate([yprev_ref[hp, rows, :] for hp in range(yprev_ref.shape[0])], axis=1)
        x1_ref[rows, :] = x_ref[0, rows, :] + _dot(y_prev, wprev_ref[...])
        h_ref[rows, :] = _rms_rows(x1_ref[rows, :], nw_ref[...]).astype(BF16)
    for i in range(tm // rb):
        rows = slice(i * rb, (i + 1) * rb)
        hh = h_ref[rows, :]
        u_ref[rows, :] = _gelu_tanh(_dot(hh, w_ref[:, 0:d]))
        vv = _gelu_tanh(_dot(hh, w_ref[:, d:2 * d]))
        mu = jnp.mean(vv, axis=-1, keepdims=True)
        vc = vv - mu
        var = jnp.mean(vc * vc, axis=-1, keepdims=True)
        v_ref[rows, :] = (vc * lax.rsqrt(var + NORM_EPS) * vw_ref[...] + vb_ref[...]).astype(BF16)
        g_ref[rows, :] = _silu(_dot(hh, w_ref[:, 2 * d:3 * d]))
    ti = lax.broadcasted_iota(jnp.int32, (C_CHUNK, C_CHUNK), 0)
    si = lax.broadcasted_iota(jnp.int32, (C_CHUNK, C_CHUNK), 1)
    n_ch = tm // C_CHUNK
    for grp in range(C_GROUPS):
        cols = slice(grp * cg, (grp + 1) * cg)
        wc = jnp.where(si <= ti, ws_ref[grp], 0.0).astype(BF16)
        rhs = jnp.concatenate([v_ref[n * C_CHUNK:(n + 1) * C_CHUNK, cols] for n in range(n_ch)], axis=1)
        s = _dot(wc, rhs)
        bias = bias_ref[:, cols]
        for n in range(n_ch):
            rows = slice(n * C_CHUNK, (n + 1) * C_CHUNK)
            sn = s[:, n * cg:(n + 1) * cg] + bias
            y_ref[rows, cols] = (u_ref[rows, cols] * sn * g_ref[rows, cols]).astype(BF16)
    for i in range(tm // rb):
        rows = slice(i * rb, (i + 1) * rb)
        o_ref[0, rows, :] = x1_ref[rows, :] + _dot(y_ref[rows, :], wout_ref[...])


def _gmlp_mixer(x, y_prev_pairs, w_prev_bf16, norm_w_row, w_bf16, vw_row, vb_row, w_s, bias_full, w_out_bf16,
                tm=512):
    bsz, seq, d = x.shape
    n_pairs, lanes = y_prev_pairs.shape[1], y_prev_pairs.shape[3]
    kern = functools.partial(_gmlp_kernel, tm=tm)
    return pl.pallas_call(
        kern,
        grid=(bsz, seq // tm),
        in_specs=[pl.BlockSpec((1, tm, d), lambda b, i: (b, i, 0)),
                  pl.BlockSpec((None, n_pairs, tm, lanes), lambda b, i: (b, 0, i, 0)),
                  pl.BlockSpec((d, d), lambda b, i: (0, 0)),
                  pl.BlockSpec((1, d), lambda b, i: (0, 0)),
                  pl.BlockSpec((d, 3 * d), lambda b, i: (0, 0)),
                  pl.BlockSpec((1, d), lambda b, i: (0, 0)),
                  pl.BlockSpec((1, d), lambda b, i: (0, 0)),
                  pl.BlockSpec(w_s.shape, lambda b, i: (0, 0, 0)),
                  pl.BlockSpec((C_CHUNK, d), lambda b, i: (0, 0)),
                  pl.BlockSpec((d, d), lambda b, i: (0, 0))],
        out_specs=pl.BlockSpec((1, tm, d), lambda b, i: (b, i, 0)),
        out_shape=jax.ShapeDtypeStruct((bsz, seq, d), F32),
        scratch_shapes=[pltpu.VMEM((tm, d), F32),
                        pltpu.VMEM((tm, d), BF16),
                        pltpu.VMEM((tm, d), F32),
                        pltpu.VMEM((tm, d), BF16),
                        pltpu.VMEM((tm, d), F32),
                        pltpu.VMEM((tm, d), BF16)],
        compiler_params=_params(("arbitrary", "arbitrary")),
        name="gmlp_mixer",
    )(x, y_prev_pairs, w_prev_bf16, norm_w_row, w_bf16, vw_row, vb_row, w_s, bias_full, w_out_bf16)


def kernel(x, norm_w, a_w_in, a_lower_bounds, a_o_norm_w, a_w_out, b_w_in, b_q_norm_w, b_k_norm_w, b_w_out,
           c_w_in, c_v_norm_w, c_v_norm_b, c_w_s, c_b_s, c_w_out):
    bsz, seq, d = x.shape
    depth = norm_w.shape[0]
    n_mixers = 3
    assert seq == DILATED_GROUPS[-1][0] and seq % A_CHUNK == 0 and d % LANES == 0

    def residual(y, w_out, xin):
        out = _out_proj(y.reshape(bsz * seq, d), w_out.astype(BF16), xin.reshape(bsz * seq, d))
        return out.reshape(bsz, seq, d)

    pending = None
    for layer in range(depth):
        kind, idx = layer % n_mixers, layer // n_mixers
        nw = norm_w[layer][None, :]
        if kind == 0:
            y = _hgrn_mixer(x, nw, a_w_in[idx].astype(BF16), a_lower_bounds, a_o_norm_w[idx][None, :], idx)
            x = residual(y, a_w_out[idx], x)
        elif kind == 1:
            n_heads = d // B_HEAD_DIM
            n_total = B_N_GROUPS * n_heads
            slopes = jnp.exp2(-ALIBI_MAX_EXP * jnp.arange(1, n_total + 1, dtype=F32) / n_total)
            rows = []
            for g in range(B_N_GROUPS):
                rows.append(jnp.tile(b_q_norm_w[idx, g] * b_k_norm_w[idx, g], n_heads)
                            * (B_HEAD_DIM ** -0.5 * LOG2E))
                rows.append(jnp.ones((d,), F32))
                rows.append(jnp.ones((d,), F32))
            rows.append(jnp.ones((d,), F32))
            head_w = jnp.stack(rows)[:, None, :]
            proj = _attn_in_proj(x, nw, b_w_in[idx], head_w)
            pending = (_attention(proj, slopes), b_w_out[idx].astype(BF16))
            if layer == depth - 1:
                x = _out_proj_pairs(*pending, x)
        else:
            bias_full = jnp.repeat(c_b_s[idx].T, d // C_GROUPS, axis=1)
            x = _gmlp_mixer(x, *pending, nw, c_w_in[idx].astype(BF16), c_v_norm_w[idx][None, :],
                            c_v_norm_b[idx][None, :], c_w_s[idx], bias_full, c_w_out[idx].astype(BF16))
    return x
```
